```python
import math
import jax, jax.numpy as jnp
from jax import lax
import numpy as np

D_MODEL = 1024
BATCH = 1
SEQ = 16384
DEPTH = 4
DEC_BATCH = 4
DEC_SEQ = 8192
PAST_LEN = 128

N_MEM = 256
EPS = 1e-6
GDN_HEADS = 4
GDN_DK = 128
GDN_DV = 128
GDN_CONV = 5
GDN_CHUNK = 64
DIFF_HEADS = 8
DIFF_DH = 64
DIFF_DV = 2 * DIFF_DH
Q_BLOCK = 128
N_BUCKETS = 32
MAX_DISTANCE = 128
X_HEADS = 4
X_DH = 128
D_FF = 2816

GDN_QK = GDN_HEADS * GDN_DK
GDN_V = GDN_HEADS * GDN_DV
DIFF_QK = DIFF_HEADS * 2 * DIFF_DH
DIFF_VW = DIFF_HEADS * DIFF_DV
IN_SPLITS = (GDN_QK, GDN_QK, GDN_V, 2 * GDN_HEADS, 2 * GDN_HEADS, GDN_V,
             DIFF_QK, DIFF_QK, DIFF_VW, D_MODEL, D_MODEL)
N_IN = sum(IN_SPLITS)

kernel_name = "hybrid_gdn_diffattn_encoder"


def rms_norm(x, w):
    xf = x.astype(jnp.float32)
    y = xf * lax.rsqrt(jnp.mean(xf * xf, axis=-1, keepdims=True) + EPS)
    return (y * w.astype(jnp.float32)).astype(x.dtype)


def l2norm(x):
    return x * lax.rsqrt(jnp.sum(x * x, axis=-1, keepdims=True) + EPS)


def swiglu_ffn(x, norm_w, w_gate, w_up, w_down):
    h = rms_norm(x, norm_w)
    return (jax.nn.silu(h @ w_gate) * (h @ w_up)) @ w_down


def split_projection(proj):
    offsets, acc = [], 0
    for s in IN_SPLITS[:-1]:
        acc += s
        offsets.append(acc)
    return jnp.split(proj, offsets, axis=-1)


def centred_depthwise_conv(x, w):
    pad = (GDN_CONV - 1) // 2
    return lax.conv_general_dilated(
        x, w[:, None, :].astype(x.dtype), window_strides=(1,), padding=[(pad, pad)],
        dimension_numbers=("NWC", "WIO", "NWC"), feature_group_count=x.shape[-1])


def gated_delta_rule(q, k, v, g, beta):
    b, h, l, dk = q.shape
    dv = v.shape[-1]
    c = GDN_CHUNK
    n = l // c
    q = q.reshape(b, h, n, c, dk)
    k = k.reshape(b, h, n, c, dk)
    v = v.reshape(b, h, n, c, dv)
    g = g.reshape(b, h, n, c)
    beta = beta.reshape(b, h, n, c)
    gc = jnp.cumsum(g, axis=-1)
    incl = jnp.tril(jnp.ones((c, c), bool))
    strict = jnp.tril(jnp.ones((c, c), bool), -1)
    diff = gc[..., :, None] - gc[..., None, :]
    gamma = jnp.where(incl, jnp.exp(jnp.where(incl, diff, 0.0)), 0.0)
    kb = k * beta[..., None]
    vb = v * beta[..., None]
    lower = jnp.where(strict, jnp.einsum("bhnid,bhnjd->bhnij", kb, k) * gamma, 0.0)
    a_mat = lower + jnp.eye(c, dtype=lower.dtype)
    rhs = jnp.concatenate([vb, kb * jnp.exp(gc)[..., None]], axis=-1)
    sol = lax.linalg.triangular_solve(a_mat, rhs, left_side=True, lower=True,
                                      unit_diagonal=True)
    u, w = sol[..., :dv], sol[..., dv:]
    a_qk = jnp.einsum("bhnid,bhnjd->bhnij", q, k) * gamma
    q_dec = q * jnp.exp(gc)[..., None]
    g_last = gc[..., -1]
    k_dec = k * jnp.exp(g_last[..., None] - gc)[..., None]

    def step(s, xs):
        u_c, w_c, qd_c, aqk_c, kd_c, gl_c = xs
        v_new = u_c - jnp.einsum("bhcd,bhde->bhce", w_c, s)
        o = (jnp.einsum("bhcd,bhde->bhce", qd_c, s)
             + jnp.einsum("bhij,bhje->bhie", aqk_c, v_new))
        s = s * jnp.exp(gl_c)[..., None, None] + jnp.einsum("bhcd,bhce->bhde", kd_c, v_new)
        return s, o

    xs = tuple(jnp.moveaxis(t, 2, 0) for t in (u, w, q_dec, a_qk, k_dec, g_last))
    s0 = jnp.zeros((b, h, dk, dv), jnp.float32)
    _, o = lax.scan(step, s0, xs)
    return jnp.moveaxis(o, 0, 2).reshape(b, h, l, dv)


def gdn_branch(q, k, v, beta_logit, a_logit, z, conv_w, a_log, dt_bias, norm_w):
    b, l, _ = q.shape
    qkv = jax.nn.silu(centred_depthwise_conv(jnp.concatenate([q, k, v], axis=-1), conv_w))
    qkv = qkv.astype(jnp.float32)
    to_heads = lambda t, d: t.reshape(b, l, GDN_HEADS, d).transpose(0, 2, 1, 3)
    qa = l2norm(to_heads(qkv[..., :GDN_QK], GDN_DK)) * (GDN_DK ** -0.5)
    ka = l2norm(to_heads(qkv[..., GDN_QK:2 * GDN_QK], GDN_DK))
    va = to_heads(qkv[..., 2 * GDN_QK:], GDN_DV)
    beta = jax.nn.sigmoid(beta_logit.astype(jnp.float32)).reshape(b, l, 2, GDN_HEADS)
    beta = beta.transpose(2, 0, 3, 1)
    a_in = a_logit.astype(jnp.float32).reshape(b, l, 2, GDN_HEADS).transpose(2, 0, 3, 1)
    g = (-jnp.exp(a_log.astype(jnp.float32))[:, None, :, None]
         * jax.nn.softplus(a_in + dt_bias.astype(jnp.float32)[:, None, :, None]))
    o_f = gated_delta_rule(qa, ka, va, g[0], beta[0])
    flip = lambda t: jnp.flip(t, axis=2)
    o_b = flip(gated_delta_rule(flip(qa), flip(ka), flip(va), flip(g[1]), flip(beta[1])))
    o = (o_f + o_b).transpose(0, 2, 1, 3)
    zf = z.reshape(b, l, GDN_HEADS, GDN_DV)
    o = rms_norm(o, norm_w) * jax.nn.silu(zf.astype(jnp.float32))
    return o.reshape(b, l, GDN_V).astype(q.dtype)


def rel_bucket(rel):
    nb = N_BUCKETS // 2
    ret = jnp.where(rel > 0, nb, 0)
    n = jnp.abs(rel)
    max_exact = nb // 2
    nf = jnp.maximum(n, 1).astype(jnp.float32)
    large = max_exact + (jnp.log(nf / max_exact) / math.log(MAX_DISTANCE / max_exact)
                         * (nb - max_exact)).astype(jnp.int32)
    large = jnp.minimum(large, nb - 1)
    return ret + jnp.where(n < max_exact, n, large)


def diff_attention_branch(q, k, v, lam, lam_init, norm_w, rel_bias):
    b, l, _ = q.shape
    qh = q.reshape(b, l, DIFF_HEADS, 2, DIFF_DH).transpose(3, 0, 2, 1, 4)
    kh = k.reshape(b, l, DIFF_HEADS, 2, DIFF_DH).transpose(3, 0, 2, 1, 4)
    vh = v.reshape(b, l, DIFF_HEADS, DIFF_DV).transpose(0, 2, 1, 3)
    lamf = lam.astype(jnp.float32)
    lam_full = (jnp.exp(jnp.sum(lamf[0] * lamf[1])) - jnp.exp(jnp.sum(lamf[2] * lamf[3]))
                + lam_init)
    nq = l // Q_BLOCK
    q_blocks = qh.reshape(2, b, DIFF_HEADS, nq, Q_BLOCK, DIFF_DH).transpose(3, 0, 1, 2, 4, 5)
    k_pos = jnp.arange(l, dtype=jnp.int32)
    scale = DIFF_DH ** -0.5

    def block(args):
        qb, start = args
        q_pos = start + jnp.arange(Q_BLOCK, dtype=jnp.int32)
        bias = rel_bias[rel_bucket(k_pos[None, :] - q_pos[:, None])]
        bias = jnp.moveaxis(bias, -1, 0).astype(jnp.float32)
        logits = jnp.einsum("mbhqd,mbhkd->mbhqk", qb, kh).astype(jnp.float32) * scale + bias
        p = jax.nn.softmax(logits, axis=-1)
        attn = p[0] - lam_full * p[1]
        return jnp.einsum("bhqk,bhkd->bhqd", attn.astype(vh.dtype), vh)

    starts = jnp.arange(nq, dtype=jnp.int32) * Q_BLOCK
    o = lax.map(block, (q_blocks, starts))
    o = o.transpose(1, 2, 0, 3, 4).reshape(b, DIFF_HEADS, l, DIFF_DV).transpose(0, 2, 1, 3)
    o = rms_norm(o, norm_w) * (1.0 - lam_init)
    return o.reshape(b, l, DIFF_VW)


def memory_cross_attention(x, mem, norm_w, mem_norm_w, wq, wkv, wo):
    b, l, _ = x.shape
    m_len = mem.shape[1]
    h = rms_norm(x, norm_w)
    m = rms_norm(mem, mem_norm_w)
    q = (h @ wq).reshape(b, l, X_HEADS, X_DH)
    kv = (m @ wkv).reshape(b, m_len, 2, X_HEADS, X_DH)
    logits = jnp.einsum("blhd,bmhd->bhlm", q, kv[:, :, 0]).astype(jnp.float32) * (X_DH ** -0.5)
    p = jax.nn.softmax(logits, axis=-1)
    o = jnp.einsum("bhlm,bmhd->blhd", p.astype(x.dtype), kv[:, :, 1])
    return o.reshape(b, l, X_HEADS * X_DH) @ wo


def encoder(x, mem, p):
    for i in range(DEPTH):
        x = x + 0.5 * swiglu_ffn(x, p["ffn1_norm"][i], p["ffn1_w_gate"][i],
                                 p["ffn1_w_up"][i], p["ffn1_w_down"][i])
        u = rms_norm(x, p["mix_norm"][i])
        (qa, ka, va, beta_l, a_l, z, qb, kb, vb, ga, gb) = split_projection(u @ p["w_in"][i])
        oa = gdn_branch(qa, ka, va, beta_l, a_l, z, p["conv_w"][i], p["gdn_a_log"][i],
                        p["gdn_dt_bias"][i], p["gdn_norm"][i])
        lam_init = 0.8 - 0.6 * math.exp(-0.3 * i)
        ob = diff_attention_branch(qb, kb, vb, p["diff_lambda"][i], lam_init,
                                   p["diff_norm"][i], p["rel_bias"])
        merged = (jax.nn.sigmoid(ga) * (oa @ p["w_up_a"][i])
                  + jax.nn.sigmoid(gb) * (ob @ p["w_up_b"][i]))
        x = x + merged @ p["w_out"][i]
        x = x + memory_cross_attention(x, mem, p["xattn_norm"][i], p["mem_norm"][i],
                                       p["xattn_wq"][i], p["xattn_wkv"][i], p["xattn_wo"][i])
        x = x + 0.5 * swiglu_ffn(x, p["ffn2_norm"][i], p["ffn2_w_gate"][i],
                                 p["ffn2_w_up"][i], p["ffn2_w_down"][i])
    return rms_norm(x, p["final_norm"])


def setup_inputs(seed: int = 0) -> dict:
    key = jax.random.key(seed)
    ks = list(jax.random.split(key, 40))
    f32 = jnp.float32
    nxt = lambda: ks.pop()
    nrm = lambda shape, scale: jax.random.normal(nxt(), shape, f32) * scale
    gain = lambda shape: 1.0 + 0.02 * jax.random.normal(nxt(), shape, f32)
    L = DEPTH
    dt = jnp.exp(jax.random.uniform(nxt(), (L, 2, GDN_HEADS), f32,
                                    math.log(1e-3), math.log(1e-1)))
    return {
        "x_prompt": nrm((BATCH, SEQ, D_MODEL), 1.0),
        "x_sample": nrm((DEC_BATCH, DEC_SEQ, D_MODEL), 1.0),
        "mem_prompt": nrm((BATCH, N_MEM, D_MODEL), 1.0),
        "mem_sample": nrm((DEC_BATCH, N_MEM, D_MODEL), 1.0),
        "ffn1_norm": gain((L, D_MODEL)),
        "ffn1_w_gate": nrm((L, D_MODEL, D_FF), D_MODEL ** -0.5),
        "ffn1_w_up": nrm((L, D_MODEL, D_FF), D_MODEL ** -0.5),
        "ffn1_w_down": nrm((L, D_FF, D_MODEL), D_FF ** -0.5),
        "mix_norm": gain((L, D_MODEL)),
        "w_in": nrm((L, D_MODEL, N_IN), D_MODEL ** -0.5),
        "conv_w": nrm((L, GDN_CONV, 2 * GDN_QK + GDN_V), GDN_CONV ** -0.5),
        "gdn_a_log": jnp.log(jax.random.uniform(nxt(), (L, 2, GDN_HEADS), f32, 1.0, 16.0)),
        "gdn_dt_bias": dt + jnp.log(-jnp.expm1(-dt)),
        "gdn_norm": gain((L, GDN_DV)),
        "w_up_a": nrm((L, GDN_V, D_MODEL), GDN_V ** -0.5),
        "diff_lambda": nrm((L, 4, DIFF_DH), 0.1),
        "diff_norm": gain((L, DIFF_DV)),
        "w_up_b": nrm((L, DIFF_VW, D_MODEL), DIFF_VW ** -0.5),
        "w_out": nrm((L, D_MODEL, D_MODEL), D_MODEL ** -0.5),
        "rel_bias": nrm((N_BUCKETS, DIFF_HEADS), 0.5),
        "xattn_norm": gain((L, D_MODEL)),
        "mem_norm": gain((L, D_MODEL)),
        "xattn_wq": nrm((L, D_MODEL, X_HEADS * X_DH), D_MODEL ** -0.5),
        "xattn_wkv": nrm((L, D_MODEL, 2 * X_HEADS * X_DH), D_MODEL ** -0.5),
        "xattn_wo": nrm((L, X_HEADS * X_DH, D_MODEL), (X_HEADS * X_DH) ** -0.5),
        "ffn2_norm": gain((L, D_MODEL)),
        "ffn2_w_gate": nrm((L, D_MODEL, D_FF), D_MODEL ** -0.5),
        "ffn2_w_up": nrm((L, D_MODEL, D_FF), D_MODEL ** -0.5),
        "ffn2_w_down": nrm((L, D_FF, D_MODEL), D_FF ** -0.5),
        "final_norm": gain((D_MODEL,)),
    }


def reference(x_prompt, x_sample, mem_prompt, mem_sample,
              ffn1_norm, ffn1_w_gate, ffn1_w_up, ffn1_w_down,
              mix_norm, w_in, conv_w, gdn_a_log, gdn_dt_bias, gdn_norm, w_up_a,
              diff_lambda, diff_norm, w_up_b, w_out, rel_bias,
              xattn_norm, mem_norm, xattn_wq, xattn_wkv, xattn_wo,
              ffn2_norm, ffn2_w_gate, ffn2_w_up, ffn2_w_down, final_norm):
    params = {
        "ffn1_norm": ffn1_norm, "ffn1_w_gate": ffn1_w_gate, "ffn1_w_up": ffn1_w_up,
        "ffn1_w_down": ffn1_w_down, "mix_norm": mix_norm, "w_in": w_in, "conv_w": conv_w,
        "gdn_a_log": gdn_a_log, "gdn_dt_bias": gdn_dt_bias, "gdn_norm": gdn_norm,
        "w_up_a": w_up_a, "diff_lambda": diff_lambda, "diff_norm": diff_norm,
        "w_up_b": w_up_b, "w_out": w_out, "rel_bias": rel_bias,
        "xattn_norm": xattn_norm, "mem_norm": mem_norm, "xattn_wq": xattn_wq,
        "xattn_wkv": xattn_wkv, "xattn_wo": xattn_wo, "ffn2_norm": ffn2_norm,
        "ffn2_w_gate": ffn2_w_gate, "ffn2_w_up": ffn2_w_up, "ffn2_w_down": ffn2_w_down,
        "final_norm": final_norm,
    }
    y_prompt = encoder(x_prompt, mem_prompt, params)
    y_sample = encoder(x_sample, mem_sample, params)
    return (y_prompt, y_sample)
```

```python
import functools
import math

import jax
import jax.numpy as jnp
from jax import lax
from jax.experimental import pallas as pl
from jax.experimental.pallas import tpu as pltpu

F32 = jnp.float32
BF16 = jnp.bfloat16
HIGHEST = lax.Precision.HIGHEST

EPS = 1e-6
D_MODEL = 1024
N_MEM = 256
GDN_HEADS = 4
GDN_DK = 128
GDN_DV = 128
GDN_CONV = 5
GDN_CHUNK = 64
DIFF_HEADS = 8
DIFF_DH = 64
DIFF_DV = 2 * DIFF_DH
N_BUCKETS = 32
MAX_DISTANCE = 128
X_HEADS = 4
X_DH = 128
D_FF = 2816
GDN_QK = GDN_HEADS * GDN_DK
GDN_V = GDN_HEADS * GDN_DV
DIFF_QK = DIFF_HEADS * 2 * DIFF_DH
DIFF_VW = DIFF_HEADS * DIFF_DV
N_GATES = 2 * GDN_HEADS

V7X_LANES = 128
V7X_SUBLANES = 8
V7X_VMEM_LIMIT = 56 * 1024 * 1024

FFN_CHUNK = 256
HALO = V7X_SUBLANES


def _cparams(sem):
    return pltpu.CompilerParams(dimension_semantics=sem, vmem_limit_bytes=V7X_VMEM_LIMIT)


def _dot(a, b, precision=None):
    return jnp.dot(a, b, preferred_element_type=F32, precision=precision)


def _dot_nt(a, b, precision=None):
    return lax.dot_general(a, b, (((1,), (1,)), ((), ())),
                           preferred_element_type=F32, precision=precision)


def _dot_tn(a, b, precision=None):
    return lax.dot_general(a, b, (((0,), (0,)), ((), ())),
                           preferred_element_type=F32, precision=precision)


def _rms(x, w):
    return x * lax.rsqrt(jnp.mean(x * x, axis=-1, keepdims=True) + EPS) * w


def _resident(shape):
    nd = len(shape)
    return pl.BlockSpec(shape, lambda *_: (0,) * nd, pipeline_mode=pl.Buffered(1))


def _ffn_body(x_ref, nw_ref, wg_ref, wu_ref, wd_ref, *rest, n_chunks, final):
    if final:
        fn_ref, o_ref, h_ref, acc_ref = rest
    else:
        o_ref, h_ref, acc_ref = rest
    x = x_ref[...]
    h_ref[...] = _rms(x, nw_ref[...]).astype(BF16)
    acc_ref[...] = jnp.zeros_like(acc_ref)

    def chunk(c, carry):
        h = h_ref[...]
        g = _dot(h, wg_ref[c])
        u = _dot(h, wu_ref[c])
        a = (g * jax.nn.sigmoid(g) * u).astype(BF16)
        acc_ref[...] += _dot(a, wd_ref[c])
        return carry

    lax.fori_loop(0, n_chunks, chunk, 0)
    y = x + 0.5 * acc_ref[...]
    if final:
        y = _rms(y, fn_ref[...])
    o_ref[...] = y


def _ffn(x, nw, wg, wu, wd, final_w, tm):
    t, d = x.shape
    nf = wg.shape[0]
    row = pl.BlockSpec((tm, d), lambda i: (i, 0))
    in_specs = [row, _resident((1, d)), _resident(wg.shape), _resident(wu.shape),
                _resident(wd.shape)]
    args = [x, nw, wg, wu, wd]
    if final_w is not None:
        in_specs.append(_resident((1, d)))
        args.append(final_w)
    return pl.pallas_call(
        functools.partial(_ffn_body, n_chunks=nf, final=final_w is not None),
        grid=(t // tm,),
        in_specs=in_specs,
        out_specs=row,
        out_shape=jax.ShapeDtypeStruct((t, d), F32),
        scratch_shapes=[pltpu.VMEM((tm, d), BF16), pltpu.VMEM((tm, d), F32)],
        compiler_params=_cparams(("parallel",)),
        name="ffn",
    )(*args)


_P_QKVA = (0, 3 * GDN_QK)
_P_GATE = (_P_QKVA[1], _P_QKVA[1] + V7X_LANES)
_P_Z = (_P_GATE[1], _P_GATE[1] + GDN_V)
_P_QB = (_P_Z[1], _P_Z[1] + DIFF_QK)
_P_KB = (_P_QB[1], _P_QB[1] + DIFF_QK)
_P_VB = (_P_KB[1], _P_KB[1] + DIFF_VW)
_P_SG = (_P_VB[1], _P_VB[1] + 2 * D_MODEL)
_P_COLS = _P_SG[1]


def _proj_body(x_ref, nw_ref, w_ref, qkva_ref, gate_ref, z_ref, qb_ref, kb_ref, vb_ref, sg_ref):
    u = _rms(x_ref[...], nw_ref[...]).astype(BF16)

    def seg(span):
        return _dot(u, w_ref[:, span[0]:span[1]])

    qkva_ref[...] = seg(_P_QKVA)
    gate_ref[...] = seg(_P_GATE)
    z_ref[...] = seg(_P_Z)
    qb_ref[...] = seg(_P_QB).astype(BF16)
    kb_ref[...] = seg(_P_KB).astype(BF16)
    vb_ref[...] = seg(_P_VB).astype(BF16)
    sg_ref[...] = jax.nn.sigmoid(seg(_P_SG))


def _proj(x, nw, w, tm):
    t, d = x.shape
    widths = [(_P_QKVA, F32), (_P_GATE, F32), (_P_Z, F32), (_P_QB, BF16), (_P_KB, BF16),
              (_P_VB, BF16), (_P_SG, F32)]
    out_shape = [jax.ShapeDtypeStruct((t, s[1] - s[0]), dt) for s, dt in widths]
    out_specs = [pl.BlockSpec((tm, s[1] - s[0]), lambda i: (i, 0)) for s, _ in widths]
    return pl.pallas_call(
        _proj_body,
        grid=(t // tm,),
        in_specs=[pl.BlockSpec((tm, d), lambda i: (i, 0)), _resident((1, d)), _resident(w.shape)],
        out_specs=out_specs,
        out_shape=out_shape,
        compiler_params=_cparams(("parallel",)),
        name="proj",
    )(x, nw, w)


def _gdnprep_body(cur_ref, prev_ref, next_ref, cw_ref, gate_ref, alog_ref, dtb_ref,
                  qkv_ref, bg_ref, *, tm, seq_tiles):
    i = pl.program_id(0)
    first = jnp.bool_(False)
    last = jnp.bool_(False)
    for start, per_seq in seq_tiles:
        rel = i - start
        first = first | ((rel >= 0) & (rel % per_seq == 0))
        last = last | ((rel >= 0) & (rel % per_seq == per_seq - 1))
    prev = jnp.where(first, 0.0, prev_ref[...])
    nxt = jnp.where(last, 0.0, next_ref[...])
    ext = jnp.concatenate([prev, cur_ref[...], nxt], axis=0)
    pad = (GDN_CONV - 1) // 2
    acc = None
    for k in range(GDN_CONV):
        lo = HALO - pad + k
        term = ext[lo:lo + tm, :] * cw_ref[k:k + 1, :]
        acc = term if acc is None else acc + term
    y = acc * jax.nn.sigmoid(acc)
    for h in range(2 * GDN_HEADS):
        lo = h * GDN_DK
        yh = y[:, lo:lo + GDN_DK]
        nrm = yh * lax.rsqrt(jnp.sum(yh * yh, axis=-1, keepdims=True) + EPS)
        if h < GDN_HEADS:
            nrm = nrm * (GDN_DK ** -0.5)
        qkv_ref[:, lo:lo + GDN_DK] = nrm
    qkv_ref[:, 2 * GDN_QK:] = y[:, 2 * GDN_QK:]
    gl = gate_ref[...]
    beta = jax.nn.sigmoid(gl)
    xa = gl + dtb_ref[...]
    softplus = jnp.maximum(xa, 0.0) + jnp.log1p(jnp.exp(-jnp.abs(xa)))
    g = -jnp.exp(alog_ref[...]) * softplus
    lane = lax.broadcasted_iota(jnp.int32, gl.shape, 1)
    bg_ref[...] = jnp.where(lane < N_GATES, beta, g)


def _gdnprep(qkva, conv_w8, gates, alog_pad, dtb_pad, tm, seq_tiles):
    t, c = qkva.shape
    hb = tm // HALO
    nblk = t // HALO
    return pl.pallas_call(
        functools.partial(_gdnprep_body, tm=tm, seq_tiles=seq_tiles),
        grid=(t // tm,),
        in_specs=[
            pl.BlockSpec((tm, c), lambda i: (i, 0)),
            pl.BlockSpec((HALO, c), lambda i: (jnp.maximum(i * hb - 1, 0), 0)),
            pl.BlockSpec((HALO, c), lambda i: (jnp.minimum((i + 1) * hb, nblk - 1), 0)),
            _resident(conv_w8.shape),
            pl.BlockSpec((tm, V7X_LANES), lambda i: (i, 0)),
            _resident((1, V7X_LANES)),
            _resident((1, V7X_LANES)),
        ],
        out_specs=[pl.BlockSpec((tm, c), lambda i: (i, 0)),
                   pl.BlockSpec((tm, V7X_LANES), lambda i: (i, 0))],
        out_shape=[jax.ShapeDtypeStruct((t, c), F32),
                   jax.ShapeDtypeStruct((t, V7X_LANES), F32)],
        compiler_params=_cparams(("parallel",)),
        name="gdnprep",
    )(qkva, qkva, qkva, conv_w8, gates, alog_pad, dtb_pad)


def _gdn_body(qkv_f, bgc_f, bgr_f, qkv_b, bgc_b, bgr_b, *rest, n_chunks):
    of_ref, ob_ref, s_ref = rest[-3:]
    c64 = GDN_CHUNK

    @pl.when(pl.program_id(1) == 0)
    def _():
        s_ref[...] = jnp.zeros_like(s_ref)

    row = lax.broadcasted_iota(jnp.int32, (c64, c64), 0)
    col = lax.broadcasted_iota(jnp.int32, (c64, c64), 1)
    eye = (row == col).astype(F32)
    incl = (row >= col, row <= col)
    strict = (row > col, row < col)

    for c in range(n_chunks):
        for d in range(2):
            qkv_ref, bgc_ref, bgr_ref, o_ref = (
                (qkv_f, bgc_f, bgr_f, of_ref), (qkv_b, bgc_b, bgr_b, ob_ref))[d]
            r0 = (c if d == 0 else n_chunks - 1 - c) * c64
            tri_c = incl[d].astype(F32)
            tri_r = incl[1 - d].astype(F32)
            gt = bgc_ref[r0:r0 + c64, :]
            gcs = _dot(tri_c, gt, HIGHEST)
            grs = _dot(bgr_ref[:, r0:r0 + c64], tri_r, HIGHEST)
            g_last = gcs[c64 - 1:c64, :] if d == 0 else gcs[0:1, :]
            egc = jnp.exp(gcs)
            ekd = jnp.exp(g_last - gcs)
            egl = jnp.exp(g_last)
            for h in range(GDN_HEADS):
                idx = d * GDN_HEADS + h
                gi = N_GATES + idx
                q = qkv_ref[r0:r0 + c64, h * GDN_DK:(h + 1) * GDN_DK]
                k = qkv_ref[r0:r0 + c64, GDN_QK + h * GDN_DK:GDN_QK + (h + 1) * GDN_DK]
                v = qkv_ref[r0:r0 + c64, 2 * GDN_QK + h * GDN_DV:2 * GDN_QK + (h + 1) * GDN_DV]
                beta = gt[:, idx:idx + 1]
                diff = gcs[:, gi:gi + 1] - grs[gi:gi + 1, :]
                gam = jnp.where(incl[d], jnp.exp(jnp.where(incl[d], diff, 0.0)), 0.0)
                kb = k * beta
                vb = v * beta
                k16 = k.astype(BF16)
                kk = _dot_nt(kb.astype(BF16), k16)
                m = -jnp.where(strict[d], kk * gam, 0.0)
                x = eye + m
                mp = m
                for _ in range(5):
                    mp = _dot(mp, mp, HIGHEST)
                    x = x + _dot(x, mp, HIGHEST)
                rhs = jnp.concatenate([vb, kb * egc[:, gi:gi + 1]], axis=1)
                sol = _dot(x, rhs, HIGHEST)
                u = sol[:, :GDN_DV]
                w = sol[:, GDN_DV:]
                aqk = _dot_nt(q.astype(BF16), k16) * gam
                qd = q * egc[:, gi:gi + 1]
                kd = k * ekd[:, gi:gi + 1]
                s = s_ref[idx]
                s16 = s.astype(BF16)
                v_new = u - _dot(w.astype(BF16), s16)
                vn16 = v_new.astype(BF16)
                o = _dot(qd.astype(BF16), s16) + _dot(aqk.astype(BF16), vn16)
                s_ref[idx] = s * egl[:, gi:gi + 1] + _dot_tn(kd.astype(BF16), vn16)
                o_ref[r0:r0 + c64, h * GDN_DV:(h + 1) * GDN_DV] = o


def _gdn(qkvn, bg, bg_rows, row_off, batch, seqlen, ts, prev):
    t, c = qkvn.shape
    ns = seqlen // ts
    off = row_off // ts
    fwd = lambda b, s: (off + b * ns + s, 0)
    bwd = lambda b, s: (off + b * ns + ns - 1 - s, 0)
    fwd_r = lambda b, s: (0, off + b * ns + s)
    bwd_r = lambda b, s: (0, off + b * ns + ns - 1 - s)
    nr = bg_rows.shape[0]
    out = jax.ShapeDtypeStruct((t, GDN_V), F32)
    in_specs = [
        pl.BlockSpec((ts, c), fwd), pl.BlockSpec((ts, V7X_LANES), fwd),
        pl.BlockSpec((nr, ts), fwd_r),
        pl.BlockSpec((ts, c), bwd), pl.BlockSpec((ts, V7X_LANES), bwd),
        pl.BlockSpec((nr, ts), bwd_r),
    ]
    args = [qkvn, bg, bg_rows, qkvn, bg, bg_rows]
    aliases = {}
    if prev is not None:
        aliases = {len(args): 0, len(args) + 1: 1}
        in_specs += [pl.BlockSpec(memory_space=pl.ANY)] * 2
        args += list(prev)
    return pl.pallas_call(
        functools.partial(_gdn_body, n_chunks=ts // GDN_CHUNK),
        grid=(batch, ns),
        in_specs=in_specs,
        out_specs=[pl.BlockSpec((ts, GDN_V), fwd), pl.BlockSpec((ts, GDN_V), bwd)],
        out_shape=[out, out],
        scratch_shapes=[pltpu.VMEM((2 * GDN_HEADS, GDN_DK, GDN_DV), F32)],
        input_output_aliases=aliases,
        compiler_params=_cparams(("parallel", "arbitrary")),
        name="gdn",
    )(*args)


def _attn_body(far_ref, q_ref, k_ref, v_ref, band_ref, lam_ref, nw_ref, *rest, lam_init):
    o_ref, qz_ref, m_ref, l_ref, acc_ref = rest[-5:]
    h = pl.program_id(1)
    qi = pl.program_id(2)
    ki = pl.program_id(3)
    nk = pl.num_programs(3)

    @pl.when(ki == 0)
    def _():
        m_ref[...] = jnp.full_like(m_ref, -jnp.inf)
        l_ref[...] = jnp.zeros_like(l_ref)
        acc_ref[...] = jnp.zeros_like(acc_ref)
        q = q_ref[...]
        lane = lax.broadcasted_iota(jnp.int32, q.shape, 1)
        zero = jnp.zeros_like(q)
        qz_ref[0] = jnp.where(lane < DIFF_DH, q, zero)
        qz_ref[1] = jnp.where(lane >= DIFF_DH, q, zero)

    def step(add_bias):
        k = k_ref[...]
        v = v_ref[...]
        for mp in range(2):
            s = add_bias(_dot_nt(qz_ref[mp], k))
            m_prev = m_ref[mp]
            m_new = jnp.maximum(m_prev, jnp.max(s, axis=-1, keepdims=True))
            p = jnp.exp(s - m_new[:, :1])
            alpha = jnp.exp(m_prev - m_new)
            l_ref[mp] = alpha * l_ref[mp] + jnp.sum(p, axis=-1, keepdims=True)
            acc_ref[mp] = alpha * acc_ref[mp] + _dot(p.astype(BF16), v)
            m_ref[mp] = m_new

    in_band = jnp.abs(ki - qi) <= 1

    @pl.when(in_band)
    def _():
        tile = band_ref[jnp.clip(ki - qi + 1, 0, 2)]
        step(lambda s: s + tile)

    @pl.when(jnp.logical_not(in_band))
    def _():
        c = jnp.where(ki < qi, far_ref[h, 0], far_ref[h, 1])
        step(lambda s: s + c)

    @pl.when(ki == nk - 1)
    def _():
        lam = lam_ref[...]
        lam_full = (jnp.exp(jnp.sum(lam[0:1] * lam[1:2], keepdims=True))
                    - jnp.exp(jnp.sum(lam[2:3] * lam[3:4], keepdims=True)) + lam_init)
        o = acc_ref[0] / l_ref[0] - lam_full * (acc_ref[1] / l_ref[1])
        o_ref[...] = (_rms(o, nw_ref[...]) * (1.0 - lam_init)).astype(o_ref.dtype)


def _attn(qb, kb, vb, band, far, lam, nw, row_off, batch, seqlen, tq, lam_init, prev):
    t = qb.shape[0]
    nq = seqlen // tq
    off = row_off // tq
    qmap = lambda b, h, qi, ki: (off + b * nq + qi, h)
    kmap = lambda b, h, qi, ki: (off + b * nq + ki, h)
    in_specs = [
        pl.BlockSpec(memory_space=pltpu.SMEM),
        pl.BlockSpec((tq, DIFF_DV), qmap),
        pl.BlockSpec((tq, DIFF_DV), kmap),
        pl.BlockSpec((tq, DIFF_DV), kmap),
        pl.BlockSpec((None, 3, tq, tq), lambda b, h, qi, ki: (h, 0, 0, 0)),
        pl.BlockSpec(lam.shape, lambda b, h, qi, ki: (0, 0)),
        pl.BlockSpec((1, DIFF_DV), lambda b, h, qi, ki: (0, 0)),
    ]
    args = [far, qb, kb, vb, band, lam, nw]
    aliases = {}
    if prev is not None:
        aliases = {len(args): 0}
        in_specs.append(pl.BlockSpec(memory_space=pl.ANY))
        args.append(prev)
    return pl.pallas_call(
        functools.partial(_attn_body, lam_init=lam_init),
        grid=(batch, DIFF_HEADS, nq, nq),
        in_specs=in_specs,
        input_output_aliases=aliases,
        out_specs=pl.BlockSpec((tq, DIFF_DV), qmap),
        out_shape=jax.ShapeDtypeStruct((t, DIFF_VW), F32),
        scratch_shapes=[pltpu.VMEM((2, tq, DIFF_DV), BF16),
                        pltpu.VMEM((2, tq, V7X_LANES), F32),
                        pltpu.VMEM((2, tq, V7X_LANES), F32),
                        pltpu.VMEM((2, tq, DIFF_DV), F32)],
        compiler_params=_cparams(("parallel", "parallel", "parallel", "arbitrary")),
        name="diffattn",
    )(*args)


def _memkv_body(m_ref, nw_ref, w_ref, o_ref):
    h = _rms(m_ref[...], nw_ref[...]).astype(BF16)
    o_ref[...] = _dot(h, w_ref[...]).astype(BF16)


def _memkv(mem, nw, wkv):
    nb, nm, d = mem.shape
    return pl.pallas_call(
        _memkv_body,
        grid=(nb,),
        in_specs=[pl.BlockSpec((None, nm, d), lambda b: (b, 0, 0)), _resident((1, d)),
                  _resident(wkv.shape)],
        out_specs=pl.BlockSpec((None, nm, wkv.shape[1]), lambda b: (b, 0, 0)),
        out_shape=jax.ShapeDtypeStruct((nb, nm, wkv.shape[1]), BF16),
        compiler_params=_cparams(("parallel",)),
        name="memkv",
    )(mem, nw, wkv)


def _mergex_body(x_ref, of_ref, ob_ref, z_ref, oattn_ref, sg_ref, kv_ref, gn_ref, wua_ref,
                 wub_ref, wout_ref, xn_ref, wq_ref, wo_ref, o_ref):
    o = of_ref[...] + ob_ref[...]
    z = z_ref[...]
    gn = gn_ref[...]
    heads = []
    for h in range(GDN_HEADS):
        sl = slice(h * GDN_DV, (h + 1) * GDN_DV)
        zh = z[:, sl]
        heads.append(_rms(o[:, sl], gn) * (zh * jax.nn.sigmoid(zh)))
    oa = jnp.concatenate(heads, axis=1).astype(BF16)
    ya = _dot(oa, wua_ref[...])
    yb = _dot(oattn_ref[...].astype(BF16), wub_ref[...])
    sg = sg_ref[...]
    merged = sg[:, :D_MODEL] * ya + sg[:, D_MODEL:] * yb
    x = x_ref[...] + _dot(merged.astype(BF16), wout_ref[...])
    hq = _rms(x, xn_ref[...]).astype(BF16)
    q = _dot(hq, wq_ref[...]) * (X_DH ** -0.5)
    kv = kv_ref[...]
    outs = []
    for h in range(X_HEADS):
        sl = slice(h * X_DH, (h + 1) * X_DH)
        kh = kv[:, sl]
        vh = kv[:, X_HEADS * X_DH + h * X_DH:X_HEADS * X_DH + (h + 1) * X_DH]
        s = _dot_nt(q[:, sl].astype(BF16), kh)
        s = s - jnp.max(s, axis=-1, keepdims=True)
        p = jnp.exp(s)
        p = p / jnp.sum(p, axis=-1, keepdims=True)
        outs.append(_dot(p.astype(BF16), vh))
    ox = jnp.concatenate(outs, axis=1).astype(BF16)
    o_ref[...] = x + _dot(ox, wo_ref[...])


def _mergex(x, o_f, o_b, z, oattn, sg, kv, gn, wua, wub, wout, xn, wq, wo, tm, tile_batch):
    t, d = x.shape
    rows = lambda width: pl.BlockSpec((tm, width), lambda i: (i, 0))
    return pl.pallas_call(
        _mergex_body,
        grid=(t // tm,),
        in_specs=[rows(d), rows(GDN_V), rows(GDN_V), rows(GDN_V), rows(DIFF_VW), rows(2 * d),
                  pl.BlockSpec((None,) + kv.shape[1:], lambda i: (tile_batch(i), 0, 0)),
                  _resident(gn.shape), _resident(wua.shape), _resident(wub.shape),
                  _resident(wout.shape), _resident(xn.shape), _resident(wq.shape),
                  _resident(wo.shape)],
        out_specs=rows(d),
        out_shape=jax.ShapeDtypeStruct((t, d), F32),
        compiler_params=_cparams(("parallel",)),
        name="mergex",
    )(x, o_f, o_b, z, oattn, sg, kv, gn, wua, wub, wout, xn, wq, wo)


def _rel_bucket(rel):
    nb = N_BUCKETS // 2
    ret = jnp.where(rel > 0, nb, 0)
    n = jnp.abs(rel)
    max_exact = nb // 2
    nf = jnp.maximum(n, 1).astype(F32)
    large = max_exact + (jnp.log(nf / max_exact) / math.log(MAX_DISTANCE / max_exact)
                         * (nb - max_exact)).astype(jnp.int32)
    large = jnp.minimum(large, nb - 1)
    return ret + jnp.where(n < max_exact, n, large)


def _bias_tables(rel_bias, tq):
    assert tq >= MAX_DISTANCE
    rel = jnp.arange(-(2 * tq - 1), 2 * tq, dtype=jnp.int32)
    by_rel = rel_bias[_rel_bucket(rel)].astype(F32)
    i = jnp.arange(tq, dtype=jnp.int32)[:, None]
    j = jnp.arange(tq, dtype=jnp.int32)[None, :]
    idx = jnp.stack([(dd - 1) * tq + j - i + (2 * tq - 1) for dd in range(3)])
    band = jnp.moveaxis(by_rel[idx], -1, 0)
    far = jnp.stack([by_rel[0], by_rel[-1]], axis=1)
    return band, far


def _pick(limit, n):
    tile = limit
    while n % tile:
        tile //= 2
    return tile


def _encode(xs, mems, p, tile_limits=None):
    lim = dict(ffn=512, proj=256, prep=256, gdn=256, attn=512, mergex=256)
    if tile_limits:
        lim.update(tile_limits)
    depth = p["w_in"].shape[0]
    d = D_MODEL
    groups = []
    row = 0
    bat = 0
    for x in xs:
        b, l, _ = x.shape
        groups.append(dict(b=b, l=l, row=row, bat=bat))
        row += b * l
        bat += b
    t_all = row
    seqlens = [g["l"] for g in groups]
    common = functools.reduce(math.gcd, seqlens)
    tiles = {k: _pick(v, common) for k, v in lim.items()}

    x = jnp.concatenate([xx.reshape(-1, d) for xx in xs], axis=0)
    mem = jnp.concatenate(mems, axis=0)

    def tile_batch(i):
        r0 = i * tiles["mergex"]
        bidx = 0
        for g in groups:
            bidx = jnp.where(r0 >= g["row"], g["bat"] + (r0 - g["row"]) // g["l"], bidx)
        return bidx

    seq_tiles = tuple((g["row"] // tiles["prep"], g["l"] // tiles["prep"]) for g in groups)
    band, far = _bias_tables(p["rel_bias"], tiles["attn"])
    nf = D_FF // FFN_CHUNK

    def ffn_weights(prefix, i):
        wg = p[prefix + "_w_gate"][i].reshape(d, nf, FFN_CHUNK).transpose(1, 0, 2).astype(BF16)
        wu = p[prefix + "_w_up"][i].reshape(d, nf, FFN_CHUNK).transpose(1, 0, 2).astype(BF16)
        wd = p[prefix + "_w_down"][i].reshape(nf, FFN_CHUNK, d).astype(BF16)
        return p[prefix + "_norm"][i].reshape(1, d), wg, wu, wd

    for i in range(depth):
        x = _ffn(x, *ffn_weights("ffn1", i), None, tiles["ffn"])

        w_in = p["w_in"][i]
        o_beta = 3 * GDN_QK
        o_z = o_beta + 2 * N_GATES
        o_qb = o_z + GDN_V
        gate_w = jnp.pad(w_in[:, o_beta:o_z], ((0, 0), (0, V7X_LANES - 2 * N_GATES)))
        w_perm = jnp.concatenate(
            [w_in[:, :o_beta], gate_w, w_in[:, o_z:o_qb],
             w_in[:, o_qb:o_qb + DIFF_QK] * (DIFF_DH ** -0.5), w_in[:, o_qb + DIFF_QK:]],
            axis=1).astype(BF16)
        assert w_perm.shape[1] == _P_COLS
        qkva, gates, z, qb, kb, vb, sg = _proj(x, p["mix_norm"][i].reshape(1, d), w_perm,
                                               tiles["proj"])

        conv_w8 = jnp.pad(p["conv_w"][i], ((0, HALO - GDN_CONV), (0, 0)))
        lane_pad = (N_GATES, V7X_LANES - 2 * N_GATES)
        alog_pad = jnp.pad(p["gdn_a_log"][i].reshape(-1), lane_pad).reshape(1, V7X_LANES)
        dtb_pad = jnp.pad(p["gdn_dt_bias"][i].reshape(-1), lane_pad).reshape(1, V7X_LANES)
        qkvn, bg = _gdnprep(qkva, conv_w8, gates, alog_pad, dtb_pad, tiles["prep"], seq_tiles)
        bg_rows = bg[:, :2 * N_GATES].T

        lam_init = 0.8 - 0.6 * math.exp(-0.3 * i)
        o_fb = o_attn = None
        for g in groups:
            o_fb = _gdn(qkvn, bg, bg_rows, g["row"], g["b"], g["l"], tiles["gdn"], o_fb)
            o_attn = _attn(qb, kb, vb, band, far, p["diff_lambda"][i],
                           p["diff_norm"][i].reshape(1, DIFF_DV), g["row"], g["b"], g["l"],
                           tiles["attn"], lam_init, o_attn)
        o_f, o_b = o_fb

        kv = _memkv(mem, p["mem_norm"][i].reshape(1, d), p["xattn_wkv"][i].astype(BF16))
        x = _mergex(x, o_f, o_b, z, o_attn, sg, kv, p["gdn_norm"][i].reshape(1, GDN_DV),
                    p["w_up_a"][i].astype(BF16), p["w_up_b"][i].astype(BF16),
                    p["w_out"][i].astype(BF16), p["xattn_norm"][i].reshape(1, d),
                    p["xattn_wq"][i].astype(BF16), p["xattn_wo"][i].astype(BF16),
                    tiles["mergex"], tile_batch)

        final_w = p["final_norm"].reshape(1, d) if i == depth - 1 else None
        x = _ffn(x, *ffn_weights("ffn2", i), final_w, tiles["ffn"])

    return tuple(x[g["row"]:g["row"] + g["b"] * g["l"]].reshape(g["b"], g["l"], d)
                 for g in groups)


def kernel(x_prompt, x_sample, mem_prompt, mem_sample, ffn1_norm, ffn1_w_gate, ffn1_w_up, ffn1_w_down, mix_norm, w_in, conv_w, gdn_a_log, gdn_dt_bias, gdn_norm, w_up_a, diff_lambda, diff_norm, w_up_b, w_out, rel_bias, xattn_norm, mem_norm, xattn_wq, xattn_wkv, xattn_wo, ffn2_norm, ffn2_w_gate, ffn2_w_up, ffn2_w_down, final_norm):
    params = dict(
        ffn1_norm=ffn1_norm, ffn1_w_gate=ffn1_w_gate, ffn1_w_up=ffn1_w_up,
        ffn1_w_down=ffn1_w_down, mix_norm=mix_norm, w_in=w_in, conv_w=conv_w,
        gdn_a_log=gdn_a_log, gdn_dt_bias=gdn_dt_bias, gdn_norm=gdn_norm, w_up_a=w_up_a,
        diff_lambda=diff_lambda, diff_norm=diff_norm, w_up_b=w_up_b, w_out=w_out,
        rel_bias=rel_bias, xattn_norm=xattn_norm, mem_norm=mem_norm, xattn_wq=xattn_wq,
        xattn_wkv=xattn_wkv, xattn_wo=xattn_wo, ffn2_norm=ffn2_norm,
        ffn2_w_gate=ffn2_w_gate, ffn2_w_up=ffn2_w_up, ffn2_w_down=ffn2_w_down,
        final_norm=final_norm)
    y_prompt, y_sample = _encode((x_prompt, x_sample), (mem_prompt, mem_sample), params)
    return (y_prompt, y_sample)
```

```python
import functools
import math

import jax
import jax.numpy as jnp
from jax import lax
from jax.experimental import pallas as pl
from jax.experimental.pallas import tpu as pltpu

F32 = jnp.float32
BF16 = jnp.bfloat16
HIGHEST = lax.Precision.HIGHEST

EPS = 1e-6
LOG2E = math.log2(math.e)
D_MODEL = 1024
N_MEM = 256
GDN_HEADS = 4
GDN_DK = 128
GDN_DV = 128
GDN_CONV = 5
GDN_CHUNK = 64
DIFF_HEADS = 8
DIFF_DH = 64
DIFF_DV = 2 * DIFF_DH
N_BUCKETS = 32
MAX_DISTANCE = 128
X_HEADS = 4
X_DH = 128
D_FF = 2816
GDN_QK = GDN_HEADS * GDN_DK
GDN_V = GDN_HEADS * GDN_DV
DIFF_QK = DIFF_HEADS * 2 * DIFF_DH
DIFF_VW = DIFF_HEADS * DIFF_DV
N_GATES = 2 * GDN_HEADS

V7X_LANES = 128
V7X_SUBLANES = 8
V7X_VMEM_LIMIT = 56 * 1024 * 1024

FFN_CHUNK = 256
HALO = V7X_SUBLANES


def _cparams(sem):
    return pltpu.CompilerParams(dimension_semantics=sem, vmem_limit_bytes=V7X_VMEM_LIMIT)


def _dot(a, b, precision=None):
    return jnp.dot(a, b, preferred_element_type=F32, precision=precision)


def _dot_nt(a, b, precision=None):
    return lax.dot_general(a, b, (((1,), (1,)), ((), ())),
                           preferred_element_type=F32, precision=precision)


def _dot_tn(a, b, precision=None):
    return lax.dot_general(a, b, (((0,), (0,)), ((), ())),
                           preferred_element_type=F32, precision=precision)


def _rms(x, w):
    return x * lax.rsqrt(jnp.mean(x * x, axis=-1, keepdims=True) + EPS) * w


def _resident(shape):
    nd = len(shape)
    return pl.BlockSpec(shape, lambda *_: (0,) * nd, pipeline_mode=pl.Buffered(1))


def _ffn_body(x_ref, nw_ref, wg_ref, wu_ref, wd_ref, *rest, n_chunks, final):
    if final:
        fn_ref, o_ref, h_ref, acc_ref = rest
    else:
        o_ref, h_ref, acc_ref = rest
    x = x_ref[...]
    h_ref[...] = _rms(x, nw_ref[...]).astype(BF16)
    acc_ref[...] = jnp.zeros_like(acc_ref)

    def chunk(c, carry):
        h = h_ref[...]
        g = _dot(h, wg_ref[c])
        u = _dot(h, wu_ref[c])
        a = (g * jax.nn.sigmoid(g) * u).astype(BF16)
        acc_ref[...] += _dot(a, wd_ref[c])
        return carry

    lax.fori_loop(0, n_chunks, chunk, 0)
    y = x + 0.5 * acc_ref[...]
    if final:
        y = _rms(y, fn_ref[...])
    o_ref[...] = y


def _ffn(x, nw, wg, wu, wd, final_w, tm):
    t, d = x.shape
    nf = wg.shape[0]
    row = pl.BlockSpec((tm, d), lambda i: (i, 0))
    in_specs = [row, _resident((1, d)), _resident(wg.shape), _resident(wu.shape),
                _resident(wd.shape)]
    args = [x, nw, wg, wu, wd]
    if final_w is not None:
        in_specs.append(_resident((1, d)))
        args.append(final_w)
    return pl.pallas_call(
        functools.partial(_ffn_body, n_chunks=nf, final=final_w is not None),
        grid=(t // tm,),
        in_specs=in_specs,
        out_specs=row,
        out_shape=jax.ShapeDtypeStruct((t, d), F32),
        scratch_shapes=[pltpu.VMEM((tm, d), BF16), pltpu.VMEM((tm, d), F32)],
        compiler_params=_cparams(("parallel",)),
        name="ffn",
    )(*args)


_P_QKVA = (0, 3 * GDN_QK)
_P_GATE = (_P_QKVA[1], _P_QKVA[1] + V7X_LANES)
_P_Z = (_P_GATE[1], _P_GATE[1] + GDN_V)
_P_QB = (_P_Z[1], _P_Z[1] + DIFF_QK)
_P_KB = (_P_QB[1], _P_QB[1] + DIFF_QK)
_P_VB = (_P_KB[1], _P_KB[1] + DIFF_VW)
_P_SG = (_P_VB[1], _P_VB[1] + 2 * D_MODEL)
_P_COLS = _P_SG[1]


def _proj_body(x_ref, nw_ref, w_ref, qkva_ref, gate_ref, z_ref, qb_ref, kb_ref, vb_ref, sg_ref):
    u = _rms(x_ref[...], nw_ref[...]).astype(BF16)

    def seg(span):
        return _dot(u, w_ref[:, span[0]:span[1]])

    qkva_ref[...] = seg(_P_QKVA)
    gate_ref[...] = seg(_P_GATE)
    z_ref[...] = seg(_P_Z)
    qb_ref[...] = seg(_P_QB).astype(BF16)
    kb_ref[...] = seg(_P_KB).astype(BF16)
    vb_ref[...] = seg(_P_VB).astype(BF16)
    sg_ref[...] = jax.nn.sigmoid(seg(_P_SG))


def _proj(x, nw, w, tm):
    t, d = x.shape
    widths = [(_P_QKVA, F32), (_P_GATE, F32), (_P_Z, F32), (_P_QB, BF16), (_P_KB, BF16),
              (_P_VB, BF16), (_P_SG, F32)]
    out_shape = [jax.ShapeDtypeStruct((t, s[1] - s[0]), dt) for s, dt in widths]
    out_specs = [pl.BlockSpec((tm, s[1] - s[0]), lambda i: (i, 0)) for s, _ in widths]
    return pl.pallas_call(
        _proj_body,
        grid=(t // tm,),
        in_specs=[pl.BlockSpec((tm, d), lambda i: (i, 0)), _resident((1, d)), _resident(w.shape)],
        out_specs=out_specs,
        out_shape=out_shape,
        compiler_params=_cparams(("parallel",)),
        name="proj",
    )(x, nw, w)


def _gdnprep_body(cur_ref, prev_ref, next_ref, cw_ref, gate_ref, alog_ref, dtb_ref,
                  qkv_ref, bg_ref, *, tm, seq_tiles):
    i = pl.program_id(0)
    first = jnp.bool_(False)
    last = jnp.bool_(False)
    for start, per_seq in seq_tiles:
        rel = i - start
        first = first | ((rel >= 0) & (rel % per_seq == 0))
        last = last | ((rel >= 0) & (rel % per_seq == per_seq - 1))
    prev = jnp.where(first, 0.0, prev_ref[...])
    nxt = jnp.where(last, 0.0, next_ref[...])
    ext = jnp.concatenate([prev, cur_ref[...], nxt], axis=0)
    pad = (GDN_CONV - 1) // 2
    acc = None
    for k in range(GDN_CONV):
        lo = HALO - pad + k
        term = ext[lo:lo + tm, :] * cw_ref[k:k + 1, :]
        acc = term if acc is None else acc + term
    y = acc * jax.nn.sigmoid(acc)
    for h in range(2 * GDN_HEADS):
        lo = h * GDN_DK
        yh = y[:, lo:lo + GDN_DK]
        nrm = yh * lax.rsqrt(jnp.sum(yh * yh, axis=-1, keepdims=True) + EPS)
        if h < GDN_HEADS:
            nrm = nrm * (GDN_DK ** -0.5)
        qkv_ref[:, lo:lo + GDN_DK] = nrm
    qkv_ref[:, 2 * GDN_QK:] = y[:, 2 * GDN_QK:]
    gl = gate_ref[...]
    beta = jax.nn.sigmoid(gl)
    xa = gl + dtb_ref[...]
    softplus = jnp.maximum(xa, 0.0) + jnp.log1p(jnp.exp(-jnp.abs(xa)))
    g = -jnp.exp(alog_ref[...]) * softplus
    lane = lax.broadcasted_iota(jnp.int32, gl.shape, 1)
    bg_ref[...] = jnp.where(lane < N_GATES, beta, g)


def _gdnprep(qkva, conv_w8, gates, alog_pad, dtb_pad, tm, seq_tiles):
    t, c = qkva.shape
    hb = tm // HALO
    nblk = t // HALO
    return pl.pallas_call(
        functools.partial(_gdnprep_body, tm=tm, seq_tiles=seq_tiles),
        grid=(t // tm,),
        in_specs=[
            pl.BlockSpec((tm, c), lambda i: (i, 0)),
            pl.BlockSpec((HALO, c), lambda i: (jnp.maximum(i * hb - 1, 0), 0)),
            pl.BlockSpec((HALO, c), lambda i: (jnp.minimum((i + 1) * hb, nblk - 1), 0)),
            _resident(conv_w8.shape),
            pl.BlockSpec((tm, V7X_LANES), lambda i: (i, 0)),
            _resident((1, V7X_LANES)),
            _resident((1, V7X_LANES)),
        ],
        out_specs=[pl.BlockSpec((tm, c), lambda i: (i, 0)),
                   pl.BlockSpec((tm, V7X_LANES), lambda i: (i, 0))],
        out_shape=[jax.ShapeDtypeStruct((t, c), F32),
                   jax.ShapeDtypeStruct((t, V7X_LANES), F32)],
        compiler_params=_cparams(("parallel",)),
        name="gdnprep",
    )(qkva, qkva, qkva, conv_w8, gates, alog_pad, dtb_pad)


def _gdn_body(qkv_f, bgc_f, bgr_f, qkv_b, bgc_b, bgr_b, *rest, n_chunks):
    of_ref, ob_ref, s_ref = rest[-3:]
    c64 = GDN_CHUNK

    @pl.when(pl.program_id(1) == 0)
    def _():
        s_ref[...] = jnp.zeros_like(s_ref)

    row = lax.broadcasted_iota(jnp.int32, (c64, c64), 0)
    col = lax.broadcasted_iota(jnp.int32, (c64, c64), 1)
    eye = (row == col).astype(F32)
    incl = (row >= col, row <= col)
    strict = (row > col, row < col)

    for c in range(n_chunks):
        for d in range(2):
            qkv_ref, bgc_ref, bgr_ref, o_ref = (
                (qkv_f, bgc_f, bgr_f, of_ref), (qkv_b, bgc_b, bgr_b, ob_ref))[d]
            r0 = (c if d == 0 else n_chunks - 1 - c) * c64
            tri_c = incl[d].astype(F32)
            tri_r = incl[1 - d].astype(F32)
            gt = bgc_ref[r0:r0 + c64, :]
            gcs = _dot(tri_c, gt, HIGHEST)
            grs = _dot(bgr_ref[:, r0:r0 + c64], tri_r, HIGHEST)
            g_last = gcs[c64 - 1:c64, :] if d == 0 else gcs[0:1, :]
            egc = jnp.exp(gcs)
            ekd = jnp.exp(g_last - gcs)
            egl = jnp.exp(g_last)
            for h in range(GDN_HEADS):
                idx = d * GDN_HEADS + h
                gi = N_GATES + idx
                q = qkv_ref[r0:r0 + c64, h * GDN_DK:(h + 1) * GDN_DK]
                k = qkv_ref[r0:r0 + c64, GDN_QK + h * GDN_DK:GDN_QK + (h + 1) * GDN_DK]
                v = qkv_ref[r0:r0 + c64, 2 * GDN_QK + h * GDN_DV:2 * GDN_QK + (h + 1) * GDN_DV]
                beta = gt[:, idx:idx + 1]
                diff = gcs[:, gi:gi + 1] - grs[gi:gi + 1, :]
                gam = jnp.where(incl[d], jnp.exp(jnp.where(incl[d], diff, 0.0)), 0.0)
                kb = k * beta
                vb = v * beta
                k16 = k.astype(BF16)
                kk = _dot_nt(kb.astype(BF16), k16)
                m = -jnp.where(strict[d], kk * gam, 0.0)
                x = eye + m
                mp = m
                for _ in range(5):
                    mp = _dot(mp, mp, HIGHEST)
                    x = x + _dot(x, mp, HIGHEST)
                rhs = jnp.concatenate([vb, kb * egc[:, gi:gi + 1]], axis=1)
                sol = _dot(x, rhs, HIGHEST)
                u = sol[:, :GDN_DV]
                w = sol[:, GDN_DV:]
                aqk = _dot_nt(q.astype(BF16), k16) * gam
                qd = q * egc[:, gi:gi + 1]
                kd = k * ekd[:, gi:gi + 1]
                s = s_ref[idx]
                s16 = s.astype(BF16)
                v_new = u - _dot(w.astype(BF16), s16)
                vn16 = v_new.astype(BF16)
                o = _dot(qd.astype(BF16), s16) + _dot(aqk.astype(BF16), vn16)
                s_ref[idx] = s * egl[:, gi:gi + 1] + _dot_tn(kd.astype(BF16), vn16)
                o_ref[r0:r0 + c64, h * GDN_DV:(h + 1) * GDN_DV] = o


def _gdn(qkvn, bg, bg_rows, row_off, batch, seqlen, ts, prev):
    t, c = qkvn.shape
    ns = seqlen // ts
    off = row_off // ts
    fwd = lambda b, s: (off + b * ns + s, 0)
    bwd = lambda b, s: (off + b * ns + ns - 1 - s, 0)
    fwd_r = lambda b, s: (0, off + b * ns + s)
    bwd_r = lambda b, s: (0, off + b * ns + ns - 1 - s)
    nr = bg_rows.shape[0]
    out = jax.ShapeDtypeStruct((t, GDN_V), F32)
    in_specs = [
        pl.BlockSpec((ts, c), fwd), pl.BlockSpec((ts, V7X_LANES), fwd),
        pl.BlockSpec((nr, ts), fwd_r),
        pl.BlockSpec((ts, c), bwd), pl.BlockSpec((ts, V7X_LANES), bwd),
        pl.BlockSpec((nr, ts), bwd_r),
    ]
    args = [qkvn, bg, bg_rows, qkvn, bg, bg_rows]
    aliases = {}
    if prev is not None:
        aliases = {len(args): 0, len(args) + 1: 1}
        in_specs += [pl.BlockSpec(memory_space=pl.ANY)] * 2
        args += list(prev)
    return pl.pallas_call(
        functools.partial(_gdn_body, n_chunks=ts // GDN_CHUNK),
        grid=(batch, ns),
        in_specs=in_specs,
        out_specs=[pl.BlockSpec((ts, GDN_V), fwd), pl.BlockSpec((ts, GDN_V), bwd)],
        out_shape=[out, out],
        scratch_shapes=[pltpu.VMEM((2 * GDN_HEADS, GDN_DK, GDN_DV), F32)],
        input_output_aliases=aliases,
        compiler_params=_cparams(("parallel", "arbitrary")),
        name="gdn",
    )(*args)


def _attn_body(far_ref, q_ref, k_ref, vt_ref, band_ref, lam_ref, nw_ref, *rest,
               lam_init, tk, nk):
    o_ref, qz_ref, s_ref, mx_ref, m_ref, l_ref, acc_ref = rest[-7:]
    h = pl.program_id(1)
    qi = pl.program_id(2)
    tq = q_ref.shape[0]
    sub = V7X_SUBLANES

    m_ref[...] = jnp.full_like(m_ref, -1e30)
    l_ref[...] = jnp.zeros_like(l_ref)
    acc_ref[...] = jnp.zeros_like(acc_ref)
    q = q_ref[...]
    lane = lax.broadcasted_iota(jnp.int32, q.shape, 1)
    zero = jnp.zeros_like(q)
    qz_ref[0] = jnp.where(lane < DIFF_DH, q, zero)
    qz_ref[1] = jnp.where(lane >= DIFF_DH, q, zero)

    def scores(kt, slot, bias_tile):
        k = k_ref[pl.ds(pl.multiple_of(kt * tk, tk), tk), :]
        for mp in range(2):
            s = _dot_nt(k, qz_ref[mp])
            if bias_tile is not None:
                s = s + bias_tile
            s_ref[slot, mp] = s
            mx_ref[slot, mp] = jnp.max(s.reshape(tk // sub, sub, tq), axis=0)

    def accumulate(kt, slot, c):
        vt = vt_ref[:, pl.ds(pl.multiple_of(kt * tk, tk), tk)]
        for mp in range(2):
            m_cur = jnp.max(mx_ref[slot, mp], axis=0, keepdims=True) + c
            m_prev = m_ref[mp]
            m_new = jnp.maximum(m_prev, m_cur)
            alpha = jnp.exp2(m_prev - m_new)
            m_ref[mp] = m_new
            p = jnp.exp2(s_ref[slot, mp] - (m_new - c))
            l_ref[mp] = alpha * l_ref[mp] + jnp.sum(p.reshape(tk // sub, sub, tq), axis=0)
            acc_ref[mp] = alpha * acc_ref[mp] + _dot(vt, p.astype(BF16))

    n_left = jnp.maximum(qi - 1, 0)
    right0 = jnp.minimum(qi + 2, nk)
    n_far = n_left + (nk - right0)

    def far_tile(f):
        return jnp.where(f < n_left, f, right0 + (f - n_left))

    @pl.when(n_far > 0)
    def _():
        scores(far_tile(0), 0, None)

    def far_step(f, carry):
        nxt = jnp.minimum(f + 1, n_far - 1)
        scores(far_tile(nxt), (f + 1) % 2, None)
        accumulate(far_tile(f), f % 2, jnp.where(f < n_left, far_ref[h, 0], far_ref[h, 1]))
        return carry

    lax.fori_loop(0, n_far, far_step, 0)

    for dd in range(3):
        kt = qi - 1 + dd

        @pl.when((kt >= 0) & (kt < nk))
        def _():
            scores(kt, 0, band_ref[dd])
            accumulate(kt, 0, 0.0)

    lam = lam_ref[...]
    lam_full = (jnp.exp(jnp.sum(lam[0:1] * lam[1:2], keepdims=True))
                - jnp.exp(jnp.sum(lam[2:3] * lam[3:4], keepdims=True)) + lam_init)
    l0 = jnp.sum(l_ref[0], axis=0, keepdims=True)
    l1 = jnp.sum(l_ref[1], axis=0, keepdims=True)
    o = acc_ref[0] / l0 - lam_full * (acc_ref[1] / l1)
    o = o * lax.rsqrt(jnp.mean(o * o, axis=0, keepdims=True) + EPS)
    o = o * (nw_ref[...] * (1.0 - lam_init))
    o_ref[...] = o.T


def _attn(qb, kb, vbt, band, far, lam, nw_col, row_off, batch, seqlen, tq, lam_init, prev):
    t = qb.shape[0]
    nq = seqlen // tq
    off = row_off // tq
    off_seq = row_off // seqlen
    assert row_off % seqlen == 0
    qmap = lambda b, h, qi: (off + b * nq + qi, h)
    in_specs = [
        pl.BlockSpec(memory_space=pltpu.SMEM),
        pl.BlockSpec((tq, DIFF_DV), qmap),
        pl.BlockSpec((seqlen, DIFF_DV), lambda b, h, qi: (off_seq + b, h)),
        pl.BlockSpec((DIFF_DV, seqlen), lambda b, h, qi: (h, off_seq + b)),
        pl.BlockSpec((None, 3, tq, tq), lambda b, h, qi: (h, 0, 0, 0)),
        pl.BlockSpec(lam.shape, lambda b, h, qi: (0, 0)),
        pl.BlockSpec((DIFF_DV, 1), lambda b, h, qi: (0, 0)),
    ]
    args = [far, qb, kb, vbt, band, lam, nw_col]
    aliases = {}
    if prev is not None:
        aliases = {len(args): 0}
        in_specs.append(pl.BlockSpec(memory_space=pl.ANY))
        args.append(prev)
    return pl.pallas_call(
        functools.partial(_attn_body, lam_init=lam_init, tk=tq, nk=nq),
        grid=(batch, DIFF_HEADS, nq),
        in_specs=in_specs,
        input_output_aliases=aliases,
        out_specs=pl.BlockSpec((tq, DIFF_DV), qmap),
        out_shape=jax.ShapeDtypeStruct((t, DIFF_VW), F32),
        scratch_shapes=[pltpu.VMEM((2, tq, DIFF_DV), BF16),
                        pltpu.VMEM((2, 2, tq, tq), F32),
                        pltpu.VMEM((2, 2, V7X_SUBLANES, tq), F32),
                        pltpu.VMEM((2, 1, tq), F32),
                        pltpu.VMEM((2, V7X_SUBLANES, tq), F32),
                        pltpu.VMEM((2, DIFF_DV, tq), F32)],
        compiler_params=_cparams(("parallel", "parallel", "parallel")),
        name="diffattn",
    )(*args)


def _memkv_body(m_ref, nw_ref, w_ref, o_ref):
    h = _rms(m_ref[...], nw_ref[...]).astype(BF16)
    o_ref[...] = _dot(h, w_ref[...]).astype(BF16)


def _memkv(mem, nw, wkv):
    nb, nm, d = mem.shape
    return pl.pallas_call(
        _memkv_body,
        grid=(nb,),
        in_specs=[pl.BlockSpec((None, nm, d), lambda b: (b, 0, 0)), _resident((1, d)),
                  _resident(wkv.shape)],
        out_specs=pl.BlockSpec((None, nm, wkv.shape[1]), lambda b: (b, 0, 0)),
        out_shape=jax.ShapeDtypeStruct((nb, nm, wkv.shape[1]), BF16),
        compiler_params=_cparams(("parallel",)),
        name="memkv",
    )(mem, nw, wkv)


def _mergex_body(x_ref, of_ref, ob_ref, z_ref, oattn_ref, sg_ref, kv_ref, gn_ref, wua_ref,
                 wub_ref, wout_ref, xn_ref, wq_ref, wo_ref, o_ref):
    o = of_ref[...] + ob_ref[...]
    z = z_ref[...]
    gn = gn_ref[...]
    heads = []
    for h in range(GDN_HEADS):
        sl = slice(h * GDN_DV, (h + 1) * GDN_DV)
        zh = z[:, sl]
        heads.append(_rms(o[:, sl], gn) * (zh * jax.nn.sigmoid(zh)))
    oa = jnp.concatenate(heads, axis=1).astype(BF16)
    ya = _dot(oa, wua_ref[...])
    yb = _dot(oattn_ref[...].astype(BF16), wub_ref[...])
    sg = sg_ref[...]
    merged = sg[:, :D_MODEL] * ya + sg[:, D_MODEL:] * yb
    x = x_ref[...] + _dot(merged.astype(BF16), wout_ref[...])
    hq = _rms(x, xn_ref[...]).astype(BF16)
    q = _dot(hq, wq_ref[...]) * (X_DH ** -0.5)
    kv = kv_ref[...]
    outs = []
    for h in range(X_HEADS):
        sl = slice(h * X_DH, (h + 1) * X_DH)
        kh = kv[:, sl]
        vh = kv[:, X_HEADS * X_DH + h * X_DH:X_HEADS * X_DH + (h + 1) * X_DH]
        s = _dot_nt(q[:, sl].astype(BF16), kh)
        s = s - jnp.max(s, axis=-1, keepdims=True)
        p = jnp.exp(s)
        p = p / jnp.sum(p, axis=-1, keepdims=True)
        outs.append(_dot(p.astype(BF16), vh))
    ox = jnp.concatenate(outs, axis=1).astype(BF16)
    o_ref[...] = x + _dot(ox, wo_ref[...])


def _mergex(x, o_f, o_b, z, oattn, sg, kv, gn, wua, wub, wout, xn, wq, wo, tm, tile_batch):
    t, d = x.shape
    rows = lambda width: pl.BlockSpec((tm, width), lambda i: (i, 0))
    return pl.pallas_call(
        _mergex_body,
        grid=(t // tm,),
        in_specs=[rows(d), rows(GDN_V), rows(GDN_V), rows(GDN_V), rows(DIFF_VW), rows(2 * d),
                  pl.BlockSpec((None,) + kv.shape[1:], lambda i: (tile_batch(i), 0, 0)),
                  _resident(gn.shape), _resident(wua.shape), _resident(wub.shape),
                  _resident(wout.shape), _resident(xn.shape), _resident(wq.shape),
                  _resident(wo.shape)],
        out_specs=rows(d),
        out_shape=jax.ShapeDtypeStruct((t, d), F32),
        compiler_params=_cparams(("parallel",)),
        name="mergex",
    )(x, o_f, o_b, z, oattn, sg, kv, gn, wua, wub, wout, xn, wq, wo)


def _rel_bucket(rel):
    nb = N_BUCKETS // 2
    ret = jnp.where(rel > 0, nb, 0)
    n = jnp.abs(rel)
    max_exact = nb // 2
    nf = jnp.maximum(n, 1).astype(F32)
    large = max_exact + (jnp.log(nf / max_exact) / math.log(MAX_DISTANCE / max_exact)
                         * (nb - max_exact)).astype(jnp.int32)
    large = jnp.minimum(large, nb - 1)
    return ret + jnp.where(n < max_exact, n, large)


def _bias_tables(rel_bias, tq):
    assert tq >= MAX_DISTANCE
    rel = jnp.arange(-(2 * tq - 1), 2 * tq, dtype=jnp.int32)
    by_rel = rel_bias[_rel_bucket(rel)].astype(F32)
    i = jnp.arange(tq, dtype=jnp.int32)[:, None]
    j = jnp.arange(tq, dtype=jnp.int32)[None, :]
    idx = jnp.stack([(dd - 1) * tq + i - j + (2 * tq - 1) for dd in range(3)])
    by_rel = by_rel * LOG2E
    band = jnp.moveaxis(by_rel[idx], -1, 0)
    far = jnp.stack([by_rel[0], by_rel[-1]], axis=1)
    return band, far


def _pick(limit, n):
    tile = limit
    while n % tile:
        tile //= 2
    return tile


def _encode(xs, mems, p, tile_limits=None):
    lim = dict(ffn=512, proj=256, prep=256, gdn=256, attn=512, mergex=256)
    if tile_limits:
        lim.update(tile_limits)
    depth = p["w_in"].shape[0]
    d = D_MODEL
    groups = []
    row = 0
    bat = 0
    for x in xs:
        b, l, _ = x.shape
        groups.append(dict(b=b, l=l, row=row, bat=bat))
        row += b * l
        bat += b
    t_all = row
    seqlens = [g["l"] for g in groups]
    common = functools.reduce(math.gcd, seqlens)
    tiles = {k: _pick(v, common) for k, v in lim.items()}

    x = jnp.concatenate([xx.reshape(-1, d) for xx in xs], axis=0)
    mem = jnp.concatenate(mems, axis=0)

    def tile_batch(i):
        r0 = i * tiles["mergex"]
        bidx = 0
        for g in groups:
            bidx = jnp.where(r0 >= g["row"], g["bat"] + (r0 - g["row"]) // g["l"], bidx)
        return bidx

    seq_tiles = tuple((g["row"] // tiles["prep"], g["l"] // tiles["prep"]) for g in groups)
    band, far = _bias_tables(p["rel_bias"], tiles["attn"])
    nf = D_FF // FFN_CHUNK

    def ffn_weights(prefix, i):
        wg = p[prefix + "_w_gate"][i].reshape(d, nf, FFN_CHUNK).transpose(1, 0, 2).astype(BF16)
        wu = p[prefix + "_w_up"][i].reshape(d, nf, FFN_CHUNK).transpose(1, 0, 2).astype(BF16)
        wd = p[prefix + "_w_down"][i].reshape(nf, FFN_CHUNK, d).astype(BF16)
        return p[prefix + "_norm"][i].reshape(1, d), wg, wu, wd

    for i in range(depth):
        x = _ffn(x, *ffn_weights("ffn1", i), None, tiles["ffn"])

        w_in = p["w_in"][i]
        o_beta = 3 * GDN_QK
        o_z = o_beta + 2 * N_GATES
        o_qb = o_z + GDN_V
        gate_w = jnp.pad(w_in[:, o_beta:o_z], ((0, 0), (0, V7X_LANES - 2 * N_GATES)))
        w_perm = jnp.concatenate(
            [w_in[:, :o_beta], gate_w, w_in[:, o_z:o_qb],
             w_in[:, o_qb:o_qb + DIFF_QK] * (DIFF_DH ** -0.5 * LOG2E),
             w_in[:, o_qb + DIFF_QK:]],
            axis=1).astype(BF16)
        assert w_perm.shape[1] == _P_COLS
        qkva, gates, z, qb, kb, vb, sg = _proj(x, p["mix_norm"][i].reshape(1, d), w_perm,
                                               tiles["proj"])

        conv_w8 = jnp.pad(p["conv_w"][i], ((0, HALO - GDN_CONV), (0, 0)))
        lane_pad = (N_GATES, V7X_LANES - 2 * N_GATES)
        alog_pad = jnp.pad(p["gdn_a_log"][i].reshape(-1), lane_pad).reshape(1, V7X_LANES)
        dtb_pad = jnp.pad(p["gdn_dt_bias"][i].reshape(-1), lane_pad).reshape(1, V7X_LANES)
        qkvn, bg = _gdnprep(qkva, conv_w8, gates, alog_pad, dtb_pad, tiles["prep"], seq_tiles)
        bg_rows = bg[:, :2 * N_GATES].T

        lam_init = 0.8 - 0.6 * math.exp(-0.3 * i)
        o_fb = o_attn = None
        vbt = vb.T
        for g in groups:
            o_fb = _gdn(qkvn, bg, bg_rows, g["row"], g["b"], g["l"], tiles["gdn"], o_fb)
            o_attn = _attn(qb, kb, vbt, band, far, p["diff_lambda"][i],
                           p["diff_norm"][i].reshape(DIFF_DV, 1), g["row"], g["b"], g["l"],
                           tiles["attn"], lam_init, o_attn)
        o_f, o_b = o_fb

        kv = _memkv(mem, p["mem_norm"][i].reshape(1, d), p["xattn_wkv"][i].astype(BF16))
        x = _mergex(x, o_f, o_b, z, o_attn, sg, kv, p["gdn_norm"][i].reshape(1, GDN_DV),
                    p["w_up_a"][i].astype(BF16), p["w_up_b"][i].astype(BF16),
                    p["w_out"][i].astype(BF16), p["xattn_norm"][i].reshape(1, d),
                    p["xattn_wq"][i].astype(BF16), p["xattn_wo"][i].astype(BF16),
                    tiles["mergex"], tile_batch)

        final_w = p["final_norm"].reshape(1, d) if i == depth - 1 else None
        x = _ffn(x, *ffn_weights("ffn2", i), final_w, tiles["ffn"])

    return tuple(x[g["row"]:g["row"] + g["b"] * g["l"]].reshape(g["b"], g["l"], d)
                 for g in groups)


def kernel(x_prompt, x_sample, mem_prompt, mem_sample, ffn1_norm, ffn1_w_gate, ffn1_w_up, ffn1_w_down, mix_norm, w_in, conv_w, gdn_a_log, gdn_dt_bias, gdn_norm, w_up_a, diff_lambda, diff_norm, w_up_b, w_out, rel_bias, xattn_norm, mem_norm, xattn_wq, xattn_wkv, xattn_wo, ffn2_norm, ffn2_w_gate, ffn2_w_up, ffn2_w_down, final_norm):
    params = dict(
        ffn1_norm=ffn1_norm, ffn1_w_gate=ffn1_w_gate, ffn1_w_up=ffn1_w_up,
        ffn1_w_down=ffn1_w_down, mix_norm=mix_norm, w_in=w_in, conv_w=conv_w,
        gdn_a_log=gdn_a_log, gdn_dt_bias=gdn_dt_bias, gdn_norm=gdn_norm, w_up_a=w_up_a,
        diff_lambda=diff_lambda, diff_norm=diff_norm, w_up_b=w_up_b, w_out=w_out,
        rel_bias=rel_bias, xattn_norm=xattn_norm, mem_norm=mem_norm, xattn_wq=xattn_wq,
        xattn_wkv=xattn_wkv, xattn_wo=xattn_wo, ffn2_norm=ffn2_norm,
        ffn2_w_gate=ffn2_w_gate, ffn2_w_up=ffn2_w_up, ffn2_w_down=ffn2_w_down,
        final_norm=final_norm)
    y_prompt, y_sample = _encode((x_prompt, x_sample), (mem_prompt, mem_sample), params)
    return (y_prompt, y_sample)
```

```python
import functools
import math

import jax
import jax.numpy as jnp
from jax import lax
from jax.experimental import pallas as pl
from jax.experimental.pallas import tpu as pltpu

F32 = jnp.float32
BF16 = jnp.bfloat16
HIGHEST = lax.Precision.HIGHEST

EPS = 1e-6
LOG2E = math.log2(math.e)
D_MODEL = 1024
N_MEM = 256
GDN_HEADS = 4
GDN_DK = 128
GDN_DV = 128
GDN_CONV = 5
GDN_CHUNK = 64
DIFF_HEADS = 8
DIFF_DH = 64
DIFF_DV = 2 * DIFF_DH
N_BUCKETS = 32
MAX_DISTANCE = 128
X_HEADS = 4
X_DH = 128
D_FF = 2816
GDN_QK = GDN_HEADS * GDN_DK
GDN_V = GDN_HEADS * GDN_DV
DIFF_QK = DIFF_HEADS * 2 * DIFF_DH
DIFF_VW = DIFF_HEADS * DIFF_DV
N_GATES = 2 * GDN_HEADS

V7X_LANES = 128
V7X_SUBLANES = 8
V7X_VMEM_LIMIT = 56 * 1024 * 1024

FFN_CHUNK = 256
HALO = V7X_SUBLANES


def _cparams(sem):
    return pltpu.CompilerParams(dimension_semantics=sem, vmem_limit_bytes=V7X_VMEM_LIMIT)


def _dot(a, b, precision=None):
    return jnp.dot(a, b, preferred_element_type=F32, precision=precision)


def _dot_nt(a, b, precision=None):
    return lax.dot_general(a, b, (((1,), (1,)), ((), ())),
                           preferred_element_type=F32, precision=precision)


def _dot_tn(a, b, precision=None):
    return lax.dot_general(a, b, (((0,), (0,)), ((), ())),
                           preferred_element_type=F32, precision=precision)


def _rms(x, w):
    return x * lax.rsqrt(jnp.mean(x * x, axis=-1, keepdims=True) + EPS) * w


def _resident(shape):
    nd = len(shape)
    return pl.BlockSpec(shape, lambda *_: (0,) * nd, pipeline_mode=pl.Buffered(1))


def _ffn_body(x_ref, nw_ref, wg_ref, wu_ref, wd_ref, *rest, n_chunks, final):
    if final:
        fn_ref, o_ref, h_ref, acc_ref = rest
    else:
        o_ref, h_ref, acc_ref = rest
    x = x_ref[...]
    h_ref[...] = _rms(x, nw_ref[...]).astype(BF16)
    acc_ref[...] = jnp.zeros_like(acc_ref)

    def chunk(c, carry):
        h = h_ref[...]
        g = _dot(h, wg_ref[c])
        u = _dot(h, wu_ref[c])
        a = (g * jax.nn.sigmoid(g) * u).astype(BF16)
        acc_ref[...] += _dot(a, wd_ref[c])
        return carry

    lax.fori_loop(0, n_chunks, chunk, 0)
    y = x + 0.5 * acc_ref[...]
    if final:
        y = _rms(y, fn_ref[...])
    o_ref[...] = y


def _ffn(x, nw, wg, wu, wd, final_w, tm):
    t, d = x.shape
    nf = wg.shape[0]
    row = pl.BlockSpec((tm, d), lambda i: (i, 0))
    in_specs = [row, _resident((1, d)), _resident(wg.shape), _resident(wu.shape),
                _resident(wd.shape)]
    args = [x, nw, wg, wu, wd]
    if final_w is not None:
        in_specs.append(_resident((1, d)))
        args.append(final_w)
    return pl.pallas_call(
        functools.partial(_ffn_body, n_chunks=nf, final=final_w is not None),
        grid=(t // tm,),
        in_specs=in_specs,
        out_specs=row,
        out_shape=jax.ShapeDtypeStruct((t, d), F32),
        scratch_shapes=[pltpu.VMEM((tm, d), BF16), pltpu.VMEM((tm, d), F32)],
        compiler_params=_cparams(("parallel",)),
        name="ffn",
    )(*args)


_P_QKVA = (0, 3 * GDN_QK)
_P_GATE = (_P_QKVA[1], _P_QKVA[1] + V7X_LANES)
_P_Z = (_P_GATE[1], _P_GATE[1] + GDN_V)
_P_QB = (_P_Z[1], _P_Z[1] + DIFF_QK)
_P_KB = (_P_QB[1], _P_QB[1] + DIFF_QK)
_P_VB = (_P_KB[1], _P_KB[1] + DIFF_VW)
_P_SG = (_P_VB[1], _P_VB[1] + 2 * D_MODEL)
_P_COLS = _P_SG[1]


def _proj_body(x_ref, nw_ref, w_ref, qkva_ref, gate_ref, z_ref, qb_ref, kb_ref, vb_ref, sg_ref):
    u = _rms(x_ref[...], nw_ref[...]).astype(BF16)

    def seg(span):
        return _dot(u, w_ref[:, span[0]:span[1]])

    qkva_ref[...] = seg(_P_QKVA)
    gate_ref[...] = seg(_P_GATE)
    z_ref[...] = seg(_P_Z)
    qb_ref[...] = seg(_P_QB).astype(BF16)
    kb_ref[...] = seg(_P_KB).astype(BF16)
    vb_ref[...] = seg(_P_VB).astype(BF16)
    sg_ref[...] = jax.nn.sigmoid(seg(_P_SG))


def _proj(x, nw, w, tm):
    t, d = x.shape
    widths = [(_P_QKVA, F32), (_P_GATE, F32), (_P_Z, F32), (_P_QB, BF16), (_P_KB, BF16),
              (_P_VB, BF16), (_P_SG, F32)]
    out_shape = [jax.ShapeDtypeStruct((t, s[1] - s[0]), dt) for s, dt in widths]
    out_specs = [pl.BlockSpec((tm, s[1] - s[0]), lambda i: (i, 0)) for s, _ in widths]
    return pl.pallas_call(
        _proj_body,
        grid=(t // tm,),
        in_specs=[pl.BlockSpec((tm, d), lambda i: (i, 0)), _resident((1, d)), _resident(w.shape)],
        out_specs=out_specs,
        out_shape=out_shape,
        compiler_params=_cparams(("parallel",)),
        name="proj",
    )(x, nw, w)


def _gdnprep_body(cur_ref, prev_ref, next_ref, cw_ref, gate_ref, alog_ref, dtb_ref,
                  qkv_ref, bg_ref, *, tm, seq_tiles):
    i = pl.program_id(0)
    first = jnp.bool_(False)
    last = jnp.bool_(False)
    for start, per_seq in seq_tiles:
        rel = i - start
        first = first | ((rel >= 0) & (rel % per_seq == 0))
        last = last | ((rel >= 0) & (rel % per_seq == per_seq - 1))
    prev = jnp.where(first, 0.0, prev_ref[...])
    nxt = jnp.where(last, 0.0, next_ref[...])
    ext = jnp.concatenate([prev, cur_ref[...], nxt], axis=0)
    pad = (GDN_CONV - 1) // 2
    acc = None
    for k in range(GDN_CONV):
        lo = HALO - pad + k
        term = ext[lo:lo + tm, :] * cw_ref[k:k + 1, :]
        acc = term if acc is None else acc + term
    y = acc * jax.nn.sigmoid(acc)
    for h in range(2 * GDN_HEADS):
        lo = h * GDN_DK
        yh = y[:, lo:lo + GDN_DK]
        nrm = yh * lax.rsqrt(jnp.sum(yh * yh, axis=-1, keepdims=True) + EPS)
        if h < GDN_HEADS:
            nrm = nrm * (GDN_DK ** -0.5)
        qkv_ref[:, lo:lo + GDN_DK] = nrm
    qkv_ref[:, 2 * GDN_QK:] = y[:, 2 * GDN_QK:]
    gl = gate_ref[...]
    beta = jax.nn.sigmoid(gl)
    xa = gl + dtb_ref[...]
    softplus = jnp.maximum(xa, 0.0) + jnp.log1p(jnp.exp(-jnp.abs(xa)))
    g = -jnp.exp(alog_ref[...]) * softplus
    lane = lax.broadcasted_iota(jnp.int32, gl.shape, 1)
    bg_ref[...] = jnp.where(lane < N_GATES, beta, g)


def _gdnprep(qkva, conv_w8, gates, alog_pad, dtb_pad, tm, seq_tiles):
    t, c = qkva.shape
    hb = tm // HALO
    nblk = t // HALO
    return pl.pallas_call(
        functools.partial(_gdnprep_body, tm=tm, seq_tiles=seq_tiles),
        grid=(t // tm,),
        in_specs=[
            pl.BlockSpec((tm, c), lambda i: (i, 0)),
            pl.BlockSpec((HALO, c), lambda i: (jnp.maximum(i * hb - 1, 0), 0)),
            pl.BlockSpec((HALO, c), lambda i: (jnp.minimum((i + 1) * hb, nblk - 1), 0)),
            _resident(conv_w8.shape),
            pl.BlockSpec((tm, V7X_LANES), lambda i: (i, 0)),
            _resident((1, V7X_LANES)),
            _resident((1, V7X_LANES)),
        ],
        out_specs=[pl.BlockSpec((tm, c), lambda i: (i, 0)),
                   pl.BlockSpec((tm, V7X_LANES), lambda i: (i, 0))],
        out_shape=[jax.ShapeDtypeStruct((t, c), F32),
                   jax.ShapeDtypeStruct((t, V7X_LANES), F32)],
        compiler_params=_cparams(("parallel",)),
        name="gdnprep",
    )(qkva, qkva, qkva, conv_w8, gates, alog_pad, dtb_pad)


def _gdn_body(qkv_f, bgc_f, bgr_f, qkv_b, bgc_b, bgr_b, *rest, n_chunks):
    of_ref, ob_ref, s_ref = rest[-3:]
    c64 = GDN_CHUNK

    @pl.when(pl.program_id(1) == 0)
    def _():
        s_ref[...] = jnp.zeros_like(s_ref)

    row = lax.broadcasted_iota(jnp.int32, (c64, c64), 0)
    col = lax.broadcasted_iota(jnp.int32, (c64, c64), 1)
    eye = (row == col).astype(F32)
    incl = (row >= col, row <= col)
    strict = (row > col, row < col)

    for c in range(n_chunks):
        for d in range(2):
            qkv_ref, bgc_ref, bgr_ref, o_ref = (
                (qkv_f, bgc_f, bgr_f, of_ref), (qkv_b, bgc_b, bgr_b, ob_ref))[d]
            r0 = (c if d == 0 else n_chunks - 1 - c) * c64
            tri_c = incl[d].astype(F32)
            tri_r = incl[1 - d].astype(F32)
            gt = bgc_ref[r0:r0 + c64, :]
            gcs = _dot(tri_c, gt, HIGHEST)
            grs = _dot(bgr_ref[:, r0:r0 + c64], tri_r, HIGHEST)
            g_last = gcs[c64 - 1:c64, :] if d == 0 else gcs[0:1, :]
            egc = jnp.exp(gcs)
            ekd = jnp.exp(g_last - gcs)
            egl = jnp.exp(g_last)
            for h in range(GDN_HEADS):
                idx = d * GDN_HEADS + h
                gi = N_GATES + idx
                q = qkv_ref[r0:r0 + c64, h * GDN_DK:(h + 1) * GDN_DK]
                k = qkv_ref[r0:r0 + c64, GDN_QK + h * GDN_DK:GDN_QK + (h + 1) * GDN_DK]
                v = qkv_ref[r0:r0 + c64, 2 * GDN_QK + h * GDN_DV:2 * GDN_QK + (h + 1) * GDN_DV]
                beta = gt[:, idx:idx + 1]
                diff = gcs[:, gi:gi + 1] - grs[gi:gi + 1, :]
                gam = jnp.where(incl[d], jnp.exp(jnp.where(incl[d], diff, 0.0)), 0.0)
                kb = k * beta
                vb = v * beta
                k16 = k.astype(BF16)
                kk = _dot_nt(kb.astype(BF16), k16)
                m = -jnp.where(strict[d], kk * gam, 0.0)
                x = eye + m
                mp = m
                for _ in range(5):
                    mp = _dot(mp, mp, HIGHEST)
                    x = x + _dot(x, mp, HIGHEST)
                rhs = jnp.concatenate([vb, kb * egc[:, gi:gi + 1]], axis=1)
                sol = _dot(x, rhs, HIGHEST)
                u = sol[:, :GDN_DV]
                w = sol[:, GDN_DV:]
                aqk = _dot_nt(q.astype(BF16), k16) * gam
                qd = q * egc[:, gi:gi + 1]
                kd = k * ekd[:, gi:gi + 1]
                s = s_ref[idx]
                s16 = s.astype(BF16)
                v_new = u - _dot(w.astype(BF16), s16)
                vn16 = v_new.astype(BF16)
                o = _dot(qd.astype(BF16), s16) + _dot(aqk.astype(BF16), vn16)
                s_ref[idx] = s * egl[:, gi:gi + 1] + _dot_tn(kd.astype(BF16), vn16)
                o_ref[r0:r0 + c64, h * GDN_DV:(h + 1) * GDN_DV] = o


def _gdn(qkvn, bg, bg_rows, row_off, batch, seqlen, ts, prev):
    t, c = qkvn.shape
    ns = seqlen // ts
    off = row_off // ts
    fwd = lambda b, s: (off + b * ns + s, 0)
    bwd = lambda b, s: (off + b * ns + ns - 1 - s, 0)
    fwd_r = lambda b, s: (0, off + b * ns + s)
    bwd_r = lambda b, s: (0, off + b * ns + ns - 1 - s)
    nr = bg_rows.shape[0]
    out = jax.ShapeDtypeStruct((t, GDN_V), F32)
    in_specs = [
        pl.BlockSpec((ts, c), fwd), pl.BlockSpec((ts, V7X_LANES), fwd),
        pl.BlockSpec((nr, ts), fwd_r),
        pl.BlockSpec((ts, c), bwd), pl.BlockSpec((ts, V7X_LANES), bwd),
        pl.BlockSpec((nr, ts), bwd_r),
    ]
    args = [qkvn, bg, bg_rows, qkvn, bg, bg_rows]
    aliases = {}
    if prev is not None:
        aliases = {len(args): 0, len(args) + 1: 1}
        in_specs += [pl.BlockSpec(memory_space=pl.ANY)] * 2
        args += list(prev)
    return pl.pallas_call(
        functools.partial(_gdn_body, n_chunks=ts // GDN_CHUNK),
        grid=(batch, ns),
        in_specs=in_specs,
        out_specs=[pl.BlockSpec((ts, GDN_V), fwd), pl.BlockSpec((ts, GDN_V), bwd)],
        out_shape=[out, out],
        scratch_shapes=[pltpu.VMEM((2 * GDN_HEADS, GDN_DK, GDN_DV), F32)],
        input_output_aliases=aliases,
        compiler_params=_cparams(("parallel", "arbitrary")),
        name="gdn",
    )(*args)


def _attn_body(far_ref, q_ref, k_ref, vt_ref, band_ref, lam_ref, nw_ref, *rest,
               lam_init, tk, nk):
    o_ref, qz_ref, s_ref, mx_ref, m_ref, l_ref, acc_ref = rest[-7:]
    h = pl.program_id(1)
    qi = pl.program_id(2)
    tq = q_ref.shape[0]
    sub = V7X_SUBLANES

    m_ref[...] = jnp.full_like(m_ref, -1e30)
    l_ref[...] = jnp.zeros_like(l_ref)
    acc_ref[...] = jnp.zeros_like(acc_ref)
    q = q_ref[...]
    lane = lax.broadcasted_iota(jnp.int32, q.shape, 1)
    zero = jnp.zeros_like(q)
    qz_ref[0] = jnp.where(lane < DIFF_DH, q, zero)
    qz_ref[1] = jnp.where(lane >= DIFF_DH, q, zero)

    def scores(kt, slot, bias_tile):
        k = k_ref[pl.ds(pl.multiple_of(kt * tk, tk), tk), :]
        for mp in range(2):
            s = _dot_nt(k, qz_ref[mp])
            if bias_tile is not None:
                s = s + bias_tile
            s_ref[slot, mp] = s
            mx_ref[slot, mp] = jnp.max(s.reshape(tk // sub, sub, tq), axis=0)

    def accumulate(kt, slot, c):
        vt = vt_ref[:, pl.ds(pl.multiple_of(kt * tk, tk), tk)]
        for mp in range(2):
            m_cur = jnp.max(mx_ref[slot, mp], axis=0, keepdims=True) + c
            m_prev = m_ref[mp]
            m_new = jnp.maximum(m_prev, m_cur)
            alpha = jnp.exp2(m_prev - m_new)
            m_ref[mp] = m_new
            p = jnp.exp2(s_ref[slot, mp] - (m_new - c))
            l_ref[mp] = alpha * l_ref[mp] + jnp.sum(p.reshape(tk // sub, sub, tq), axis=0)
            acc_ref[mp] = alpha * acc_ref[mp] + _dot(vt, p.astype(BF16))

    n_left = jnp.maximum(qi - 1, 0)
    right0 = jnp.minimum(qi + 2, nk)
    n_far = n_left + (nk - right0)

    def far_tile(f):
        return jnp.where(f < n_left, f, right0 + (f - n_left))

    @pl.when(n_far > 0)
    def _():
        scores(far_tile(0), 0, None)

    def far_step(f, slot):
        nxt = jnp.minimum(f + 1, n_far - 1)
        scores(far_tile(nxt), 1 - slot, None)
        accumulate(far_tile(f), slot, jnp.where(f < n_left, far_ref[h, 0], far_ref[h, 1]))

    def far_pair(g, carry):
        far_step(2 * g, 0)
        far_step(2 * g + 1, 1)
        return carry

    lax.fori_loop(0, n_far // 2, far_pair, 0)

    @pl.when(n_far % 2 == 1)
    def _():
        far_step(n_far - 1, 0)

    for dd in range(3):
        kt = qi - 1 + dd

        @pl.when((kt >= 0) & (kt < nk))
        def _():
            scores(kt, 0, band_ref[dd])
            accumulate(kt, 0, 0.0)

    lam = lam_ref[...]
    lam_full = (jnp.exp(jnp.sum(lam[0:1] * lam[1:2], keepdims=True))
                - jnp.exp(jnp.sum(lam[2:3] * lam[3:4], keepdims=True)) + lam_init)
    l0 = jnp.sum(l_ref[0], axis=0, keepdims=True)
    l1 = jnp.sum(l_ref[1], axis=0, keepdims=True)
    o = acc_ref[0] / l0 - lam_full * (acc_ref[1] / l1)
    o = o * lax.rsqrt(jnp.mean(o * o, axis=0, keepdims=True) + EPS)
    o = o * (nw_ref[...] * (1.0 - lam_init))
    o_ref[...] = o.T


def _attn(qb, kb, vbt, band, far, lam, nw_col, row_off, batch, seqlen, tq, lam_init, prev):
    t = qb.shape[0]
    nq = seqlen // tq
    off = row_off // tq
    off_seq = row_off // seqlen
    assert row_off % seqlen == 0
    qmap = lambda b, h, qi: (off + b * nq + qi, h)
    in_specs = [
        pl.BlockSpec(memory_space=pltpu.SMEM),
        pl.BlockSpec((tq, DIFF_DV), qmap),
        pl.BlockSpec((seqlen, DIFF_DV), lambda b, h, qi: (off_seq + b, h)),
        pl.BlockSpec((DIFF_DV, seqlen), lambda b, h, qi: (h, off_seq + b)),
        pl.BlockSpec((None, 3, tq, tq), lambda b, h, qi: (h, 0, 0, 0)),
        pl.BlockSpec(lam.shape, lambda b, h, qi: (0, 0)),
        pl.BlockSpec((DIFF_DV, 1), lambda b, h, qi: (0, 0)),
    ]
    args = [far, qb, kb, vbt, band, lam, nw_col]
    aliases = {}
    if prev is not None:
        aliases = {len(args): 0}
        in_specs.append(pl.BlockSpec(memory_space=pl.ANY))
        args.append(prev)
    return pl.pallas_call(
        functools.partial(_attn_body, lam_init=lam_init, tk=tq, nk=nq),
        grid=(batch, DIFF_HEADS, nq),
        in_specs=in_specs,
        input_output_aliases=aliases,
        out_specs=pl.BlockSpec((tq, DIFF_DV), qmap),
        out_shape=jax.ShapeDtypeStruct((t, DIFF_VW), F32),
        scratch_shapes=[pltpu.VMEM((2, tq, DIFF_DV), BF16),
                        pltpu.VMEM((2, 2, tq, tq), F32),
                        pltpu.VMEM((2, 2, V7X_SUBLANES, tq), F32),
                        pltpu.VMEM((2, 1, tq), F32),
                        pltpu.VMEM((2, V7X_SUBLANES, tq), F32),
                        pltpu.VMEM((2, DIFF_DV, tq), F32)],
        compiler_params=_cparams(("parallel", "parallel", "parallel")),
        name="diffattn",
    )(*args)


def _memkv_body(m_ref, nw_ref, w_ref, o_ref):
    h = _rms(m_ref[...], nw_ref[...]).astype(BF16)
    o_ref[...] = _dot(h, w_ref[...]).astype(BF16)


def _memkv(mem, nw, wkv):
    nb, nm, d = mem.shape
    return pl.pallas_call(
        _memkv_body,
        grid=(nb,),
        in_specs=[pl.BlockSpec((None, nm, d), lambda b: (b, 0, 0)), _resident((1, d)),
                  _resident(wkv.shape)],
        out_specs=pl.BlockSpec((None, nm, wkv.shape[1]), lambda b: (b, 0, 0)),
        out_shape=jax.ShapeDtypeStruct((nb, nm, wkv.shape[1]), BF16),
        compiler_params=_cparams(("parallel",)),
        name="memkv",
    )(mem, nw, wkv)


def _mergex_body(x_ref, of_ref, ob_ref, z_ref, oattn_ref, sg_ref, kv_ref, gn_ref, wua_ref,
                 wub_ref, wout_ref, xn_ref, wq_ref, wo_ref, o_ref):
    o = of_ref[...] + ob_ref[...]
    z = z_ref[...]
    gn = gn_ref[...]
    heads = []
    for h in range(GDN_HEADS):
        sl = slice(h * GDN_DV, (h + 1) * GDN_DV)
        zh = z[:, sl]
        heads.append(_rms(o[:, sl], gn) * (zh * jax.nn.sigmoid(zh)))
    oa = jnp.concatenate(heads, axis=1).astype(BF16)
    ya = _dot(oa, wua_ref[...])
    yb = _dot(oattn_ref[...].astype(BF16), wub_ref[...])
    sg = sg_ref[...]
    merged = sg[:, :D_MODEL] * ya + sg[:, D_MODEL:] * yb
    x = x_ref[...] + _dot(merged.astype(BF16), wout_ref[...])
    hq = _rms(x, xn_ref[...]).astype(BF16)
    q = _dot(hq, wq_ref[...]) * (X_DH ** -0.5)
    kv = kv_ref[...]
    outs = []
    for h in range(X_HEADS):
        sl = slice(h * X_DH, (h + 1) * X_DH)
        kh = kv[:, sl]
        vh = kv[:, X_HEADS * X_DH + h * X_DH:X_HEADS * X_DH + (h + 1) * X_DH]
        s = _dot_nt(q[:, sl].astype(BF16), kh)
        s = s - jnp.max(s, axis=-1, keepdims=True)
        p = jnp.exp(s)
        p = p / jnp.sum(p, axis=-1, keepdims=True)
        outs.append(_dot(p.astype(BF16), vh))
    ox = jnp.concatenate(outs, axis=1).astype(BF16)
    o_ref[...] = x + _dot(ox, wo_ref[...])


def _mergex(x, o_f, o_b, z, oattn, sg, kv, gn, wua, wub, wout, xn, wq, wo, tm, tile_batch):
    t, d = x.shape
    rows = lambda width: pl.BlockSpec((tm, width), lambda i: (i, 0))
    return pl.pallas_call(
        _mergex_body,
        grid=(t // tm,),
        in_specs=[rows(d), rows(GDN_V), rows(GDN_V), rows(GDN_V), rows(DIFF_VW), rows(2 * d),
                  pl.BlockSpec((None,) + kv.shape[1:], lambda i: (tile_batch(i), 0, 0)),
                  _resident(gn.shape), _resident(wua.shape), _resident(wub.shape),
                  _resident(wout.shape), _resident(xn.shape), _resident(wq.shape),
                  _resident(wo.shape)],
        out_specs=rows(d),
        out_shape=jax.ShapeDtypeStruct((t, d), F32),
        compiler_params=_cparams(("parallel",)),
        name="mergex",
    )(x, o_f, o_b, z, oattn, sg, kv, gn, wua, wub, wout, xn, wq, wo)


def _rel_bucket(rel):
    nb = N_BUCKETS // 2
    ret = jnp.where(rel > 0, nb, 0)
    n = jnp.abs(rel)
    max_exact = nb // 2
    nf = jnp.maximum(n, 1).astype(F32)
    large = max_exact + (jnp.log(nf / max_exact) / math.log(MAX_DISTANCE / max_exact)
                         * (nb - max_exact)).astype(jnp.int32)
    large = jnp.minimum(large, nb - 1)
    return ret + jnp.where(n < max_exact, n, large)


def _bias_tables(rel_bias, tq):
    assert tq >= MAX_DISTANCE
    rel = jnp.arange(-(2 * tq - 1), 2 * tq, dtype=jnp.int32)
    by_rel = rel_bias[_rel_bucket(rel)].astype(F32)
    i = jnp.arange(tq, dtype=jnp.int32)[:, None]
    j = jnp.arange(tq, dtype=jnp.int32)[None, :]
    idx = jnp.stack([(dd - 1) * tq + i - j + (2 * tq - 1) for dd in range(3)])
    by_rel = by_rel * LOG2E
    band = jnp.moveaxis(by_rel[idx], -1, 0)
    far = jnp.stack([by_rel[0], by_rel[-1]], axis=1)
    return band, far


def _pick(limit, n):
    tile = limit
    while n % tile:
        tile //= 2
    return tile


def _encode(xs, mems, p, tile_limits=None):
    lim = dict(ffn=512, proj=256, prep=256, gdn=256, attn=512, mergex=256)
    if tile_limits:
        lim.update(tile_limits)
    depth = p["w_in"].shape[0]
    d = D_MODEL
    groups = []
    row = 0
    bat = 0
    for x in xs:
        b, l, _ = x.shape
        groups.append(dict(b=b, l=l, row=row, bat=bat))
        row += b * l
        bat += b
    t_all = row
    seqlens = [g["l"] for g in groups]
    common = functools.reduce(math.gcd, seqlens)
    tiles = {k: _pick(v, common) for k, v in lim.items()}

    x = jnp.concatenate([xx.reshape(-1, d) for xx in xs], axis=0)
    mem = jnp.concatenate(mems, axis=0)

    def tile_batch(i):
        r0 = i * tiles["mergex"]
        bidx = 0
        for g in groups:
            bidx = jnp.where(r0 >= g["row"], g["bat"] + (r0 - g["row"]) // g["l"], bidx)
        return bidx

    seq_tiles = tuple((g["row"] // tiles["prep"], g["l"] // tiles["prep"]) for g in groups)
    band, far = _bias_tables(p["rel_bias"], tiles["attn"])
    nf = D_FF // FFN_CHUNK

    def ffn_weights(prefix, i):
        wg = p[prefix + "_w_gate"][i].reshape(d, nf, FFN_CHUNK).transpose(1, 0, 2).astype(BF16)
        wu = p[prefix + "_w_up"][i].reshape(d, nf, FFN_CHUNK).transpose(1, 0, 2).astype(BF16)
        wd = p[prefix + "_w_down"][i].reshape(nf, FFN_CHUNK, d).astype(BF16)
        return p[prefix + "_norm"][i].reshape(1, d), wg, wu, wd

    for i in range(depth):
        x = _ffn(x, *ffn_weights("ffn1", i), None, tiles["ffn"])

        w_in = p["w_in"][i]
        o_beta = 3 * GDN_QK
        o_z = o_beta + 2 * N_GATES
        o_qb = o_z + GDN_V
        gate_w = jnp.pad(w_in[:, o_beta:o_z], ((0, 0), (0, V7X_LANES - 2 * N_GATES)))
        w_perm = jnp.concatenate(
            [w_in[:, :o_beta], gate_w, w_in[:, o_z:o_qb],
             w_in[:, o_qb:o_qb + DIFF_QK] * (DIFF_DH ** -0.5 * LOG2E),
             w_in[:, o_qb + DIFF_QK:]],
            axis=1).astype(BF16)
        assert w_perm.shape[1] == _P_COLS
        qkva, gates, z, qb, kb, vb, sg = _proj(x, p["mix_norm"][i].reshape(1, d), w_perm,
                                               tiles["proj"])

        conv_w8 = jnp.pad(p["conv_w"][i], ((0, HALO - GDN_CONV), (0, 0)))
        lane_pad = (N_GATES, V7X_LANES - 2 * N_GATES)
        alog_pad = jnp.pad(p["gdn_a_log"][i].reshape(-1), lane_pad).reshape(1, V7X_LANES)
        dtb_pad = jnp.pad(p["gdn_dt_bias"][i].reshape(-1), lane_pad).reshape(1, V7X_LANES)
        qkvn, bg = _gdnprep(qkva, conv_w8, gates, alog_pad, dtb_pad, tiles["prep"], seq_tiles)
        bg_rows = bg[:, :2 * N_GATES].T

        lam_init = 0.8 - 0.6 * math.exp(-0.3 * i)
        o_fb = o_attn = None
        vbt = vb.T
        for g in groups:
            o_fb = _gdn(qkvn, bg, bg_rows, g["row"], g["b"], g["l"], tiles["gdn"], o_fb)
            o_attn = _attn(qb, kb, vbt, band, far, p["diff_lambda"][i],
                           p["diff_norm"][i].reshape(DIFF_DV, 1), g["row"], g["b"], g["l"],
                           tiles["attn"], lam_init, o_attn)
        o_f, o_b = o_fb

        kv = _memkv(mem, p["mem_norm"][i].reshape(1, d), p["xattn_wkv"][i].astype(BF16))
        x = _mergex(x, o_f, o_b, z, o_attn, sg, kv, p["gdn_norm"][i].reshape(1, GDN_DV),
                    p["w_up_a"][i].astype(BF16), p["w_up_b"][i].astype(BF16),
                    p["w_out"][i].astype(BF16), p["xattn_norm"][i].reshape(1, d),
                    p["xattn_wq"][i].astype(BF16), p["xattn_wo"][i].astype(BF16),
                    tiles["mergex"], tile_batch)

        final_w = p["final_norm"].reshape(1, d) if i == depth - 1 else None
        x = _ffn(x, *ffn_weights("ffn2", i), final_w, tiles["ffn"])

    return tuple(x[g["row"]:g["row"] + g["b"] * g["l"]].reshape(g["b"], g["l"], d)
                 for g in groups)


def kernel(x_prompt, x_sample, mem_prompt, mem_sample, ffn1_norm, ffn1_w_gate, ffn1_w_up, ffn1_w_down, mix_norm, w_in, conv_w, gdn_a_log, gdn_dt_bias, gdn_norm, w_up_a, diff_lambda, diff_norm, w_up_b, w_out, rel_bias, xattn_norm, mem_norm, xattn_wq, xattn_wkv, xattn_wo, ffn2_norm, ffn2_w_gate, ffn2_w_up, ffn2_w_down, final_norm):
    params = dict(
        ffn1_norm=ffn1_norm, ffn1_w_gate=ffn1_w_gate, ffn1_w_up=ffn1_w_up,
        ffn1_w_down=ffn1_w_down, mix_norm=mix_norm, w_in=w_in, conv_w=conv_w,
        gdn_a_log=gdn_a_log, gdn_dt_bias=gdn_dt_bias, gdn_norm=gdn_norm, w_up_a=w_up_a,
        diff_lambda=diff_lambda, diff_norm=diff_norm, w_up_b=w_up_b, w_out=w_out,
        rel_bias=rel_bias, xattn_norm=xattn_norm, mem_norm=mem_norm, xattn_wq=xattn_wq,
        xattn_wkv=xattn_wkv, xattn_wo=xattn_wo, ffn2_norm=ffn2_norm,
        ffn2_w_gate=ffn2_w_gate, ffn2_w_up=ffn2_w_up, ffn2_w_down=ffn2_w_down,
        final_norm=final_norm)
    y_prompt, y_sample = _encode((x_prompt, x_sample), (mem_prompt, mem_sample), params)
    return (y_prompt, y_sample)
```

```python
import functools
import math

import jax
import jax.numpy as jnp
from jax import lax
from jax.experimental import pallas as pl
from jax.experimental.pallas import tpu as pltpu

F32 = jnp.float32
BF16 = jnp.bfloat16
HIGHEST = lax.Precision.HIGHEST

EPS = 1e-6
LOG2E = math.log2(math.e)
D_MODEL = 1024
N_MEM = 256
GDN_HEADS = 4
GDN_DK = 128
GDN_DV = 128
GDN_CONV = 5
GDN_CHUNK = 64
DIFF_HEADS = 8
DIFF_DH = 64
DIFF_DV = 2 * DIFF_DH
N_BUCKETS = 32
MAX_DISTANCE = 128
X_HEADS = 4
X_DH = 128
D_FF = 2816
GDN_QK = GDN_HEADS * GDN_DK
GDN_V = GDN_HEADS * GDN_DV
DIFF_QK = DIFF_HEADS * 2 * DIFF_DH
DIFF_VW = DIFF_HEADS * DIFF_DV
N_GATES = 2 * GDN_HEADS

V7X_LANES = 128
V7X_SUBLANES = 8
V7X_VMEM_LIMIT = 56 * 1024 * 1024

FFN_CHUNK = 256
HALO = V7X_SUBLANES


def _cparams(sem):
    return pltpu.CompilerParams(dimension_semantics=sem, vmem_limit_bytes=V7X_VMEM_LIMIT)


def _dot(a, b, precision=None):
    return jnp.dot(a, b, preferred_element_type=F32, precision=precision)


def _dot_nt(a, b, precision=None):
    return lax.dot_general(a, b, (((1,), (1,)), ((), ())),
                           preferred_element_type=F32, precision=precision)


def _dot_tn(a, b, precision=None):
    return lax.dot_general(a, b, (((0,), (0,)), ((), ())),
                           preferred_element_type=F32, precision=precision)


def _dot_split(a, b):
    a_hi = a.astype(BF16)
    b_hi = b.astype(BF16)
    a_lo = (a - a_hi.astype(F32)).astype(BF16)
    b_lo = (b - b_hi.astype(F32)).astype(BF16)
    return _dot(a_hi, b_hi) + (_dot(a_hi, b_lo) + _dot(a_lo, b_hi))


def _rms(x, w):
    return x * lax.rsqrt(jnp.mean(x * x, axis=-1, keepdims=True) + EPS) * w


def _resident(shape):
    nd = len(shape)
    return pl.BlockSpec(shape, lambda *_: (0,) * nd, pipeline_mode=pl.Buffered(1))


def _ffn_body(x_ref, nw_ref, wg_ref, wu_ref, wd_ref, *rest, n_chunks, final):
    if final:
        fn_ref, o_ref, h_ref, acc_ref = rest
    else:
        o_ref, h_ref, acc_ref = rest
    x = x_ref[...]
    h_ref[...] = _rms(x, nw_ref[...]).astype(BF16)
    acc_ref[...] = jnp.zeros_like(acc_ref)

    def chunk(c, carry):
        h = h_ref[...]
        g = _dot(h, wg_ref[c])
        u = _dot(h, wu_ref[c])
        a = (g * jax.nn.sigmoid(g) * u).astype(BF16)
        acc_ref[...] += _dot(a, wd_ref[c])
        return carry

    lax.fori_loop(0, n_chunks, chunk, 0)
    y = x + 0.5 * acc_ref[...]
    if final:
        y = _rms(y, fn_ref[...])
    o_ref[...] = y


def _ffn(x, nw, wg, wu, wd, final_w, tm):
    t, d = x.shape
    nf = wg.shape[0]
    row = pl.BlockSpec((tm, d), lambda i: (i, 0))
    in_specs = [row, _resident((1, d)), _resident(wg.shape), _resident(wu.shape),
                _resident(wd.shape)]
    args = [x, nw, wg, wu, wd]
    if final_w is not None:
        in_specs.append(_resident((1, d)))
        args.append(final_w)
    return pl.pallas_call(
        functools.partial(_ffn_body, n_chunks=nf, final=final_w is not None),
        grid=(t // tm,),
        in_specs=in_specs,
        out_specs=row,
        out_shape=jax.ShapeDtypeStruct((t, d), F32),
        scratch_shapes=[pltpu.VMEM((tm, d), BF16), pltpu.VMEM((tm, d), F32)],
        compiler_params=_cparams(("parallel",)),
        name="ffn",
    )(*args)


_P_QKVA = (0, 3 * GDN_QK)
_P_GATE = (_P_QKVA[1], _P_QKVA[1] + V7X_LANES)
_P_Z = (_P_GATE[1], _P_GATE[1] + GDN_V)
_P_QB = (_P_Z[1], _P_Z[1] + DIFF_QK)
_P_KB = (_P_QB[1], _P_QB[1] + DIFF_QK)
_P_VB = (_P_KB[1], _P_KB[1] + DIFF_VW)
_P_SG = (_P_VB[1], _P_VB[1] + 2 * D_MODEL)
_P_COLS = _P_SG[1]


def _proj_body(x_ref, nw_ref, w_ref, qkva_ref, gate_ref, z_ref, qb_ref, kb_ref, vb_ref, sg_ref):
    u = _rms(x_ref[...], nw_ref[...]).astype(BF16)

    def seg(span):
        return _dot(u, w_ref[:, span[0]:span[1]])

    qkva_ref[...] = seg(_P_QKVA)
    gate_ref[...] = seg(_P_GATE)
    z_ref[...] = seg(_P_Z)
    qb_ref[...] = seg(_P_QB).astype(BF16)
    kb_ref[...] = seg(_P_KB).astype(BF16)
    vb_ref[...] = seg(_P_VB).astype(BF16)
    sg_ref[...] = jax.nn.sigmoid(seg(_P_SG))


def _proj(x, nw, w, tm):
    t, d = x.shape
    widths = [(_P_QKVA, F32), (_P_GATE, F32), (_P_Z, F32), (_P_QB, BF16), (_P_KB, BF16),
              (_P_VB, BF16), (_P_SG, F32)]
    out_shape = [jax.ShapeDtypeStruct((t, s[1] - s[0]), dt) for s, dt in widths]
    out_specs = [pl.BlockSpec((tm, s[1] - s[0]), lambda i: (i, 0)) for s, _ in widths]
    return pl.pallas_call(
        _proj_body,
        grid=(t // tm,),
        in_specs=[pl.BlockSpec((tm, d), lambda i: (i, 0)), _resident((1, d)), _resident(w.shape)],
        out_specs=out_specs,
        out_shape=out_shape,
        compiler_params=_cparams(("parallel",)),
        name="proj",
    )(x, nw, w)


def _gdnprep_body(cur_ref, prev_ref, next_ref, cw_ref, gate_ref, alog_ref, dtb_ref,
                  qkv_ref, bg_ref, *, tm, seq_tiles):
    i = pl.program_id(0)
    first = jnp.bool_(False)
    last = jnp.bool_(False)
    for start, per_seq in seq_tiles:
        rel = i - start
        first = first | ((rel >= 0) & (rel % per_seq == 0))
        last = last | ((rel >= 0) & (rel % per_seq == per_seq - 1))
    prev = jnp.where(first, 0.0, prev_ref[...])
    nxt = jnp.where(last, 0.0, next_ref[...])
    ext = jnp.concatenate([prev, cur_ref[...], nxt], axis=0)
    pad = (GDN_CONV - 1) // 2
    acc = None
    for k in range(GDN_CONV):
        lo = HALO - pad + k
        term = ext[lo:lo + tm, :] * cw_ref[k:k + 1, :]
        acc = term if acc is None else acc + term
    y = acc * jax.nn.sigmoid(acc)
    for h in range(2 * GDN_HEADS):
        lo = h * GDN_DK
        yh = y[:, lo:lo + GDN_DK]
        nrm = yh * lax.rsqrt(jnp.sum(yh * yh, axis=-1, keepdims=True) + EPS)
        if h < GDN_HEADS:
            nrm = nrm * (GDN_DK ** -0.5)
        qkv_ref[:, lo:lo + GDN_DK] = nrm
    qkv_ref[:, 2 * GDN_QK:] = y[:, 2 * GDN_QK:]
    gl = gate_ref[...]
    beta = jax.nn.sigmoid(gl)
    xa = gl + dtb_ref[...]
    softplus = jnp.maximum(xa, 0.0) + jnp.log1p(jnp.exp(-jnp.abs(xa)))
    g = -jnp.exp(alog_ref[...]) * softplus
    lane = lax.broadcasted_iota(jnp.int32, gl.shape, 1)
    bg_ref[...] = jnp.where(lane < N_GATES, beta, g)


def _gdnprep(qkva, conv_w8, gates, alog_pad, dtb_pad, tm, seq_tiles):
    t, c = qkva.shape
    hb = tm // HALO
    nblk = t // HALO
    return pl.pallas_call(
        functools.partial(_gdnprep_body, tm=tm, seq_tiles=seq_tiles),
        grid=(t // tm,),
        in_specs=[
            pl.BlockSpec((tm, c), lambda i: (i, 0)),
            pl.BlockSpec((HALO, c), lambda i: (jnp.maximum(i * hb - 1, 0), 0)),
            pl.BlockSpec((HALO, c), lambda i: (jnp.minimum((i + 1) * hb, nblk - 1), 0)),
            _resident(conv_w8.shape),
            pl.BlockSpec((tm, V7X_LANES), lambda i: (i, 0)),
            _resident((1, V7X_LANES)),
            _resident((1, V7X_LANES)),
        ],
        out_specs=[pl.BlockSpec((tm, c), lambda i: (i, 0)),
                   pl.BlockSpec((tm, V7X_LANES), lambda i: (i, 0))],
        out_shape=[jax.ShapeDtypeStruct((t, c), F32),
                   jax.ShapeDtypeStruct((t, V7X_LANES), F32)],
        compiler_params=_cparams(("parallel",)),
        name="gdnprep",
    )(qkva, qkva, qkva, conv_w8, gates, alog_pad, dtb_pad)


def _gdn_body(qkv_f, bgc_f, bgr_f, qkv_b, bgc_b, bgr_b, *rest, n_chunks):
    of_ref, ob_ref, s_ref = rest[-3:]
    c64 = GDN_CHUNK

    @pl.when(pl.program_id(1) == 0)
    def _():
        s_ref[...] = jnp.zeros_like(s_ref)

    row = lax.broadcasted_iota(jnp.int32, (c64, c64), 0)
    col = lax.broadcasted_iota(jnp.int32, (c64, c64), 1)
    eye = (row == col).astype(F32)
    incl = (row >= col, row <= col)
    strict = (row > col, row < col)

    refs = ((qkv_f, bgc_f, bgr_f, of_ref), (qkv_b, bgc_b, bgr_b, ob_ref))
    chains = [(d, h) for d in range(2) for h in range(GDN_HEADS)]

    def row0(c, d):
        return (c if d == 0 else n_chunks - 1 - c) * c64

    def prepare(c):
        gates = []
        for d in range(2):
            _, bgc_ref, bgr_ref, _ = refs[d]
            r0 = row0(c, d)
            gt = bgc_ref[r0:r0 + c64, :]
            gcs = _dot(incl[d].astype(F32), gt, HIGHEST)
            grs = _dot(bgr_ref[:, r0:r0 + c64], incl[1 - d].astype(F32), HIGHEST)
            g_last = gcs[c64 - 1:c64, :] if d == 0 else gcs[0:1, :]
            gates.append((gt, gcs, grs, jnp.exp(gcs), jnp.exp(g_last - gcs), jnp.exp(g_last)))
        q, k, k16, kb, vb, gam, egc_c, ekd_c, egl_c = ([] for _ in range(9))
        for d, h in chains:
            qkv_ref = refs[d][0]
            r0 = row0(c, d)
            gt, gcs, grs, egc, ekd, egl = gates[d]
            idx = d * GDN_HEADS + h
            gi = N_GATES + idx
            qq = qkv_ref[r0:r0 + c64, h * GDN_DK:(h + 1) * GDN_DK]
            kk_ = qkv_ref[r0:r0 + c64, GDN_QK + h * GDN_DK:GDN_QK + (h + 1) * GDN_DK]
            vv = qkv_ref[r0:r0 + c64, 2 * GDN_QK + h * GDN_DV:2 * GDN_QK + (h + 1) * GDN_DV]
            beta = gt[:, idx:idx + 1]
            diff = gcs[:, gi:gi + 1] - grs[gi:gi + 1, :]
            gam.append(jnp.where(incl[d], jnp.exp(jnp.where(incl[d], diff, 0.0)), 0.0))
            q.append(qq)
            k.append(kk_)
            k16.append(kk_.astype(BF16))
            kb.append(kk_ * beta)
            vb.append(vv * beta)
            egc_c.append(egc[:, gi:gi + 1])
            ekd_c.append(ekd[:, gi:gi + 1])
            egl_c.append(egl[:, gi:gi + 1])
        n = len(chains)
        kk = [_dot_nt(kb[i].astype(BF16), k16[i]) for i in range(n)]
        qk = [_dot_nt(q[i].astype(BF16), k16[i]) for i in range(n)]
        m = [-jnp.where(strict[chains[i][0]], kk[i] * gam[i], 0.0) for i in range(n)]
        x = [eye + m[i] for i in range(n)]
        for _ in range(5):
            m = [_dot_split(m[i], m[i]) for i in range(n)]
            x = [x[i] + _dot_split(x[i], m[i]) for i in range(n)]
        rhs = [jnp.concatenate([vb[i], kb[i] * egc_c[i]], axis=1) for i in range(n)]
        sol = [_dot_split(x[i], rhs[i]) for i in range(n)]
        return dict(
            u=[sol[i][:, :GDN_DV] for i in range(n)],
            w=[sol[i][:, GDN_DV:].astype(BF16) for i in range(n)],
            aqk=[(qk[i] * gam[i]).astype(BF16) for i in range(n)],
            qd=[(q[i] * egc_c[i]).astype(BF16) for i in range(n)],
            kd=[(k[i] * ekd_c[i]).astype(BF16) for i in range(n)],
            egl=egl_c)

    def advance(c, pre):
        n = len(chains)
        s = [s_ref[i] for i in range(n)]
        s16 = [s[i].astype(BF16) for i in range(n)]
        ws = [_dot(pre["w"][i], s16[i]) for i in range(n)]
        qs = [_dot(pre["qd"][i], s16[i]) for i in range(n)]
        vn16 = [(pre["u"][i] - ws[i]).astype(BF16) for i in range(n)]
        av = [_dot(pre["aqk"][i], vn16[i]) for i in range(n)]
        kv = [_dot_tn(pre["kd"][i], vn16[i]) for i in range(n)]
        for i, (d, h) in enumerate(chains):
            r0 = row0(c, d)
            s_ref[i] = s[i] * pre["egl"][i] + kv[i]
            refs[d][3][r0:r0 + c64, h * GDN_DV:(h + 1) * GDN_DV] = qs[i] + av[i]

    pre = prepare(0)
    for c in range(n_chunks):
        nxt = prepare(c + 1) if c + 1 < n_chunks else None
        advance(c, pre)
        pre = nxt


def _gdn(qkvn, bg, bg_rows, row_off, batch, seqlen, ts, prev):
    t, c = qkvn.shape
    ns = seqlen // ts
    off = row_off // ts
    fwd = lambda b, s: (off + b * ns + s, 0)
    bwd = lambda b, s: (off + b * ns + ns - 1 - s, 0)
    fwd_r = lambda b, s: (0, off + b * ns + s)
    bwd_r = lambda b, s: (0, off + b * ns + ns - 1 - s)
    nr = bg_rows.shape[0]
    out = jax.ShapeDtypeStruct((t, GDN_V), F32)
    in_specs = [
        pl.BlockSpec((ts, c), fwd), pl.BlockSpec((ts, V7X_LANES), fwd),
        pl.BlockSpec((nr, ts), fwd_r),
        pl.BlockSpec((ts, c), bwd), pl.BlockSpec((ts, V7X_LANES), bwd),
        pl.BlockSpec((nr, ts), bwd_r),
    ]
    args = [qkvn, bg, bg_rows, qkvn, bg, bg_rows]
    aliases = {}
    if prev is not None:
        aliases = {len(args): 0, len(args) + 1: 1}
        in_specs += [pl.BlockSpec(memory_space=pl.ANY)] * 2
        args += list(prev)
    return pl.pallas_call(
        functools.partial(_gdn_body, n_chunks=ts // GDN_CHUNK),
        grid=(batch, ns),
        in_specs=in_specs,
        out_specs=[pl.BlockSpec((ts, GDN_V), fwd), pl.BlockSpec((ts, GDN_V), bwd)],
        out_shape=[out, out],
        scratch_shapes=[pltpu.VMEM((2 * GDN_HEADS, GDN_DK, GDN_DV), F32)],
        input_output_aliases=aliases,
        compiler_params=_cparams(("parallel", "arbitrary")),
        name="gdn",
    )(*args)


def _attn_body(far_ref, q_ref, k_ref, vt_ref, band_ref, lam_ref, nw_ref, *rest,
               lam_init, tk, nk):
    o_ref, qz_ref, s_ref, mx_ref, m_ref, l_ref, acc_ref = rest[-7:]
    h = pl.program_id(1)
    qi = pl.program_id(2)
    tq = q_ref.shape[0]
    sub = V7X_SUBLANES

    m_ref[...] = jnp.full_like(m_ref, -1e30)
    l_ref[...] = jnp.zeros_like(l_ref)
    acc_ref[...] = jnp.zeros_like(acc_ref)
    q = q_ref[...]
    lane = lax.broadcasted_iota(jnp.int32, q.shape, 1)
    zero = jnp.zeros_like(q)
    qz_ref[0] = jnp.where(lane < DIFF_DH, q, zero)
    qz_ref[1] = jnp.where(lane >= DIFF_DH, q, zero)

    def scores(kt, slot, bias_tile):
        k = k_ref[pl.ds(pl.multiple_of(kt * tk, tk), tk), :]
        for mp in range(2):
            s = _dot_nt(k, qz_ref[mp])
            if bias_tile is not None:
                s = s + bias_tile
            s_ref[slot, mp] = s
            mx_ref[slot, mp] = jnp.max(s.reshape(tk // sub, sub, tq), axis=0)

    def accumulate(kt, slot, c):
        vt = vt_ref[:, pl.ds(pl.multiple_of(kt * tk, tk), tk)]
        for mp in range(2):
            m_cur = jnp.max(mx_ref[slot, mp], axis=0, keepdims=True) + c
            m_prev = m_ref[mp]
            m_new = jnp.maximum(m_prev, m_cur)
            alpha = jnp.exp2(m_prev - m_new)
            m_ref[mp] = m_new
            p = jnp.exp2(s_ref[slot, mp] - (m_new - c))
            l_ref[mp] = alpha * l_ref[mp] + jnp.sum(p.reshape(tk // sub, sub, tq), axis=0)
            acc_ref[mp] = alpha * acc_ref[mp] + _dot(vt, p.astype(BF16))

    n_left = jnp.maximum(qi - 1, 0)
    right0 = jnp.minimum(qi + 2, nk)
    n_far = n_left + (nk - right0)

    def far_tile(f):
        return jnp.where(f < n_left, f, right0 + (f - n_left))

    @pl.when(n_far > 0)
    def _():
        scores(far_tile(0), 0, None)

    def far_step(f, slot):
        nxt = jnp.minimum(f + 1, n_far - 1)
        scores(far_tile(nxt), 1 - slot, None)
        accumulate(far_tile(f), slot, jnp.where(f < n_left, far_ref[h, 0], far_ref[h, 1]))

    def far_pair(g, carry):
        far_step(2 * g, 0)
        far_step(2 * g + 1, 1)
        return carry

    lax.fori_loop(0, n_far // 2, far_pair, 0)

    @pl.when(n_far % 2 == 1)
    def _():
        far_step(n_far - 1, 0)

    for dd in range(3):
        kt = qi - 1 + dd

        @pl.when((kt >= 0) & (kt < nk))
        def _():
            scores(kt, 0, band_ref[dd])
            accumulate(kt, 0, 0.0)

    lam = lam_ref[...]
    lam_full = (jnp.exp(jnp.sum(lam[0:1] * lam[1:2], keepdims=True))
                - jnp.exp(jnp.sum(lam[2:3] * lam[3:4], keepdims=True)) + lam_init)
    l0 = jnp.sum(l_ref[0], axis=0, keepdims=True)
    l1 = jnp.sum(l_ref[1], axis=0, keepdims=True)
    o = acc_ref[0] / l0 - lam_full * (acc_ref[1] / l1)
    o = o * lax.rsqrt(jnp.mean(o * o, axis=0, keepdims=True) + EPS)
    o = o * (nw_ref[...] * (1.0 - lam_init))
    o_ref[...] = o.T


def _attn(qb, kb, vbt, band, far, lam, nw_col, row_off, batch, seqlen, tq, lam_init, prev):
    t = qb.shape[0]
    nq = seqlen // tq
    off = row_off // tq
    off_seq = row_off // seqlen
    assert row_off % seqlen == 0
    qmap = lambda b, h, qi: (off + b * nq + qi, h)
    in_specs = [
        pl.BlockSpec(memory_space=pltpu.SMEM),
        pl.BlockSpec((tq, DIFF_DV), qmap),
        pl.BlockSpec((seqlen, DIFF_DV), lambda b, h, qi: (off_seq + b, h)),
        pl.BlockSpec((DIFF_DV, seqlen), lambda b, h, qi: (h, off_seq + b)),
        pl.BlockSpec((None, 3, tq, tq), lambda b, h, qi: (h, 0, 0, 0)),
        pl.BlockSpec(lam.shape, lambda b, h, qi: (0, 0)),
        pl.BlockSpec((DIFF_DV, 1), lambda b, h, qi: (0, 0)),
    ]
    args = [far, qb, kb, vbt, band, lam, nw_col]
    aliases = {}
    if prev is not None:
        aliases = {len(args): 0}
        in_specs.append(pl.BlockSpec(memory_space=pl.ANY))
        args.append(prev)
    return pl.pallas_call(
        functools.partial(_attn_body, lam_init=lam_init, tk=tq, nk=nq),
        grid=(batch, DIFF_HEADS, nq),
        in_specs=in_specs,
        input_output_aliases=aliases,
        out_specs=pl.BlockSpec((tq, DIFF_DV), qmap),
        out_shape=jax.ShapeDtypeStruct((t, DIFF_VW), F32),
        scratch_shapes=[pltpu.VMEM((2, tq, DIFF_DV), BF16),
                        pltpu.VMEM((2, 2, tq, tq), F32),
                        pltpu.VMEM((2, 2, V7X_SUBLANES, tq), F32),
                        pltpu.VMEM((2, 1, tq), F32),
                        pltpu.VMEM((2, V7X_SUBLANES, tq), F32),
                        pltpu.VMEM((2, DIFF_DV, tq), F32)],
        compiler_params=_cparams(("parallel", "parallel", "parallel")),
        name="diffattn",
    )(*args)


def _memkv_body(m_ref, nw_ref, w_ref, o_ref):
    h = _rms(m_ref[...], nw_ref[...]).astype(BF16)
    o_ref[...] = _dot(h, w_ref[...]).astype(BF16)


def _memkv(mem, nw, wkv):
    nb, nm, d = mem.shape
    return pl.pallas_call(
        _memkv_body,
        grid=(nb,),
        in_specs=[pl.BlockSpec((None, nm, d), lambda b: (b, 0, 0)), _resident((1, d)),
                  _resident(wkv.shape)],
        out_specs=pl.BlockSpec((None, nm, wkv.shape[1]), lambda b: (b, 0, 0)),
        out_shape=jax.ShapeDtypeStruct((nb, nm, wkv.shape[1]), BF16),
        compiler_params=_cparams(("parallel",)),
        name="memkv",
    )(mem, nw, wkv)


def _mergex_body(x_ref, of_ref, ob_ref, z_ref, oattn_ref, sg_ref, kv_ref, gn_ref, wua_ref,
                 wub_ref, wout_ref, xn_ref, wq_ref, wo_ref, o_ref):
    o = of_ref[...] + ob_ref[...]
    z = z_ref[...]
    gn = gn_ref[...]
    heads = []
    for h in range(GDN_HEADS):
        sl = slice(h * GDN_DV, (h + 1) * GDN_DV)
        zh = z[:, sl]
        heads.append(_rms(o[:, sl], gn) * (zh * jax.nn.sigmoid(zh)))
    oa = jnp.concatenate(heads, axis=1).astype(BF16)
    ya = _dot(oa, wua_ref[...])
    yb = _dot(oattn_ref[...].astype(BF16), wub_ref[...])
    sg = sg_ref[...]
    merged = sg[:, :D_MODEL] * ya + sg[:, D_MODEL:] * yb
    x = x_ref[...] + _dot(merged.astype(BF16), wout_ref[...])
    hq = _rms(x, xn_ref[...]).astype(BF16)
    q = _dot(hq, wq_ref[...]) * (X_DH ** -0.5)
    kv = kv_ref[...]
    outs = []
    for h in range(X_HEADS):
        sl = slice(h * X_DH, (h + 1) * X_DH)
        kh = kv[:, sl]
        vh = kv[:, X_HEADS * X_DH + h * X_DH:X_HEADS * X_DH + (h + 1) * X_DH]
        s = _dot_nt(q[:, sl].astype(BF16), kh)
        s = s - jnp.max(s, axis=-1, keepdims=True)
        p = jnp.exp(s)
        p = p / jnp.sum(p, axis=-1, keepdims=True)
        outs.append(_dot(p.astype(BF16), vh))
    ox = jnp.concatenate(outs, axis=1).astype(BF16)
    o_ref[...] = x + _dot(ox, wo_ref[...])


def _mergex(x, o_f, o_b, z, oattn, sg, kv, gn, wua, wub, wout, xn, wq, wo, tm, tile_batch):
    t, d = x.shape
    rows = lambda width: pl.BlockSpec((tm, width), lambda i: (i, 0))
    return pl.pallas_call(
        _mergex_body,
        grid=(t // tm,),
        in_specs=[rows(d), rows(GDN_V), rows(GDN_V), rows(GDN_V), rows(DIFF_VW), rows(2 * d),
                  pl.BlockSpec((None,) + kv.shape[1:], lambda i: (tile_batch(i), 0, 0)),
                  _resident(gn.shape), _resident(wua.shape), _resident(wub.shape),
                  _resident(wout.shape), _resident(xn.shape), _resident(wq.shape),
                  _resident(wo.shape)],
        out_specs=rows(d),
        out_shape=jax.ShapeDtypeStruct((t, d), F32),
        compiler_params=_cparams(("parallel",)),
        name="mergex",
    )(x, o_f, o_b, z, oattn, sg, kv, gn, wua, wub, wout, xn, wq, wo)


def _rel_bucket(rel):
    nb = N_BUCKETS // 2
    ret = jnp.where(rel > 0, nb, 0)
    n = jnp.abs(rel)
    max_exact = nb // 2
    nf = jnp.maximum(n, 1).astype(F32)
    large = max_exact + (jnp.log(nf / max_exact) / math.log(MAX_DISTANCE / max_exact)
                         * (nb - max_exact)).astype(jnp.int32)
    large = jnp.minimum(large, nb - 1)
    return ret + jnp.where(n < max_exact, n, large)


def _bias_tables(rel_bias, tq):
    assert tq >= MAX_DISTANCE
    rel = jnp.arange(-(2 * tq - 1), 2 * tq, dtype=jnp.int32)
    by_rel = rel_bias[_rel_bucket(rel)].astype(F32)
    i = jnp.arange(tq, dtype=jnp.int32)[:, None]
    j = jnp.arange(tq, dtype=jnp.int32)[None, :]
    idx = jnp.stack([(dd - 1) * tq + i - j + (2 * tq - 1) for dd in range(3)])
    by_rel = by_rel * LOG2E
    band = jnp.moveaxis(by_rel[idx], -1, 0)
    far = jnp.stack([by_rel[0], by_rel[-1]], axis=1)
    return band, far


def _pick(limit, n):
    tile = limit
    while n % tile:
        tile //= 2
    return tile


def _encode(xs, mems, p, tile_limits=None):
    lim = dict(ffn=512, proj=256, prep=256, gdn=512, attn=512, mergex=256)
    if tile_limits:
        lim.update(tile_limits)
    depth = p["w_in"].shape[0]
    d = D_MODEL
    groups = []
    row = 0
    bat = 0
    for x in xs:
        b, l, _ = x.shape
        groups.append(dict(b=b, l=l, row=row, bat=bat))
        row += b * l
        bat += b
    t_all = row
    seqlens = [g["l"] for g in groups]
    common = functools.reduce(math.gcd, seqlens)
    tiles = {k: _pick(v, common) for k, v in lim.items()}

    x = jnp.concatenate([xx.reshape(-1, d) for xx in xs], axis=0)
    mem = jnp.concatenate(mems, axis=0)

    def tile_batch(i):
        r0 = i * tiles["mergex"]
        bidx = 0
        for g in groups:
            bidx = jnp.where(r0 >= g["row"], g["bat"] + (r0 - g["row"]) // g["l"], bidx)
        return bidx

    seq_tiles = tuple((g["row"] // tiles["prep"], g["l"] // tiles["prep"]) for g in groups)
    band, far = _bias_tables(p["rel_bias"], tiles["attn"])
    nf = D_FF // FFN_CHUNK

    def ffn_weights(prefix, i):
        wg = p[prefix + "_w_gate"][i].reshape(d, nf, FFN_CHUNK).transpose(1, 0, 2).astype(BF16)
        wu = p[prefix + "_w_up"][i].reshape(d, nf, FFN_CHUNK).transpose(1, 0, 2).astype(BF16)
        wd = p[prefix + "_w_down"][i].reshape(nf, FFN_CHUNK, d).astype(BF16)
        return p[prefix + "_norm"][i].reshape(1, d), wg, wu, wd

    for i in range(depth):
        x = _ffn(x, *ffn_weights("ffn1", i), None, tiles["ffn"])

        w_in = p["w_in"][i]
        o_beta = 3 * GDN_QK
        o_z = o_beta + 2 * N_GATES
        o_qb = o_z + GDN_V
        gate_w = jnp.pad(w_in[:, o_beta:o_z], ((0, 0), (0, V7X_LANES - 2 * N_GATES)))
        w_perm = jnp.concatenate(
            [w_in[:, :o_beta], gate_w, w_in[:, o_z:o_qb],
             w_in[:, o_qb:o_qb + DIFF_QK] * (DIFF_DH ** -0.5 * LOG2E),
             w_in[:, o_qb + DIFF_QK:]],
            axis=1).astype(BF16)
        assert w_perm.shape[1] == _P_COLS
        qkva, gates, z, qb, kb, vb, sg = _proj(x, p["mix_norm"][i].reshape(1, d), w_perm,
                                               tiles["proj"])

        conv_w8 = jnp.pad(p["conv_w"][i], ((0, HALO - GDN_CONV), (0, 0)))
        lane_pad = (N_GATES, V7X_LANES - 2 * N_GATES)
        alog_pad = jnp.pad(p["gdn_a_log"][i].reshape(-1), lane_pad).reshape(1, V7X_LANES)
        dtb_pad = jnp.pad(p["gdn_dt_bias"][i].reshape(-1), lane_pad).reshape(1, V7X_LANES)
        qkvn, bg = _gdnprep(qkva, conv_w8, gates, alog_pad, dtb_pad, tiles["prep"], seq_tiles)
        bg_rows = bg[:, :2 * N_GATES].T

        lam_init = 0.8 - 0.6 * math.exp(-0.3 * i)
        o_fb = o_attn = None
        vbt = vb.T
        for g in groups:
            o_fb = _gdn(qkvn, bg, bg_rows, g["row"], g["b"], g["l"], tiles["gdn"], o_fb)
            o_attn = _attn(qb, kb, vbt, band, far, p["diff_lambda"][i],
                           p["diff_norm"][i].reshape(DIFF_DV, 1), g["row"], g["b"], g["l"],
                           tiles["attn"], lam_init, o_attn)
        o_f, o_b = o_fb

        kv = _memkv(mem, p["mem_norm"][i].reshape(1, d), p["xattn_wkv"][i].astype(BF16))
        x = _mergex(x, o_f, o_b, z, o_attn, sg, kv, p["gdn_norm"][i].reshape(1, GDN_DV),
                    p["w_up_a"][i].astype(BF16), p["w_up_b"][i].astype(BF16),
                    p["w_out"][i].astype(BF16), p["xattn_norm"][i].reshape(1, d),
                    p["xattn_wq"][i].astype(BF16), p["xattn_wo"][i].astype(BF16),
                    tiles["mergex"], tile_batch)

        final_w = p["final_norm"].reshape(1, d) if i == depth - 1 else None
        x = _ffn(x, *ffn_weights("ffn2", i), final_w, tiles["ffn"])

    return tuple(x[g["row"]:g["row"] + g["b"] * g["l"]].reshape(g["b"], g["l"], d)
                 for g in groups)


def kernel(x_prompt, x_sample, mem_prompt, mem_sample, ffn1_norm, ffn1_w_gate, ffn1_w_up, ffn1_w_down, mix_norm, w_in, conv_w, gdn_a_log, gdn_dt_bias, gdn_norm, w_up_a, diff_lambda, diff_norm, w_up_b, w_out, rel_bias, xattn_norm, mem_norm, xattn_wq, xattn_wkv, xattn_wo, ffn2_norm, ffn2_w_gate, ffn2_w_up, ffn2_w_down, final_norm):
    params = dict(
        ffn1_norm=ffn1_norm, ffn1_w_gate=ffn1_w_gate, ffn1_w_up=ffn1_w_up,
        ffn1_w_down=ffn1_w_down, mix_norm=mix_norm, w_in=w_in, conv_w=conv_w,
        gdn_a_log=gdn_a_log, gdn_dt_bias=gdn_dt_bias, gdn_norm=gdn_norm, w_up_a=w_up_a,
        diff_lambda=diff_lambda, diff_norm=diff_norm, w_up_b=w_up_b, w_out=w_out,
        rel_bias=rel_bias, xattn_norm=xattn_norm, mem_norm=mem_norm, xattn_wq=xattn_wq,
        xattn_wkv=xattn_wkv, xattn_wo=xattn_wo, ffn2_norm=ffn2_norm,
        ffn2_w_gate=ffn2_w_gate, ffn2_w_up=ffn2_w_up, ffn2_w_down=ffn2_w_down,
        final_norm=final_norm)
    y_prompt, y_sample = _encode((x_prompt, x_sample), (mem_prompt, mem_sample), params)
    return (y_prompt, y_sample)
```

```python
import functools
import math

import jax
import jax.numpy as jnp
from jax import lax
from jax.experimental import pallas as pl
from jax.experimental.pallas import tpu as pltpu

F32 = jnp.float32
BF16 = jnp.bfloat16
HIGHEST = lax.Precision.HIGHEST

EPS = 1e-6
LOG2E = math.log2(math.e)
D_MODEL = 1024
N_MEM = 256
GDN_HEADS = 4
GDN_DK = 128
GDN_DV = 128
GDN_CONV = 5
GDN_CHUNK = 64
DIFF_HEADS = 8
DIFF_DH = 64
DIFF_DV = 2 * DIFF_DH
N_BUCKETS = 32
MAX_DISTANCE = 128
X_HEADS = 4
X_DH = 128
D_FF = 2816
GDN_QK = GDN_HEADS * GDN_DK
GDN_V = GDN_HEADS * GDN_DV
DIFF_QK = DIFF_HEADS * 2 * DIFF_DH
DIFF_VW = DIFF_HEADS * DIFF_DV
N_GATES = 2 * GDN_HEADS
VT_ROWS = DIFF_DV + 16

V7X_LANES = 128
V7X_SUBLANES = 8
V7X_VMEM_LIMIT = 56 * 1024 * 1024

FFN_CHUNK = 256
HALO = V7X_SUBLANES


def _cparams(sem):
    return pltpu.CompilerParams(dimension_semantics=sem, vmem_limit_bytes=V7X_VMEM_LIMIT)


def _dot(a, b, precision=None):
    return jnp.dot(a, b, preferred_element_type=F32, precision=precision)


def _dot_nt(a, b, precision=None):
    return lax.dot_general(a, b, (((1,), (1,)), ((), ())),
                           preferred_element_type=F32, precision=precision)


def _dot_tn(a, b, precision=None):
    return lax.dot_general(a, b, (((0,), (0,)), ((), ())),
                           preferred_element_type=F32, precision=precision)


def _dot_split(a, b):
    a_hi = a.astype(BF16)
    b_hi = b.astype(BF16)
    a_lo = (a - a_hi.astype(F32)).astype(BF16)
    b_lo = (b - b_hi.astype(F32)).astype(BF16)
    return _dot(a_hi, b_hi) + (_dot(a_hi, b_lo) + _dot(a_lo, b_hi))


def _rms(x, w):
    return x * lax.rsqrt(jnp.mean(x * x, axis=-1, keepdims=True) + EPS) * w


def _resident(shape):
    nd = len(shape)
    return pl.BlockSpec(shape, lambda *_: (0,) * nd, pipeline_mode=pl.Buffered(1))


def _ffn_body(x_ref, nw_ref, wg_ref, wu_ref, wd_ref, *rest, n_chunks, final):
    if final:
        fn_ref, o_ref, h_ref, acc_ref = rest
    else:
        o_ref, h_ref, acc_ref = rest
    x = x_ref[...]
    h_ref[...] = _rms(x, nw_ref[...]).astype(BF16)
    acc_ref[...] = jnp.zeros_like(acc_ref)

    def chunk(c, carry):
        h = h_ref[...]
        g = _dot(h, wg_ref[c])
        u = _dot(h, wu_ref[c])
        a = (g * jax.nn.sigmoid(g) * u).astype(BF16)
        acc_ref[...] += _dot(a, wd_ref[c])
        return carry

    lax.fori_loop(0, n_chunks, chunk, 0)
    y = x + 0.5 * acc_ref[...]
    if final:
        y = _rms(y, fn_ref[...])
    o_ref[...] = y


def _ffn(x, nw, wg, wu, wd, final_w, tm):
    t, d = x.shape
    nf = wg.shape[0]
    row = pl.BlockSpec((tm, d), lambda i: (i, 0))
    in_specs = [row, _resident((1, d)), _resident(wg.shape), _resident(wu.shape),
                _resident(wd.shape)]
    args = [x, nw, wg, wu, wd]
    if final_w is not None:
        in_specs.append(_resident((1, d)))
        args.append(final_w)
    return pl.pallas_call(
        functools.partial(_ffn_body, n_chunks=nf, final=final_w is not None),
        grid=(t // tm,),
        in_specs=in_specs,
        out_specs=row,
        out_shape=jax.ShapeDtypeStruct((t, d), F32),
        scratch_shapes=[pltpu.VMEM((tm, d), BF16), pltpu.VMEM((tm, d), F32)],
        compiler_params=_cparams(("parallel",)),
        name="ffn",
    )(*args)


_P_QKVA = (0, 3 * GDN_QK)
_P_GATE = (_P_QKVA[1], _P_QKVA[1] + V7X_LANES)
_P_Z = (_P_GATE[1], _P_GATE[1] + GDN_V)
_P_QB = (_P_Z[1], _P_Z[1] + DIFF_QK)
_P_KB = (_P_QB[1], _P_QB[1] + DIFF_QK)
_P_VB = (_P_KB[1], _P_KB[1] + DIFF_VW)
_P_SG = (_P_VB[1], _P_VB[1] + 2 * D_MODEL)
_P_COLS = _P_SG[1]


def _proj_body(x_ref, nw_ref, w_ref, qkva_ref, gate_ref, z_ref, qb_ref, kb_ref, vb_ref, sg_ref):
    u = _rms(x_ref[...], nw_ref[...]).astype(BF16)

    def seg(span):
        return _dot(u, w_ref[:, span[0]:span[1]])

    qkva_ref[...] = seg(_P_QKVA)
    gate_ref[...] = seg(_P_GATE)
    z_ref[...] = seg(_P_Z)
    qb_ref[...] = seg(_P_QB).astype(BF16)
    kb_ref[...] = seg(_P_KB).astype(BF16)
    vb_ref[...] = seg(_P_VB).astype(BF16)
    sg_ref[...] = jax.nn.sigmoid(seg(_P_SG))


def _proj(x, nw, w, tm):
    t, d = x.shape
    widths = [(_P_QKVA, F32), (_P_GATE, F32), (_P_Z, F32), (_P_QB, BF16), (_P_KB, BF16),
              (_P_VB, BF16), (_P_SG, F32)]
    out_shape = [jax.ShapeDtypeStruct((t, s[1] - s[0]), dt) for s, dt in widths]
    out_specs = [pl.BlockSpec((tm, s[1] - s[0]), lambda i: (i, 0)) for s, _ in widths]
    return pl.pallas_call(
        _proj_body,
        grid=(t // tm,),
        in_specs=[pl.BlockSpec((tm, d), lambda i: (i, 0)), _resident((1, d)), _resident(w.shape)],
        out_specs=out_specs,
        out_shape=out_shape,
        compiler_params=_cparams(("parallel",)),
        name="proj",
    )(x, nw, w)


def _gdnprep_body(cur_ref, prev_ref, next_ref, cw_ref, gate_ref, alog_ref, dtb_ref,
                  qkv_ref, bg_ref, *, tm, seq_tiles):
    i = pl.program_id(0)
    first = jnp.bool_(False)
    last = jnp.bool_(False)
    for start, per_seq in seq_tiles:
        rel = i - start
        first = first | ((rel >= 0) & (rel % per_seq == 0))
        last = last | ((rel >= 0) & (rel % per_seq == per_seq - 1))
    prev = jnp.where(first, 0.0, prev_ref[...])
    nxt = jnp.where(last, 0.0, next_ref[...])
    ext = jnp.concatenate([prev, cur_ref[...], nxt], axis=0)
    pad = (GDN_CONV - 1) // 2
    acc = None
    for k in range(GDN_CONV):
        lo = HALO - pad + k
        term = ext[lo:lo + tm, :] * cw_ref[k:k + 1, :]
        acc = term if acc is None else acc + term
    y = acc * jax.nn.sigmoid(acc)
    for h in range(2 * GDN_HEADS):
        lo = h * GDN_DK
        yh = y[:, lo:lo + GDN_DK]
        nrm = yh * lax.rsqrt(jnp.sum(yh * yh, axis=-1, keepdims=True) + EPS)
        if h < GDN_HEADS:
            nrm = nrm * (GDN_DK ** -0.5)
        qkv_ref[:, lo:lo + GDN_DK] = nrm
    qkv_ref[:, 2 * GDN_QK:] = y[:, 2 * GDN_QK:]
    gl = gate_ref[...]
    beta = jax.nn.sigmoid(gl)
    xa = gl + dtb_ref[...]
    softplus = jnp.maximum(xa, 0.0) + jnp.log1p(jnp.exp(-jnp.abs(xa)))
    g = -jnp.exp(alog_ref[...]) * softplus
    lane = lax.broadcasted_iota(jnp.int32, gl.shape, 1)
    bg_ref[...] = jnp.where(lane < N_GATES, beta, g)


def _gdnprep(qkva, conv_w8, gates, alog_pad, dtb_pad, tm, seq_tiles):
    t, c = qkva.shape
    hb = tm // HALO
    nblk = t // HALO
    return pl.pallas_call(
        functools.partial(_gdnprep_body, tm=tm, seq_tiles=seq_tiles),
        grid=(t // tm,),
        in_specs=[
            pl.BlockSpec((tm, c), lambda i: (i, 0)),
            pl.BlockSpec((HALO, c), lambda i: (jnp.maximum(i * hb - 1, 0), 0)),
            pl.BlockSpec((HALO, c), lambda i: (jnp.minimum((i + 1) * hb, nblk - 1), 0)),
            _resident(conv_w8.shape),
            pl.BlockSpec((tm, V7X_LANES), lambda i: (i, 0)),
            _resident((1, V7X_LANES)),
            _resident((1, V7X_LANES)),
        ],
        out_specs=[pl.BlockSpec((tm, c), lambda i: (i, 0)),
                   pl.BlockSpec((tm, V7X_LANES), lambda i: (i, 0))],
        out_shape=[jax.ShapeDtypeStruct((t, c), F32),
                   jax.ShapeDtypeStruct((t, V7X_LANES), F32)],
        compiler_params=_cparams(("parallel",)),
        name="gdnprep",
    )(qkva, qkva, qkva, conv_w8, gates, alog_pad, dtb_pad)


def _gdn_body(qkv_f, bgc_f, bgr_f, qkv_b, bgc_b, bgr_b, *rest, n_chunks):
    of_ref, ob_ref, s_ref = rest[-3:]
    c64 = GDN_CHUNK

    @pl.when(pl.program_id(1) == 0)
    def _():
        s_ref[...] = jnp.zeros_like(s_ref)

    row = lax.broadcasted_iota(jnp.int32, (c64, c64), 0)
    col = lax.broadcasted_iota(jnp.int32, (c64, c64), 1)
    eye = (row == col).astype(F32)
    incl = (row >= col, row <= col)
    strict = (row > col, row < col)

    refs = ((qkv_f, bgc_f, bgr_f, of_ref), (qkv_b, bgc_b, bgr_b, ob_ref))
    chains = [(d, h) for d in range(2) for h in range(GDN_HEADS)]

    def row0(c, d):
        return (c if d == 0 else n_chunks - 1 - c) * c64

    def prepare(c):
        gates = []
        for d in range(2):
            _, bgc_ref, bgr_ref, _ = refs[d]
            r0 = row0(c, d)
            gt = bgc_ref[r0:r0 + c64, :]
            gcs = _dot(incl[d].astype(F32), gt, HIGHEST)
            grs = _dot(bgr_ref[:, r0:r0 + c64], incl[1 - d].astype(F32), HIGHEST)
            g_last = gcs[c64 - 1:c64, :] if d == 0 else gcs[0:1, :]
            gates.append((gt, gcs, grs, jnp.exp(gcs), jnp.exp(g_last - gcs), jnp.exp(g_last)))
        q, k, k16, kb, vb, gam, egc_c, ekd_c, egl_c = ([] for _ in range(9))
        for d, h in chains:
            qkv_ref = refs[d][0]
            r0 = row0(c, d)
            gt, gcs, grs, egc, ekd, egl = gates[d]
            idx = d * GDN_HEADS + h
            gi = N_GATES + idx
            qq = qkv_ref[r0:r0 + c64, h * GDN_DK:(h + 1) * GDN_DK]
            kk_ = qkv_ref[r0:r0 + c64, GDN_QK + h * GDN_DK:GDN_QK + (h + 1) * GDN_DK]
            vv = qkv_ref[r0:r0 + c64, 2 * GDN_QK + h * GDN_DV:2 * GDN_QK + (h + 1) * GDN_DV]
            beta = gt[:, idx:idx + 1]
            diff = gcs[:, gi:gi + 1] - grs[gi:gi + 1, :]
            gam.append(jnp.where(incl[d], jnp.exp(jnp.where(incl[d], diff, 0.0)), 0.0))
            q.append(qq)
            k.append(kk_)
            k16.append(kk_.astype(BF16))
            kb.append(kk_ * beta)
            vb.append(vv * beta)
            egc_c.append(egc[:, gi:gi + 1])
            ekd_c.append(ekd[:, gi:gi + 1])
            egl_c.append(egl[:, gi:gi + 1])
        n = len(chains)
        kk = [_dot_nt(kb[i].astype(BF16), k16[i]) for i in range(n)]
        qk = [_dot_nt(q[i].astype(BF16), k16[i]) for i in range(n)]
        m = [-jnp.where(strict[chains[i][0]], kk[i] * gam[i], 0.0) for i in range(n)]
        x = [eye + m[i] for i in range(n)]
        for _ in range(5):
            m = [_dot_split(m[i], m[i]) for i in range(n)]
            x = [x[i] + _dot_split(x[i], m[i]) for i in range(n)]
        rhs = [jnp.concatenate([vb[i], kb[i] * egc_c[i]], axis=1) for i in range(n)]
        sol = [_dot_split(x[i], rhs[i]) for i in range(n)]
        return dict(
            u=[sol[i][:, :GDN_DV] for i in range(n)],
            w=[sol[i][:, GDN_DV:].astype(BF16) for i in range(n)],
            aqk=[(qk[i] * gam[i]).astype(BF16) for i in range(n)],
            qd=[(q[i] * egc_c[i]).astype(BF16) for i in range(n)],
            kd=[(k[i] * ekd_c[i]).astype(BF16) for i in range(n)],
            egl=egl_c)

    def advance(c, pre):
        n = len(chains)
        s = [s_ref[i] for i in range(n)]
        s16 = [s[i].astype(BF16) for i in range(n)]
        ws = [_dot(pre["w"][i], s16[i]) for i in range(n)]
        qs = [_dot(pre["qd"][i], s16[i]) for i in range(n)]
        vn16 = [(pre["u"][i] - ws[i]).astype(BF16) for i in range(n)]
        av = [_dot(pre["aqk"][i], vn16[i]) for i in range(n)]
        kv = [_dot_tn(pre["kd"][i], vn16[i]) for i in range(n)]
        for i, (d, h) in enumerate(chains):
            r0 = row0(c, d)
            s_ref[i] = s[i] * pre["egl"][i] + kv[i]
            refs[d][3][r0:r0 + c64, h * GDN_DV:(h + 1) * GDN_DV] = qs[i] + av[i]

    pre = prepare(0)
    for c in range(n_chunks):
        nxt = prepare(c + 1) if c + 1 < n_chunks else None
        advance(c, pre)
        pre = nxt


def _gdn(qkvn, bg, bg_rows, row_off, batch, seqlen, ts, prev):
    t, c = qkvn.shape
    ns = seqlen // ts
    off = row_off // ts
    fwd = lambda b, s: (off + b * ns + s, 0)
    bwd = lambda b, s: (off + b * ns + ns - 1 - s, 0)
    fwd_r = lambda b, s: (0, off + b * ns + s)
    bwd_r = lambda b, s: (0, off + b * ns + ns - 1 - s)
    nr = bg_rows.shape[0]
    out = jax.ShapeDtypeStruct((t, GDN_V), F32)
    in_specs = [
        pl.BlockSpec((ts, c), fwd), pl.BlockSpec((ts, V7X_LANES), fwd),
        pl.BlockSpec((nr, ts), fwd_r),
        pl.BlockSpec((ts, c), bwd), pl.BlockSpec((ts, V7X_LANES), bwd),
        pl.BlockSpec((nr, ts), bwd_r),
    ]
    args = [qkvn, bg, bg_rows, qkvn, bg, bg_rows]
    aliases = {}
    if prev is not None:
        aliases = {len(args): 0, len(args) + 1: 1}
        in_specs += [pl.BlockSpec(memory_space=pl.ANY)] * 2
        args += list(prev)
    return pl.pallas_call(
        functools.partial(_gdn_body, n_chunks=ts // GDN_CHUNK),
        grid=(batch, ns),
        in_specs=in_specs,
        out_specs=[pl.BlockSpec((ts, GDN_V), fwd), pl.BlockSpec((ts, GDN_V), bwd)],
        out_shape=[out, out],
        scratch_shapes=[pltpu.VMEM((2 * GDN_HEADS, GDN_DK, GDN_DV), F32)],
        input_output_aliases=aliases,
        compiler_params=_cparams(("parallel", "arbitrary")),
        name="gdn",
    )(*args)


def _attn_body(far_ref, q_ref, k_ref, vt_ref, band_ref, lam_ref, nw_ref, *rest,
               lam_init, tk, nk):
    o_ref, qz_ref, sf_ref, mxf_ref, sb_ref, mxb_ref, m_ref, acc_ref = rest[-8:]
    far_buf = (sf_ref, mxf_ref)
    band_buf = (sb_ref, mxb_ref)
    h = pl.program_id(1)
    qi = pl.program_id(2)
    tq = q_ref.shape[0]
    sub = V7X_SUBLANES

    m_ref[...] = jnp.full_like(m_ref, -1e30)
    acc_ref[...] = jnp.zeros_like(acc_ref)
    q = q_ref[...]
    lane = lax.broadcasted_iota(jnp.int32, q.shape, 1)
    zero = jnp.zeros_like(q)
    qz_ref[0] = jnp.where(lane < DIFF_DH, q, zero)
    qz_ref[1] = jnp.where(lane >= DIFF_DH, q, zero)

    def scores(kt, buf, slot, bias_tile):
        s_ref, mx_ref = buf
        k = k_ref[pl.ds(pl.multiple_of(kt * tk, tk), tk), :]
        for mp in range(2):
            s = _dot_nt(k, qz_ref[mp])
            if bias_tile is not None:
                s = s + bias_tile
            s_ref[slot, mp] = s
            mx_ref[slot, mp] = jnp.max(s.reshape(tk // sub, sub, tq), axis=0)

    def accumulate(kt, buf, slot, c):
        s_ref, mx_ref = buf
        vt = vt_ref[:, pl.ds(pl.multiple_of(kt * tk, tk), tk)]
        for mp in range(2):
            m_cur = jnp.max(mx_ref[slot, mp], axis=0, keepdims=True) + c
            m_prev = m_ref[mp]
            m_new = jnp.maximum(m_prev, m_cur)
            alpha = jnp.exp2(m_prev - m_new)
            m_ref[mp] = m_new
            p = jnp.exp2(s_ref[slot, mp] - (m_new - c)).astype(BF16)
            acc_ref[mp] = alpha * acc_ref[mp] + _dot(vt, p)

    n_left = jnp.maximum(qi - 1, 0)
    right0 = jnp.minimum(qi + 2, nk)
    n_far = n_left + (nk - right0)

    def far_tile(f):
        return jnp.where(f < n_left, f, right0 + (f - n_left))

    def far_const(f):
        return jnp.where(f < n_left, far_ref[h, 0], far_ref[h, 1])

    band = [(qi, 1), (qi - 1, 0), (qi + 1, 2)]

    def band_scores(j):
        kt, dd = band[j]
        scores(jnp.clip(kt, 0, nk - 1), band_buf, j, band_ref[dd])

    def band_accumulate(j):
        kt, _ = band[j]
        valid = (kt >= 0) & (kt < nk)
        accumulate(jnp.clip(kt, 0, nk - 1), band_buf, j, jnp.where(valid, 0.0, -1e30))

    if nk >= 4:
        scores(far_tile(0), far_buf, 0, None)

        def far_step(f, slot):
            scores(far_tile(f + 1), far_buf, 1 - slot, None)
            accumulate(far_tile(f), far_buf, slot, far_const(f))

        n_steps = n_far - 1

        def far_quad(g, carry):
            for u in range(4):
                far_step(4 * g + u, u % 2)
            return carry

        lax.fori_loop(0, n_steps // 4, far_quad, 0)
        done = (n_steps // 4) * 4

        @pl.when(n_steps % 4 >= 2)
        def _():
            far_step(done, 0)
            far_step(done + 1, 1)

        @pl.when(n_steps % 2 == 1)
        def _():
            far_step(n_steps - 1, 0)

        band_scores(0)
        accumulate(far_tile(n_far - 1), far_buf, (n_far - 1) % 2, far_const(n_far - 1))
        band_scores(1)
        band_accumulate(0)
        band_scores(2)
        band_accumulate(1)
        band_accumulate(2)
    else:
        for j in range(3):
            band_scores(j)
            band_accumulate(j)

    lam = lam_ref[...]
    lam_full = (jnp.exp(jnp.sum(lam[0:1] * lam[1:2], keepdims=True))
                - jnp.exp(jnp.sum(lam[2:3] * lam[3:4], keepdims=True)) + lam_init)
    a0 = acc_ref[0]
    a1 = acc_ref[1]
    o = (a0[:DIFF_DV] / a0[DIFF_DV:DIFF_DV + 1]
         - lam_full * (a1[:DIFF_DV] / a1[DIFF_DV:DIFF_DV + 1]))
    o = o * lax.rsqrt(jnp.mean(o * o, axis=0, keepdims=True) + EPS)
    o = o * (nw_ref[...] * (1.0 - lam_init))
    o_ref[...] = o.T


def _attn(qb, kb, vbt, band, far, lam, nw_col, row_off, batch, seqlen, tq, lam_init, prev):
    t = qb.shape[0]
    nq = seqlen // tq
    off = row_off // tq
    off_seq = row_off // seqlen
    assert row_off % seqlen == 0
    qmap = lambda b, h, qi: (off + b * nq + qi, h)
    in_specs = [
        pl.BlockSpec(memory_space=pltpu.SMEM),
        pl.BlockSpec((tq, DIFF_DV), qmap),
        pl.BlockSpec((seqlen, DIFF_DV), lambda b, h, qi: (off_seq + b, h)),
        pl.BlockSpec((VT_ROWS, seqlen), lambda b, h, qi: (h, off_seq + b)),
        pl.BlockSpec((None, 3, tq, tq), lambda b, h, qi: (h, 0, 0, 0)),
        pl.BlockSpec(lam.shape, lambda b, h, qi: (0, 0)),
        pl.BlockSpec((DIFF_DV, 1), lambda b, h, qi: (0, 0)),
    ]
    args = [far, qb, kb, vbt, band, lam, nw_col]
    aliases = {}
    if prev is not None:
        aliases = {len(args): 0}
        in_specs.append(pl.BlockSpec(memory_space=pl.ANY))
        args.append(prev)
    return pl.pallas_call(
        functools.partial(_attn_body, lam_init=lam_init, tk=tq, nk=nq),
        grid=(batch, DIFF_HEADS, nq),
        in_specs=in_specs,
        input_output_aliases=aliases,
        out_specs=pl.BlockSpec((tq, DIFF_DV), qmap),
        out_shape=jax.ShapeDtypeStruct((t, DIFF_VW), F32),
        scratch_shapes=[pltpu.VMEM((2, tq, DIFF_DV), BF16),
                        pltpu.VMEM((2, 2, tq, tq), F32),
                        pltpu.VMEM((2, 2, V7X_SUBLANES, tq), F32),
                        pltpu.VMEM((3, 2, tq, tq), F32),
                        pltpu.VMEM((3, 2, V7X_SUBLANES, tq), F32),
                        pltpu.VMEM((2, 1, tq), F32),
                        pltpu.VMEM((2, VT_ROWS, tq), F32)],
        compiler_params=_cparams(("parallel", "parallel", "parallel")),
        name="diffattn",
    )(*args)


def _memkv_body(m_ref, nw_ref, w_ref, o_ref):
    h = _rms(m_ref[...], nw_ref[...]).astype(BF16)
    o_ref[...] = _dot(h, w_ref[...]).astype(BF16)


def _memkv(mem, nw, wkv):
    nb, nm, d = mem.shape
    return pl.pallas_call(
        _memkv_body,
        grid=(nb,),
        in_specs=[pl.BlockSpec((None, nm, d), lambda b: (b, 0, 0)), _resident((1, d)),
                  _resident(wkv.shape)],
        out_specs=pl.BlockSpec((None, nm, wkv.shape[1]), lambda b: (b, 0, 0)),
        out_shape=jax.ShapeDtypeStruct((nb, nm, wkv.shape[1]), BF16),
        compiler_params=_cparams(("parallel",)),
        name="memkv",
    )(mem, nw, wkv)


def _mergex_body(x_ref, of_ref, ob_ref, z_ref, oattn_ref, sg_ref, kv_ref, gn_ref, wua_ref,
                 wub_ref, wout_ref, xn_ref, wq_ref, wo_ref, o_ref):
    o = of_ref[...] + ob_ref[...]
    z = z_ref[...]
    gn = gn_ref[...]
    heads = []
    for h in range(GDN_HEADS):
        sl = slice(h * GDN_DV, (h + 1) * GDN_DV)
        zh = z[:, sl]
        heads.append(_rms(o[:, sl], gn) * (zh * jax.nn.sigmoid(zh)))
    oa = jnp.concatenate(heads, axis=1).astype(BF16)
    ya = _dot(oa, wua_ref[...])
    yb = _dot(oattn_ref[...].astype(BF16), wub_ref[...])
    sg = sg_ref[...]
    merged = sg[:, :D_MODEL] * ya + sg[:, D_MODEL:] * yb
    x = x_ref[...] + _dot(merged.astype(BF16), wout_ref[...])
    hq = _rms(x, xn_ref[...]).astype(BF16)
    q = _dot(hq, wq_ref[...]) * (X_DH ** -0.5)
    kv = kv_ref[...]
    outs = []
    for h in range(X_HEADS):
        sl = slice(h * X_DH, (h + 1) * X_DH)
        kh = kv[:, sl]
        vh = kv[:, X_HEADS * X_DH + h * X_DH:X_HEADS * X_DH + (h + 1) * X_DH]
        s = _dot_nt(q[:, sl].astype(BF16), kh)
        s = s - jnp.max(s, axis=-1, keepdims=True)
        p = jnp.exp(s)
        p = p / jnp.sum(p, axis=-1, keepdims=True)
        outs.append(_dot(p.astype(BF16), vh))
    ox = jnp.concatenate(outs, axis=1).astype(BF16)
    o_ref[...] = x + _dot(ox, wo_ref[...])


def _mergex(x, o_f, o_b, z, oattn, sg, kv, gn, wua, wub, wout, xn, wq, wo, tm, tile_batch):
    t, d = x.shape
    rows = lambda width: pl.BlockSpec((tm, width), lambda i: (i, 0))
    return pl.pallas_call(
        _mergex_body,
        grid=(t // tm,),
        in_specs=[rows(d), rows(GDN_V), rows(GDN_V), rows(GDN_V), rows(DIFF_VW), rows(2 * d),
                  pl.BlockSpec((None,) + kv.shape[1:], lambda i: (tile_batch(i), 0, 0)),
                  _resident(gn.shape), _resident(wua.shape), _resident(wub.shape),
                  _resident(wout.shape), _resident(xn.shape), _resident(wq.shape),
                  _resident(wo.shape)],
        out_specs=rows(d),
        out_shape=jax.ShapeDtypeStruct((t, d), F32),
        compiler_params=_cparams(("parallel",)),
        name="mergex",
    )(x, o_f, o_b, z, oattn, sg, kv, gn, wua, wub, wout, xn, wq, wo)


def _rel_bucket(rel):
    nb = N_BUCKETS // 2
    ret = jnp.where(rel > 0, nb, 0)
    n = jnp.abs(rel)
    max_exact = nb // 2
    nf = jnp.maximum(n, 1).astype(F32)
    large = max_exact + (jnp.log(nf / max_exact) / math.log(MAX_DISTANCE / max_exact)
                         * (nb - max_exact)).astype(jnp.int32)
    large = jnp.minimum(large, nb - 1)
    return ret + jnp.where(n < max_exact, n, large)


def _toeplitz(v, rows, cols):
    n = rows + cols
    x = jnp.concatenate([v[..., :cols][..., ::-1], jnp.zeros_like(v[..., :1]),
                         v[..., cols:][..., ::-1]], axis=-1)
    lead = v.shape[:-1]
    tiled = jnp.broadcast_to(x[..., None, :], lead + (rows, n)).reshape(lead + (rows * n,))
    skew = tiled[..., :rows * (n - 1)].reshape(lead + (rows, n - 1))
    return skew[..., :cols]


def _bias_tables(rel_bias, tq):
    assert tq >= MAX_DISTANCE
    rel = jnp.arange(-(2 * tq - 1), 2 * tq, dtype=jnp.int32)
    by_rel = (rel_bias[_rel_bucket(rel)].astype(F32) * LOG2E).T
    nh = by_rel.shape[0]
    band = _toeplitz(by_rel, 3 * tq, tq).reshape(nh, 3, tq, tq)
    far = jnp.stack([by_rel[:, 0], by_rel[:, -1]], axis=1)
    return band, far


def _pick(limit, n):
    tile = limit
    while n % tile:
        tile //= 2
    return tile


def _encode(xs, mems, p, tile_limits=None):
    lim = dict(ffn=512, proj=256, prep=256, gdn=512, attn=512, mergex=256)
    if tile_limits:
        lim.update(tile_limits)
    depth = p["w_in"].shape[0]
    d = D_MODEL
    groups = []
    row = 0
    bat = 0
    for x in xs:
        b, l, _ = x.shape
        groups.append(dict(b=b, l=l, row=row, bat=bat))
        row += b * l
        bat += b
    t_all = row
    seqlens = [g["l"] for g in groups]
    common = functools.reduce(math.gcd, seqlens)
    tiles = {k: _pick(v, common) for k, v in lim.items()}

    x = jnp.concatenate([xx.reshape(-1, d) for xx in xs], axis=0)
    mem = jnp.concatenate(mems, axis=0)

    def tile_batch(i):
        r0 = i * tiles["mergex"]
        bidx = 0
        for g in groups:
            bidx = jnp.where(r0 >= g["row"], g["bat"] + (r0 - g["row"]) // g["l"], bidx)
        return bidx

    seq_tiles = tuple((g["row"] // tiles["prep"], g["l"] // tiles["prep"]) for g in groups)
    band, far = _bias_tables(p["rel_bias"], tiles["attn"])
    nf = D_FF // FFN_CHUNK

    def ffn_weights(prefix, i):
        wg = p[prefix + "_w_gate"][i].reshape(d, nf, FFN_CHUNK).transpose(1, 0, 2).astype(BF16)
        wu = p[prefix + "_w_up"][i].reshape(d, nf, FFN_CHUNK).transpose(1, 0, 2).astype(BF16)
        wd = p[prefix + "_w_down"][i].reshape(nf, FFN_CHUNK, d).astype(BF16)
        return p[prefix + "_norm"][i].reshape(1, d), wg, wu, wd

    for i in range(depth):
        x = _ffn(x, *ffn_weights("ffn1", i), None, tiles["ffn"])

        w_in = p["w_in"][i]
        o_beta = 3 * GDN_QK
        o_z = o_beta + 2 * N_GATES
        o_qb = o_z + GDN_V
        gate_w = jnp.pad(w_in[:, o_beta:o_z], ((0, 0), (0, V7X_LANES - 2 * N_GATES)))
        w_perm = jnp.concatenate(
            [w_in[:, :o_beta], gate_w, w_in[:, o_z:o_qb],
             w_in[:, o_qb:o_qb + DIFF_QK] * (DIFF_DH ** -0.5 * LOG2E),
             w_in[:, o_qb + DIFF_QK:]],
            axis=1).astype(BF16)
        assert w_perm.shape[1] == _P_COLS
        qkva, gates, z, qb, kb, vb, sg = _proj(x, p["mix_norm"][i].reshape(1, d), w_perm,
                                               tiles["proj"])

        conv_w8 = jnp.pad(p["conv_w"][i], ((0, HALO - GDN_CONV), (0, 0)))
        lane_pad = (N_GATES, V7X_LANES - 2 * N_GATES)
        alog_pad = jnp.pad(p["gdn_a_log"][i].reshape(-1), lane_pad).reshape(1, V7X_LANES)
        dtb_pad = jnp.pad(p["gdn_dt_bias"][i].reshape(-1), lane_pad).reshape(1, V7X_LANES)
        qkvn, bg = _gdnprep(qkva, conv_w8, gates, alog_pad, dtb_pad, tiles["prep"], seq_tiles)
        bg_rows = bg[:, :2 * N_GATES].T

        lam_init = 0.8 - 0.6 * math.exp(-0.3 * i)
        o_fb = o_attn = None
        vbt = vb.T.reshape(DIFF_HEADS, DIFF_DV, t_all)
        ones_pad = jnp.zeros((DIFF_HEADS, VT_ROWS - DIFF_DV, t_all), BF16).at[:, 0].set(1.0)
        vbt = jnp.concatenate([vbt, ones_pad], axis=1).reshape(DIFF_HEADS * VT_ROWS, t_all)
        for g in groups:
            o_fb = _gdn(qkvn, bg, bg_rows, g["row"], g["b"], g["l"], tiles["gdn"], o_fb)
            o_attn = _attn(qb, kb, vbt, band, far, p["diff_lambda"][i],
                           p["diff_norm"][i].reshape(DIFF_DV, 1), g["row"], g["b"], g["l"],
                           tiles["attn"], lam_init, o_attn)
        o_f, o_b = o_fb

        kv = _memkv(mem, p["mem_norm"][i].reshape(1, d), p["xattn_wkv"][i].astype(BF16))
        x = _mergex(x, o_f, o_b, z, o_attn, sg, kv, p["gdn_norm"][i].reshape(1, GDN_DV),
                    p["w_up_a"][i].astype(BF16), p["w_up_b"][i].astype(BF16),
                    p["w_out"][i].astype(BF16), p["xattn_norm"][i].reshape(1, d),
                    p["xattn_wq"][i].astype(BF16), p["xattn_wo"][i].astype(BF16),
                    tiles["mergex"], tile_batch)

        final_w = p["final_norm"].reshape(1, d) if i == depth - 1 else None
        x = _ffn(x, *ffn_weights("ffn2", i), final_w, tiles["ffn"])

    return tuple(x[g["row"]:g["row"] + g["b"] * g["l"]].reshape(g["b"], g["l"], d)
                 for g in groups)


def kernel(x_prompt, x_sample, mem_prompt, mem_sample, ffn1_norm, ffn1_w_gate, ffn1_w_up, ffn1_w_down, mix_norm, w_in, conv_w, gdn_a_log, gdn_dt_bias, gdn_norm, w_up_a, diff_lambda, diff_norm, w_up_b, w_out, rel_bias, xattn_norm, mem_norm, xattn_wq, xattn_wkv, xattn_wo, ffn2_norm, ffn2_w_gate, ffn2_w_up, ffn2_w_down, final_norm):
    params = dict(
        ffn1_norm=ffn1_norm, ffn1_w_gate=ffn1_w_gate, ffn1_w_up=ffn1_w_up,
        ffn1_w_down=ffn1_w_down, mix_norm=mix_norm, w_in=w_in, conv_w=conv_w,
        gdn_a_log=gdn_a_log, gdn_dt_bias=gdn_dt_bias, gdn_norm=gdn_norm, w_up_a=w_up_a,
        diff_lambda=diff_lambda, diff_norm=diff_norm, w_up_b=w_up_b, w_out=w_out,
        rel_bias=rel_bias, xattn_norm=xattn_norm, mem_norm=mem_norm, xattn_wq=xattn_wq,
        xattn_wkv=xattn_wkv, xattn_wo=xattn_wo, ffn2_norm=ffn2_norm,
        ffn2_w_gate=ffn2_w_gate, ffn2_w_up=ffn2_w_up, ffn2_w_down=ffn2_w_down,
        final_norm=final_norm)
    y_prompt, y_sample = _encode((x_prompt, x_sample), (mem_prompt, mem_sample), params)
    return (y_prompt, y_sample)
```

```python
import functools
import math

import jax
import jax.numpy as jnp
from jax import lax
from jax.experimental import pallas as pl
from jax.experimental.pallas import tpu as pltpu

F32 = jnp.float32
BF16 = jnp.bfloat16
HIGHEST = lax.Precision.HIGHEST

EPS = 1e-6
LOG2E = math.log2(math.e)
D_MODEL = 1024
N_MEM = 256
GDN_HEADS = 4
GDN_DK = 128
GDN_DV = 128
GDN_CONV = 5
GDN_CHUNK = 64
GDN_BASE = 8
DIFF_HEADS = 8
DIFF_DH = 64
DIFF_DV = 2 * DIFF_DH
N_BUCKETS = 32
MAX_DISTANCE = 128
X_HEADS = 4
X_DH = 128
D_FF = 2816
GDN_QK = GDN_HEADS * GDN_DK
GDN_V = GDN_HEADS * GDN_DV
DIFF_QK = DIFF_HEADS * 2 * DIFF_DH
DIFF_VW = DIFF_HEADS * DIFF_DV
N_GATES = 2 * GDN_HEADS
VT_ROWS = DIFF_DV + 16

V7X_LANES = 128
V7X_SUBLANES = 8
V7X_VMEM_LIMIT = 56 * 1024 * 1024

FFN_CHUNK = 256
HALO = V7X_SUBLANES


def _cparams(sem):
    return pltpu.CompilerParams(dimension_semantics=sem, vmem_limit_bytes=V7X_VMEM_LIMIT)


def _dot(a, b, precision=None):
    return jnp.dot(a, b, preferred_element_type=F32, precision=precision)


def _dot_nt(a, b, precision=None):
    return lax.dot_general(a, b, (((1,), (1,)), ((), ())),
                           preferred_element_type=F32, precision=precision)


def _dot_tn(a, b, precision=None):
    return lax.dot_general(a, b, (((0,), (0,)), ((), ())),
                           preferred_element_type=F32, precision=precision)


def _dot_split(a, b):
    a_hi = a.astype(BF16)
    b_hi = b.astype(BF16)
    a_lo = (a - a_hi.astype(F32)).astype(BF16)
    b_lo = (b - b_hi.astype(F32)).astype(BF16)
    return _dot(a_hi, b_hi) + (_dot(a_hi, b_lo) + _dot(a_lo, b_hi))


def _rms(x, w):
    return x * lax.rsqrt(jnp.mean(x * x, axis=-1, keepdims=True) + EPS) * w


def _resident(shape):
    nd = len(shape)
    return pl.BlockSpec(shape, lambda *_: (0,) * nd, pipeline_mode=pl.Buffered(1))


def _ffn_body(x_ref, nw_ref, wg_ref, wu_ref, wd_ref, *rest, n_chunks, final):
    if final:
        fn_ref, o_ref, h_ref, acc_ref = rest
    else:
        o_ref, h_ref, acc_ref = rest
    x = x_ref[...]
    h_ref[...] = _rms(x, nw_ref[...]).astype(BF16)
    acc_ref[...] = jnp.zeros_like(acc_ref)

    def chunk(c, carry):
        h = h_ref[...]
        g = _dot(h, wg_ref[c])
        u = _dot(h, wu_ref[c])
        a = (g * jax.nn.sigmoid(g) * u).astype(BF16)
        acc_ref[...] += _dot(a, wd_ref[c])
        return carry

    lax.fori_loop(0, n_chunks, chunk, 0)
    y = x + 0.5 * acc_ref[...]
    if final:
        y = _rms(y, fn_ref[...])
    o_ref[...] = y


def _ffn(x, nw, wg, wu, wd, final_w, tm):
    t, d = x.shape
    nf = wg.shape[0]
    row = pl.BlockSpec((tm, d), lambda i: (i, 0))
    in_specs = [row, _resident((1, d)), _resident(wg.shape), _resident(wu.shape),
                _resident(wd.shape)]
    args = [x, nw, wg, wu, wd]
    if final_w is not None:
        in_specs.append(_resident((1, d)))
        args.append(final_w)
    return pl.pallas_call(
        functools.partial(_ffn_body, n_chunks=nf, final=final_w is not None),
        grid=(t // tm,),
        in_specs=in_specs,
        out_specs=row,
        out_shape=jax.ShapeDtypeStruct((t, d), F32),
        scratch_shapes=[pltpu.VMEM((tm, d), BF16), pltpu.VMEM((tm, d), F32)],
        compiler_params=_cparams(("parallel",)),
        name="ffn",
    )(*args)


_P_QKVA = (0, 3 * GDN_QK)
_P_GATE = (_P_QKVA[1], _P_QKVA[1] + V7X_LANES)
_P_Z = (_P_GATE[1], _P_GATE[1] + GDN_V)
_P_QB = (_P_Z[1], _P_Z[1] + DIFF_QK)
_P_KB = (_P_QB[1], _P_QB[1] + DIFF_QK)
_P_VB = (_P_KB[1], _P_KB[1] + DIFF_VW)
_P_SG = (_P_VB[1], _P_VB[1] + 2 * D_MODEL)
_P_COLS = _P_SG[1]


def _proj_body(x_ref, nw_ref, w_ref, qkva_ref, gate_ref, z_ref, qb_ref, kb_ref, vb_ref, sg_ref):
    u = _rms(x_ref[...], nw_ref[...]).astype(BF16)

    def seg(span):
        return _dot(u, w_ref[:, span[0]:span[1]])

    qkva_ref[...] = seg(_P_QKVA)
    gate_ref[...] = seg(_P_GATE)
    z_ref[...] = seg(_P_Z)
    qb_ref[...] = seg(_P_QB).astype(BF16)
    kb_ref[...] = seg(_P_KB).astype(BF16)
    vb_ref[...] = seg(_P_VB).astype(BF16)
    sg_ref[...] = jax.nn.sigmoid(seg(_P_SG))


def _proj(x, nw, w, tm):
    t, d = x.shape
    widths = [(_P_QKVA, F32), (_P_GATE, F32), (_P_Z, F32), (_P_QB, BF16), (_P_KB, BF16),
              (_P_VB, BF16), (_P_SG, F32)]
    out_shape = [jax.ShapeDtypeStruct((t, s[1] - s[0]), dt) for s, dt in widths]
    out_specs = [pl.BlockSpec((tm, s[1] - s[0]), lambda i: (i, 0)) for s, _ in widths]
    return pl.pallas_call(
        _proj_body,
        grid=(t // tm,),
        in_specs=[pl.BlockSpec((tm, d), lambda i: (i, 0)), _resident((1, d)), _resident(w.shape)],
        out_specs=out_specs,
        out_shape=out_shape,
        compiler_params=_cparams(("parallel",)),
        name="proj",
    )(x, nw, w)


def _gdnprep_body(cur_ref, prev_ref, next_ref, cw_ref, gate_ref, alog_ref, dtb_ref,
                  qkv_ref, bg_ref, *, tm, seq_tiles):
    i = pl.program_id(0)
    first = jnp.bool_(False)
    last = jnp.bool_(False)
    for start, per_seq in seq_tiles:
        rel = i - start
        first = first | ((rel >= 0) & (rel % per_seq == 0))
        last = last | ((rel >= 0) & (rel % per_seq == per_seq - 1))
    prev = jnp.where(first, 0.0, prev_ref[...])
    nxt = jnp.where(last, 0.0, next_ref[...])
    ext = jnp.concatenate([prev, cur_ref[...], nxt], axis=0)
    pad = (GDN_CONV - 1) // 2
    acc = None
    for k in range(GDN_CONV):
        lo = HALO - pad + k
        term = ext[lo:lo + tm, :] * cw_ref[k:k + 1, :]
        acc = term if acc is None else acc + term
    y = acc * jax.nn.sigmoid(acc)
    for h in range(2 * GDN_HEADS):
        lo = h * GDN_DK
        yh = y[:, lo:lo + GDN_DK]
        nrm = yh * lax.rsqrt(jnp.sum(yh * yh, axis=-1, keepdims=True) + EPS)
        if h < GDN_HEADS:
            nrm = nrm * (GDN_DK ** -0.5)
        qkv_ref[:, lo:lo + GDN_DK] = nrm
    qkv_ref[:, 2 * GDN_QK:] = y[:, 2 * GDN_QK:]
    gl = gate_ref[...]
    beta = jax.nn.sigmoid(gl)
    xa = gl + dtb_ref[...]
    softplus = jnp.maximum(xa, 0.0) + jnp.log1p(jnp.exp(-jnp.abs(xa)))
    g = -jnp.exp(alog_ref[...]) * softplus
    lane = lax.broadcasted_iota(jnp.int32, gl.shape, 1)
    bg_ref[...] = jnp.where(lane < N_GATES, beta, g)


def _gdnprep(qkva, conv_w8, gates, alog_pad, dtb_pad, tm, seq_tiles):
    t, c = qkva.shape
    hb = tm // HALO
    nblk = t // HALO
    return pl.pallas_call(
        functools.partial(_gdnprep_body, tm=tm, seq_tiles=seq_tiles),
        grid=(t // tm,),
        in_specs=[
            pl.BlockSpec((tm, c), lambda i: (i, 0)),
            pl.BlockSpec((HALO, c), lambda i: (jnp.maximum(i * hb - 1, 0), 0)),
            pl.BlockSpec((HALO, c), lambda i: (jnp.minimum((i + 1) * hb, nblk - 1), 0)),
            _resident(conv_w8.shape),
            pl.BlockSpec((tm, V7X_LANES), lambda i: (i, 0)),
            _resident((1, V7X_LANES)),
            _resident((1, V7X_LANES)),
        ],
        out_specs=[pl.BlockSpec((tm, c), lambda i: (i, 0)),
                   pl.BlockSpec((tm, V7X_LANES), lambda i: (i, 0))],
        out_shape=[jax.ShapeDtypeStruct((t, c), F32),
                   jax.ShapeDtypeStruct((t, V7X_LANES), F32)],
        compiler_params=_cparams(("parallel",)),
        name="gdnprep",
    )(qkva, qkva, qkva, conv_w8, gates, alog_pad, dtb_pad)


def _gdn_body(qkv_f, bgc_f, bgr_f, qkv_b, bgc_b, bgr_b, *rest, n_chunks):
    of_ref, ob_ref, s_ref = rest[-3:]
    c64 = GDN_CHUNK

    @pl.when(pl.program_id(1) == 0)
    def _():
        s_ref[...] = jnp.zeros_like(s_ref)

    row = lax.broadcasted_iota(jnp.int32, (c64, c64), 0)
    col = lax.broadcasted_iota(jnp.int32, (c64, c64), 1)
    eye = (row == col).astype(F32)
    incl = (row >= col, row <= col)
    strict = (row > col, row < col)
    same_block = {}
    size = GDN_BASE
    while size <= c64:
        same_block[size] = (row // size) == (col // size)
        size *= 2

    refs = ((qkv_f, bgc_f, bgr_f, of_ref), (qkv_b, bgc_b, bgr_b, ob_ref))
    chains = [(d, h) for d in range(2) for h in range(GDN_HEADS)]

    def row0(c, d):
        return (c if d == 0 else n_chunks - 1 - c) * c64

    def prepare(c):
        gates = []
        for d in range(2):
            _, bgc_ref, bgr_ref, _ = refs[d]
            r0 = row0(c, d)
            gt = bgc_ref[r0:r0 + c64, :]
            gcs = _dot(incl[d].astype(F32), gt, HIGHEST)
            grs = _dot(bgr_ref[:, r0:r0 + c64], incl[1 - d].astype(F32), HIGHEST)
            g_last = gcs[c64 - 1:c64, :] if d == 0 else gcs[0:1, :]
            gates.append((gt, gcs, grs, jnp.exp(gcs), jnp.exp(g_last - gcs), jnp.exp(g_last)))
        q, k, k16, kb, vb, gam, egc_c, ekd_c, egl_c = ([] for _ in range(9))
        for d, h in chains:
            qkv_ref = refs[d][0]
            r0 = row0(c, d)
            gt, gcs, grs, egc, ekd, egl = gates[d]
            idx = d * GDN_HEADS + h
            gi = N_GATES + idx
            qq = qkv_ref[r0:r0 + c64, h * GDN_DK:(h + 1) * GDN_DK]
            kk_ = qkv_ref[r0:r0 + c64, GDN_QK + h * GDN_DK:GDN_QK + (h + 1) * GDN_DK]
            vv = qkv_ref[r0:r0 + c64, 2 * GDN_QK + h * GDN_DV:2 * GDN_QK + (h + 1) * GDN_DV]
            beta = gt[:, idx:idx + 1]
            diff = gcs[:, gi:gi + 1] - grs[gi:gi + 1, :]
            gam.append(jnp.where(incl[d], jnp.exp(jnp.where(incl[d], diff, 0.0)), 0.0))
            q.append(qq)
            k.append(kk_)
            k16.append(kk_.astype(BF16))
            kb.append(kk_ * beta)
            vb.append(vv * beta)
            egc_c.append(egc[:, gi:gi + 1])
            ekd_c.append(ekd[:, gi:gi + 1])
            egl_c.append(egl[:, gi:gi + 1])
        n = len(chains)
        kk = [_dot_nt(kb[i].astype(BF16), k16[i]) for i in range(n)]
        qk = [_dot_nt(q[i].astype(BF16), k16[i]) for i in range(n)]
        nn = [jnp.where(strict[chains[i][0]], kk[i] * gam[i], 0.0) for i in range(n)]
        m = [-jnp.where(same_block[GDN_BASE], nn[i], 0.0) for i in range(n)]
        x = [eye + m[i] for i in range(n)]
        for _ in range(2):
            m = [_dot_split(m[i], m[i]) for i in range(n)]
            x = [x[i] + _dot_split(x[i], m[i]) for i in range(n)]
        size = GDN_BASE
        while size < c64:
            e = [jnp.where(same_block[2 * size] & ~same_block[size], nn[i], 0.0)
                 for i in range(n)]
            ex = [_dot_split(e[i], x[i]) for i in range(n)]
            x = [x[i] - _dot_split(x[i], ex[i]) for i in range(n)]
            size *= 2
        rhs = [jnp.concatenate([vb[i], kb[i] * egc_c[i]], axis=1) for i in range(n)]
        sol = [_dot_split(x[i], rhs[i]) for i in range(n)]
        return dict(
            u=[sol[i][:, :GDN_DV] for i in range(n)],
            w=[sol[i][:, GDN_DV:].astype(BF16) for i in range(n)],
            aqk=[(qk[i] * gam[i]).astype(BF16) for i in range(n)],
            qd=[(q[i] * egc_c[i]).astype(BF16) for i in range(n)],
            kd=[(k[i] * ekd_c[i]).astype(BF16) for i in range(n)],
            egl=egl_c)

    def advance(c, pre):
        n = len(chains)
        s = [s_ref[i] for i in range(n)]
        s16 = [s[i].astype(BF16) for i in range(n)]
        ws = [_dot(pre["w"][i], s16[i]) for i in range(n)]
        qs = [_dot(pre["qd"][i], s16[i]) for i in range(n)]
        vn16 = [(pre["u"][i] - ws[i]).astype(BF16) for i in range(n)]
        av = [_dot(pre["aqk"][i], vn16[i]) for i in range(n)]
        kv = [_dot_tn(pre["kd"][i], vn16[i]) for i in range(n)]
        for i, (d, h) in enumerate(chains):
            r0 = row0(c, d)
            s_ref[i] = s[i] * pre["egl"][i] + kv[i]
            refs[d][3][r0:r0 + c64, h * GDN_DV:(h + 1) * GDN_DV] = qs[i] + av[i]

    pre = prepare(0)
    for c in range(n_chunks):
        nxt = prepare(c + 1) if c + 1 < n_chunks else None
        advance(c, pre)
        pre = nxt


def _gdn(qkvn, bg, bg_rows, row_off, batch, seqlen, ts, prev):
    t, c = qkvn.shape
    ns = seqlen // ts
    off = row_off // ts
    fwd = lambda b, s: (off + b * ns + s, 0)
    bwd = lambda b, s: (off + b * ns + ns - 1 - s, 0)
    fwd_r = lambda b, s: (0, off + b * ns + s)
    bwd_r = lambda b, s: (0, off + b * ns + ns - 1 - s)
    nr = bg_rows.shape[0]
    out = jax.ShapeDtypeStruct((t, GDN_V), F32)
    in_specs = [
        pl.BlockSpec((ts, c), fwd), pl.BlockSpec((ts, V7X_LANES), fwd),
        pl.BlockSpec((nr, ts), fwd_r),
        pl.BlockSpec((ts, c), bwd), pl.BlockSpec((ts, V7X_LANES), bwd),
        pl.BlockSpec((nr, ts), bwd_r),
    ]
    args = [qkvn, bg, bg_rows, qkvn, bg, bg_rows]
    aliases = {}
    if prev is not None:
        aliases = {len(args): 0, len(args) + 1: 1}
        in_specs += [pl.BlockSpec(memory_space=pl.ANY)] * 2
        args += list(prev)
    return pl.pallas_call(
        functools.partial(_gdn_body, n_chunks=ts // GDN_CHUNK),
        grid=(batch, ns),
        in_specs=in_specs,
        out_specs=[pl.BlockSpec((ts, GDN_V), fwd), pl.BlockSpec((ts, GDN_V), bwd)],
        out_shape=[out, out],
        scratch_shapes=[pltpu.VMEM((2 * GDN_HEADS, GDN_DK, GDN_DV), F32)],
        input_output_aliases=aliases,
        compiler_params=_cparams(("parallel", "arbitrary")),
        name="gdn",
    )(*args)


def _attn_body(far_ref, q_ref, k_ref, vt_ref, band_ref, lam_ref, nw_ref, *rest,
               lam_init, tk, nk):
    o_ref, qz_ref, sf_ref, mxf_ref, sb_ref, mxb_ref, m_ref, acc_ref = rest[-8:]
    far_buf = (sf_ref, mxf_ref)
    band_buf = (sb_ref, mxb_ref)
    h = pl.program_id(1)
    qi = pl.program_id(2)
    tq = q_ref.shape[0]
    sub = V7X_SUBLANES

    m_ref[...] = jnp.full_like(m_ref, -1e30)
    acc_ref[...] = jnp.zeros_like(acc_ref)
    q = q_ref[...]
    lane = lax.broadcasted_iota(jnp.int32, q.shape, 1)
    zero = jnp.zeros_like(q)
    qz_ref[0] = jnp.where(lane < DIFF_DH, q, zero)
    qz_ref[1] = jnp.where(lane >= DIFF_DH, q, zero)

    def scores(kt, buf, slot, bias_tile):
        s_ref, mx_ref = buf
        k = k_ref[pl.ds(pl.multiple_of(kt * tk, tk), tk), :]
        for mp in range(2):
            s = _dot_nt(k, qz_ref[mp])
            if bias_tile is not None:
                s = s + bias_tile
            s_ref[slot, mp] = s
            mx_ref[slot, mp] = jnp.max(s.reshape(tk // sub, sub, tq), axis=0)

    def accumulate(kt, buf, slot, c):
        s_ref, mx_ref = buf
        vt = vt_ref[:, pl.ds(pl.multiple_of(kt * tk, tk), tk)]
        for mp in range(2):
            m_cur = jnp.max(mx_ref[slot, mp], axis=0, keepdims=True) + c
            m_prev = m_ref[mp]
            m_new = jnp.maximum(m_prev, m_cur)
            alpha = jnp.exp2(m_prev - m_new)
            m_ref[mp] = m_new
            p = jnp.exp2(s_ref[slot, mp] - (m_new - c)).astype(BF16)
            acc_ref[mp] = alpha * acc_ref[mp] + _dot(vt, p)

    n_left = jnp.maximum(qi - 1, 0)
    right0 = jnp.minimum(qi + 2, nk)
    n_far = n_left + (nk - right0)

    def far_tile(f):
        return jnp.where(f < n_left, f, right0 + (f - n_left))

    def far_const(f):
        return jnp.where(f < n_left, far_ref[h, 0], far_ref[h, 1])

    band = [(qi, 1), (qi - 1, 0), (qi + 1, 2)]

    def band_scores(j):
        kt, dd = band[j]
        scores(jnp.clip(kt, 0, nk - 1), band_buf, j, band_ref[dd])

    def band_accumulate(j):
        kt, _ = band[j]
        valid = (kt >= 0) & (kt < nk)
        accumulate(jnp.clip(kt, 0, nk - 1), band_buf, j, jnp.where(valid, 0.0, -1e30))

    if nk >= 4:
        scores(far_tile(0), far_buf, 0, None)

        def far_step(f, slot):
            scores(far_tile(f + 1), far_buf, 1 - slot, None)
            accumulate(far_tile(f), far_buf, slot, far_const(f))

        n_steps = n_far - 1

        def far_quad(g, carry):
            for u in range(4):
                far_step(4 * g + u, u % 2)
            return carry

        lax.fori_loop(0, n_steps // 4, far_quad, 0)
        done = (n_steps // 4) * 4

        @pl.when(n_steps % 4 >= 2)
        def _():
            far_step(done, 0)
            far_step(done + 1, 1)

        @pl.when(n_steps % 2 == 1)
        def _():
            far_step(n_steps - 1, 0)

        band_scores(0)
        accumulate(far_tile(n_far - 1), far_buf, (n_far - 1) % 2, far_const(n_far - 1))
        band_scores(1)
        band_accumulate(0)
        band_scores(2)
        band_accumulate(1)
        band_accumulate(2)
    else:
        for j in range(3):
            band_scores(j)
            band_accumulate(j)

    lam = lam_ref[...]
    lam_full = (jnp.exp(jnp.sum(lam[0:1] * lam[1:2], keepdims=True))
                - jnp.exp(jnp.sum(lam[2:3] * lam[3:4], keepdims=True)) + lam_init)
    a0 = acc_ref[0]
    a1 = acc_ref[1]
    o = (a0[:DIFF_DV] / a0[DIFF_DV:DIFF_DV + 1]
         - lam_full * (a1[:DIFF_DV] / a1[DIFF_DV:DIFF_DV + 1]))
    o = o * lax.rsqrt(jnp.mean(o * o, axis=0, keepdims=True) + EPS)
    o = o * (nw_ref[...] * (1.0 - lam_init))
    o_ref[...] = o.T


def _attn(qb, kb, vbt, band, far, lam, nw_col, row_off, batch, seqlen, tq, lam_init, prev):
    t = qb.shape[0]
    nq = seqlen // tq
    off = row_off // tq
    off_seq = row_off // seqlen
    assert row_off % seqlen == 0
    qmap = lambda b, h, qi: (off + b * nq + qi, h)
    in_specs = [
        pl.BlockSpec(memory_space=pltpu.SMEM),
        pl.BlockSpec((tq, DIFF_DV), qmap),
        pl.BlockSpec((seqlen, DIFF_DV), lambda b, h, qi: (off_seq + b, h)),
        pl.BlockSpec((VT_ROWS, seqlen), lambda b, h, qi: (h, off_seq + b)),
        pl.BlockSpec((None, 3, tq, tq), lambda b, h, qi: (h, 0, 0, 0)),
        pl.BlockSpec(lam.shape, lambda b, h, qi: (0, 0)),
        pl.BlockSpec((DIFF_DV, 1), lambda b, h, qi: (0, 0)),
    ]
    args = [far, qb, kb, vbt, band, lam, nw_col]
    aliases = {}
    if prev is not None:
        aliases = {len(args): 0}
        in_specs.append(pl.BlockSpec(memory_space=pl.ANY))
        args.append(prev)
    return pl.pallas_call(
        functools.partial(_attn_body, lam_init=lam_init, tk=tq, nk=nq),
        grid=(batch, DIFF_HEADS, nq),
        in_specs=in_specs,
        input_output_aliases=aliases,
        out_specs=pl.BlockSpec((tq, DIFF_DV), qmap),
        out_shape=jax.ShapeDtypeStruct((t, DIFF_VW), F32),
        scratch_shapes=[pltpu.VMEM((2, tq, DIFF_DV), BF16),
                        pltpu.VMEM((2, 2, tq, tq), F32),
                        pltpu.VMEM((2, 2, V7X_SUBLANES, tq), F32),
                        pltpu.VMEM((3, 2, tq, tq), F32),
                        pltpu.VMEM((3, 2, V7X_SUBLANES, tq), F32),
                        pltpu.VMEM((2, 1, tq), F32),
                        pltpu.VMEM((2, VT_ROWS, tq), F32)],
        compiler_params=_cparams(("parallel", "parallel", "parallel")),
        name="diffattn",
    )(*args)


def _memkv_body(m_ref, nw_ref, w_ref, o_ref):
    h = _rms(m_ref[...], nw_ref[...]).astype(BF16)
    o_ref[...] = _dot(h, w_ref[...]).astype(BF16)


def _memkv(mem, nw, wkv):
    nb, nm, d = mem.shape
    return pl.pallas_call(
        _memkv_body,
        grid=(nb,),
        in_specs=[pl.BlockSpec((None, nm, d), lambda b: (b, 0, 0)), _resident((1, d)),
                  _resident(wkv.shape)],
        out_specs=pl.BlockSpec((None, nm, wkv.shape[1]), lambda b: (b, 0, 0)),
        out_shape=jax.ShapeDtypeStruct((nb, nm, wkv.shape[1]), BF16),
        compiler_params=_cparams(("parallel",)),
        name="memkv",
    )(mem, nw, wkv)


def _mergex_body(x_ref, of_ref, ob_ref, z_ref, oattn_ref, sg_ref, kv_ref, gn_ref, wua_ref,
                 wub_ref, wout_ref, xn_ref, wq_ref, wo_ref, o_ref):
    o = of_ref[...] + ob_ref[...]
    z = z_ref[...]
    gn = gn_ref[...]
    heads = []
    for h in range(GDN_HEADS):
        sl = slice(h * GDN_DV, (h + 1) * GDN_DV)
        zh = z[:, sl]
        heads.append(_rms(o[:, sl], gn) * (zh * jax.nn.sigmoid(zh)))
    oa = jnp.concatenate(heads, axis=1).astype(BF16)
    ya = _dot(oa, wua_ref[...])
    yb = _dot(oattn_ref[...].astype(BF16), wub_ref[...])
    sg = sg_ref[...]
    merged = sg[:, :D_MODEL] * ya + sg[:, D_MODEL:] * yb
    x = x_ref[...] + _dot(merged.astype(BF16), wout_ref[...])
    hq = _rms(x, xn_ref[...]).astype(BF16)
    q = _dot(hq, wq_ref[...]) * (X_DH ** -0.5)
    kv = kv_ref[...]
    outs = []
    for h in range(X_HEADS):
        sl = slice(h * X_DH, (h + 1) * X_DH)
        kh = kv[:, sl]
        vh = kv[:, X_HEADS * X_DH + h * X_DH:X_HEADS * X_DH + (h + 1) * X_DH]
        s = _dot_nt(q[:, sl].astype(BF16), kh)
        s = s - jnp.max(s, axis=-1, keepdims=True)
        p = jnp.exp(s)
        p = p / jnp.sum(p, axis=-1, keepdims=True)
        outs.append(_dot(p.astype(BF16), vh))
    ox = jnp.concatenate(outs, axis=1).astype(BF16)
    o_ref[...] = x + _dot(ox, wo_ref[...])


def _mergex(x, o_f, o_b, z, oattn, sg, kv, gn, wua, wub, wout, xn, wq, wo, tm, tile_batch):
    t, d = x.shape
    rows = lambda width: pl.BlockSpec((tm, width), lambda i: (i, 0))
    return pl.pallas_call(
        _mergex_body,
        grid=(t // tm,),
        in_specs=[rows(d), rows(GDN_V), rows(GDN_V), rows(GDN_V), rows(DIFF_VW), rows(2 * d),
                  pl.BlockSpec((None,) + kv.shape[1:], lambda i: (tile_batch(i), 0, 0)),
                  _resident(gn.shape), _resident(wua.shape), _resident(wub.shape),
                  _resident(wout.shape), _resident(xn.shape), _resident(wq.shape),
                  _resident(wo.shape)],
        out_specs=rows(d),
        out_shape=jax.ShapeDtypeStruct((t, d), F32),
        compiler_params=_cparams(("parallel",)),
        name="mergex",
    )(x, o_f, o_b, z, oattn, sg, kv, gn, wua, wub, wout, xn, wq, wo)


def _rel_bucket(rel):
    nb = N_BUCKETS // 2
    ret = jnp.where(rel > 0, nb, 0)
    n = jnp.abs(rel)
    max_exact = nb // 2
    nf = jnp.maximum(n, 1).astype(F32)
    large = max_exact + (jnp.log(nf / max_exact) / math.log(MAX_DISTANCE / max_exact)
                         * (nb - max_exact)).astype(jnp.int32)
    large = jnp.minimum(large, nb - 1)
    return ret + jnp.where(n < max_exact, n, large)


def _toeplitz_body(w_ref, o_ref):
    tq = o_ref.shape[-1]
    x = jnp.broadcast_to(w_ref[...], (tq, 2 * tq))
    o_ref[...] = pltpu.roll(x, 0, 1, stride=1, stride_axis=0)[:, :tq]


def _bias_tables(rel_bias, tq):
    assert tq >= MAX_DISTANCE
    rel = jnp.arange(-(2 * tq - 1), 2 * tq, dtype=jnp.int32)
    by_rel = (rel_bias[_rel_bucket(rel)].astype(F32) * LOG2E).T
    nh = by_rel.shape[0]
    gens = []
    for dd in range(3):
        lo = dd * tq
        gens.append(jnp.concatenate(
            [by_rel[:, lo:lo + tq][:, ::-1], jnp.zeros((nh, 1), F32),
             by_rel[:, lo + tq:lo + 2 * tq - 1][:, ::-1]], axis=1))
    gen = jnp.stack(gens, axis=1)[:, :, None, :]
    band = pl.pallas_call(
        _toeplitz_body,
        grid=(nh, 3),
        in_specs=[pl.BlockSpec((None, None, 1, 2 * tq), lambda h, d: (h, d, 0, 0))],
        out_specs=pl.BlockSpec((None, None, tq, tq), lambda h, d: (h, d, 0, 0)),
        out_shape=jax.ShapeDtypeStruct((nh, 3, tq, tq), F32),
        compiler_params=_cparams(("parallel", "parallel")),
        name="toeplitz",
    )(gen)
    far = jnp.stack([by_rel[:, 0], by_rel[:, -1]], axis=1)
    return band, far


def _pick(limit, n):
    tile = limit
    while n % tile:
        tile //= 2
    return tile


def _encode(xs, mems, p, tile_limits=None):
    lim = dict(ffn=512, proj=256, prep=256, gdn=512, attn=512, mergex=256)
    if tile_limits:
        lim.update(tile_limits)
    depth = p["w_in"].shape[0]
    d = D_MODEL
    groups = []
    row = 0
    bat = 0
    for x in xs:
        b, l, _ = x.shape
        groups.append(dict(b=b, l=l, row=row, bat=bat))
        row += b * l
        bat += b
    t_all = row
    seqlens = [g["l"] for g in groups]
    common = functools.reduce(math.gcd, seqlens)
    tiles = {k: _pick(v, common) for k, v in lim.items()}

    x = jnp.concatenate([xx.reshape(-1, d) for xx in xs], axis=0)
    mem = jnp.concatenate(mems, axis=0)

    def tile_batch(i):
        r0 = i * tiles["mergex"]
        bidx = 0
        for g in groups:
            bidx = jnp.where(r0 >= g["row"], g["bat"] + (r0 - g["row"]) // g["l"], bidx)
        return bidx

    seq_tiles = tuple((g["row"] // tiles["prep"], g["l"] // tiles["prep"]) for g in groups)
    band, far = _bias_tables(p["rel_bias"], tiles["attn"])
    nf = D_FF // FFN_CHUNK

    def ffn_weights(prefix, i):
        wg = p[prefix + "_w_gate"][i].reshape(d, nf, FFN_CHUNK).transpose(1, 0, 2).astype(BF16)
        wu = p[prefix + "_w_up"][i].reshape(d, nf, FFN_CHUNK).transpose(1, 0, 2).astype(BF16)
        wd = p[prefix + "_w_down"][i].reshape(nf, FFN_CHUNK, d).astype(BF16)
        return p[prefix + "_norm"][i].reshape(1, d), wg, wu, wd

    for i in range(depth):
        x = _ffn(x, *ffn_weights("ffn1", i), None, tiles["ffn"])

        w_in = p["w_in"][i]
        o_beta = 3 * GDN_QK
        o_z = o_beta + 2 * N_GATES
        o_qb = o_z + GDN_V
        gate_w = jnp.pad(w_in[:, o_beta:o_z], ((0, 0), (0, V7X_LANES - 2 * N_GATES)))
        w_perm = jnp.concatenate(
            [w_in[:, :o_beta], gate_w, w_in[:, o_z:o_qb],
             w_in[:, o_qb:o_qb + DIFF_QK] * (DIFF_DH ** -0.5 * LOG2E),
             w_in[:, o_qb + DIFF_QK:]],
            axis=1).astype(BF16)
        assert w_perm.shape[1] == _P_COLS
        qkva, gates, z, qb, kb, vb, sg = _proj(x, p["mix_norm"][i].reshape(1, d), w_perm,
                                               tiles["proj"])

        conv_w8 = jnp.pad(p["conv_w"][i], ((0, HALO - GDN_CONV), (0, 0)))
        lane_pad = (N_GATES, V7X_LANES - 2 * N_GATES)
        alog_pad = jnp.pad(p["gdn_a_log"][i].reshape(-1), lane_pad).reshape(1, V7X_LANES)
        dtb_pad = jnp.pad(p["gdn_dt_bias"][i].reshape(-1), lane_pad).reshape(1, V7X_LANES)
        qkvn, bg = _gdnprep(qkva, conv_w8, gates, alog_pad, dtb_pad, tiles["prep"], seq_tiles)
        bg_rows = bg[:, :2 * N_GATES].T

        lam_init = 0.8 - 0.6 * math.exp(-0.3 * i)
        o_fb = o_attn = None
        vbt = vb.T.reshape(DIFF_HEADS, DIFF_DV, t_all)
        ones_pad = jnp.zeros((DIFF_HEADS, VT_ROWS - DIFF_DV, t_all), BF16).at[:, 0].set(1.0)
        vbt = jnp.concatenate([vbt, ones_pad], axis=1).reshape(DIFF_HEADS * VT_ROWS, t_all)
        for g in groups:
            o_fb = _gdn(qkvn, bg, bg_rows, g["row"], g["b"], g["l"], tiles["gdn"], o_fb)
            o_attn = _attn(qb, kb, vbt, band, far, p["diff_lambda"][i],
                           p["diff_norm"][i].reshape(DIFF_DV, 1), g["row"], g["b"], g["l"],
                           tiles["attn"], lam_init, o_attn)
        o_f, o_b = o_fb

        kv = _memkv(mem, p["mem_norm"][i].reshape(1, d), p["xattn_wkv"][i].astype(BF16))
        x = _mergex(x, o_f, o_b, z, o_attn, sg, kv, p["gdn_norm"][i].reshape(1, GDN_DV),
                    p["w_up_a"][i].astype(BF16), p["w_up_b"][i].astype(BF16),
                    p["w_out"][i].astype(BF16), p["xattn_norm"][i].reshape(1, d),
                    p["xattn_wq"][i].astype(BF16), p["xattn_wo"][i].astype(BF16),
                    tiles["mergex"], tile_batch)

        final_w = p["final_norm"].reshape(1, d) if i == depth - 1 else None
        x = _ffn(x, *ffn_weights("ffn2", i), final_w, tiles["ffn"])

    return tuple(x[g["row"]:g["row"] + g["b"] * g["l"]].reshape(g["b"], g["l"], d)
                 for g in groups)


def kernel(x_prompt, x_sample, mem_prompt, mem_sample, ffn1_norm, ffn1_w_gate, ffn1_w_up, ffn1_w_down, mix_norm, w_in, conv_w, gdn_a_log, gdn_dt_bias, gdn_norm, w_up_a, diff_lambda, diff_norm, w_up_b, w_out, rel_bias, xattn_norm, mem_norm, xattn_wq, xattn_wkv, xattn_wo, ffn2_norm, ffn2_w_gate, ffn2_w_up, ffn2_w_down, final_norm):
    params = dict(
        ffn1_norm=ffn1_norm, ffn1_w_gate=ffn1_w_gate, ffn1_w_up=ffn1_w_up,
        ffn1_w_down=ffn1_w_down, mix_norm=mix_norm, w_in=w_in, conv_w=conv_w,
        gdn_a_log=gdn_a_log, gdn_dt_bias=gdn_dt_bias, gdn_norm=gdn_norm, w_up_a=w_up_a,
        diff_lambda=diff_lambda, diff_norm=diff_norm, w_up_b=w_up_b, w_out=w_out,
        rel_bias=rel_bias, xattn_norm=xattn_norm, mem_norm=mem_norm, xattn_wq=xattn_wq,
        xattn_wkv=xattn_wkv, xattn_wo=xattn_wo, ffn2_norm=ffn2_norm,
        ffn2_w_gate=ffn2_w_gate, ffn2_w_up=ffn2_w_up, ffn2_w_down=ffn2_w_down,
        final_norm=final_norm)
    y_prompt, y_sample = _encode((x_prompt, x_sample), (mem_prompt, mem_sample), params)
    return (y_prompt, y_sample)
```

```python
import functools
import math

import jax
import jax.numpy as jnp
from jax import lax
from jax.experimental import pallas as pl
from jax.experimental.pallas import tpu as pltpu

F32 = jnp.float32
BF16 = jnp.bfloat16
HIGHEST = lax.Precision.HIGHEST

EPS = 1e-6
LOG2E = math.log2(math.e)
D_MODEL = 1024
N_MEM = 256
GDN_HEADS = 4
GDN_DK = 128
GDN_DV = 128
GDN_CONV = 5
GDN_CHUNK = 64
GDN_BASE = 8
DIFF_HEADS = 8
DIFF_DH = 64
DIFF_DV = 2 * DIFF_DH
N_BUCKETS = 32
MAX_DISTANCE = 128
X_HEADS = 4
X_DH = 128
D_FF = 2816
GDN_QK = GDN_HEADS * GDN_DK
GDN_V = GDN_HEADS * GDN_DV
DIFF_QK = DIFF_HEADS * 2 * DIFF_DH
DIFF_VW = DIFF_HEADS * DIFF_DV
N_GATES = 2 * GDN_HEADS
VT_ROWS = DIFF_DV + 16

V7X_LANES = 128
V7X_SUBLANES = 8
V7X_VMEM_LIMIT = 56 * 1024 * 1024

FFN_CHUNK = 2816
HALO = V7X_SUBLANES


def _cparams(sem):
    return pltpu.CompilerParams(dimension_semantics=sem, vmem_limit_bytes=V7X_VMEM_LIMIT)


def _dot(a, b, precision=None):
    return jnp.dot(a, b, preferred_element_type=F32, precision=precision)


def _dot_nt(a, b, precision=None):
    return lax.dot_general(a, b, (((1,), (1,)), ((), ())),
                           preferred_element_type=F32, precision=precision)


def _dot_tn(a, b, precision=None):
    return lax.dot_general(a, b, (((0,), (0,)), ((), ())),
                           preferred_element_type=F32, precision=precision)


def _dot_split(a, b):
    a_hi = a.astype(BF16)
    b_hi = b.astype(BF16)
    a_lo = (a - a_hi.astype(F32)).astype(BF16)
    b_lo = (b - b_hi.astype(F32)).astype(BF16)
    return _dot(a_hi, b_hi) + (_dot(a_hi, b_lo) + _dot(a_lo, b_hi))


def _rms(x, w):
    return x * lax.rsqrt(jnp.mean(x * x, axis=-1, keepdims=True) + EPS) * w


def _resident(shape):
    nd = len(shape)
    return pl.BlockSpec(shape, lambda *_: (0,) * nd, pipeline_mode=pl.Buffered(1))


def _ffn_body(x_ref, nw_ref, wg_ref, wu_ref, wd_ref, *rest, n_chunks, final):
    if final:
        fn_ref, o_ref, h_ref, acc_ref = rest
    else:
        o_ref, h_ref, acc_ref = rest
    x = x_ref[...]
    h_ref[...] = _rms(x, nw_ref[...]).astype(BF16)
    acc_ref[...] = jnp.zeros_like(acc_ref)

    def chunk(c, carry):
        h = h_ref[...]
        g = _dot(h, wg_ref[c])
        u = _dot(h, wu_ref[c])
        a = (g * jax.nn.sigmoid(g) * u).astype(BF16)
        acc_ref[...] += _dot(a, wd_ref[c])
        return carry

    lax.fori_loop(0, n_chunks, chunk, 0)
    y = x + 0.5 * acc_ref[...]
    if final:
        y = _rms(y, fn_ref[...])
    o_ref[...] = y


def _ffn(x, nw, wg, wu, wd, final_w, tm):
    t, d = x.shape
    nf = wg.shape[0]
    row = pl.BlockSpec((tm, d), lambda i: (i, 0))
    in_specs = [row, _resident((1, d)), _resident(wg.shape), _resident(wu.shape),
                _resident(wd.shape)]
    args = [x, nw, wg, wu, wd]
    if final_w is not None:
        in_specs.append(_resident((1, d)))
        args.append(final_w)
    return pl.pallas_call(
        functools.partial(_ffn_body, n_chunks=nf, final=final_w is not None),
        grid=(t // tm,),
        in_specs=in_specs,
        out_specs=row,
        out_shape=jax.ShapeDtypeStruct((t, d), F32),
        scratch_shapes=[pltpu.VMEM((tm, d), BF16), pltpu.VMEM((tm, d), F32)],
        compiler_params=_cparams(("parallel",)),
        name="ffn",
    )(*args)


_P_QKVA = (0, 3 * GDN_QK)
_P_GATE = (_P_QKVA[1], _P_QKVA[1] + V7X_LANES)
_P_Z = (_P_GATE[1], _P_GATE[1] + GDN_V)
_P_QB = (_P_Z[1], _P_Z[1] + DIFF_QK)
_P_KB = (_P_QB[1], _P_QB[1] + DIFF_QK)
_P_VB = (_P_KB[1], _P_KB[1] + DIFF_VW)
_P_SG = (_P_VB[1], _P_VB[1] + 2 * D_MODEL)
_P_COLS = _P_SG[1]


def _proj_body(x_ref, nw_ref, w_ref, qkva_ref, gate_ref, z_ref, qb_ref, kb_ref, vb_ref, sg_ref):
    u = _rms(x_ref[...], nw_ref[...]).astype(BF16)

    def seg(span):
        return _dot(u, w_ref[:, span[0]:span[1]])

    qkva_ref[...] = seg(_P_QKVA)
    gate_ref[...] = seg(_P_GATE)
    z_ref[...] = seg(_P_Z)
    qb_ref[...] = seg(_P_QB).astype(BF16)
    kb_ref[...] = seg(_P_KB).astype(BF16)
    vb_ref[...] = seg(_P_VB).astype(BF16)
    sg_ref[...] = jax.nn.sigmoid(seg(_P_SG))


def _proj(x, nw, w, tm):
    t, d = x.shape
    widths = [(_P_QKVA, F32), (_P_GATE, F32), (_P_Z, F32), (_P_QB, BF16), (_P_KB, BF16),
              (_P_VB, BF16), (_P_SG, F32)]
    out_shape = [jax.ShapeDtypeStruct((t, s[1] - s[0]), dt) for s, dt in widths]
    out_specs = [pl.BlockSpec((tm, s[1] - s[0]), lambda i: (i, 0)) for s, _ in widths]
    return pl.pallas_call(
        _proj_body,
        grid=(t // tm,),
        in_specs=[pl.BlockSpec((tm, d), lambda i: (i, 0)), _resident((1, d)), _resident(w.shape)],
        out_specs=out_specs,
        out_shape=out_shape,
        compiler_params=_cparams(("parallel",)),
        name="proj",
    )(x, nw, w)


def _gdnprep_body(cur_ref, prev_ref, next_ref, cw_ref, gate_ref, alog_ref, dtb_ref,
                  qkv_ref, bg_ref, *, tm, seq_tiles):
    i = pl.program_id(0)
    first = jnp.bool_(False)
    last = jnp.bool_(False)
    for start, per_seq in seq_tiles:
        rel = i - start
        first = first | ((rel >= 0) & (rel % per_seq == 0))
        last = last | ((rel >= 0) & (rel % per_seq == per_seq - 1))
    prev = jnp.where(first, 0.0, prev_ref[...])
    nxt = jnp.where(last, 0.0, next_ref[...])
    ext = jnp.concatenate([prev, cur_ref[...], nxt], axis=0)
    pad = (GDN_CONV - 1) // 2
    acc = None
    for k in range(GDN_CONV):
        lo = HALO - pad + k
        term = ext[lo:lo + tm, :] * cw_ref[k:k + 1, :]
        acc = term if acc is None else acc + term
    y = acc * jax.nn.sigmoid(acc)
    for h in range(2 * GDN_HEADS):
        lo = h * GDN_DK
        yh = y[:, lo:lo + GDN_DK]
        nrm = yh * lax.rsqrt(jnp.sum(yh * yh, axis=-1, keepdims=True) + EPS)
        if h < GDN_HEADS:
            nrm = nrm * (GDN_DK ** -0.5)
        qkv_ref[:, lo:lo + GDN_DK] = nrm
    qkv_ref[:, 2 * GDN_QK:] = y[:, 2 * GDN_QK:]
    gl = gate_ref[...]
    beta = jax.nn.sigmoid(gl)
    xa = gl + dtb_ref[...]
    softplus = jnp.maximum(xa, 0.0) + jnp.log1p(jnp.exp(-jnp.abs(xa)))
    g = -jnp.exp(alog_ref[...]) * softplus
    lane = lax.broadcasted_iota(jnp.int32, gl.shape, 1)
    bg_ref[...] = jnp.where(lane < N_GATES, beta, g)


def _gdnprep(qkva, conv_w8, gates, alog_pad, dtb_pad, tm, seq_tiles):
    t, c = qkva.shape
    hb = tm // HALO
    nblk = t // HALO
    return pl.pallas_call(
        functools.partial(_gdnprep_body, tm=tm, seq_tiles=seq_tiles),
        grid=(t // tm,),
        in_specs=[
            pl.BlockSpec((tm, c), lambda i: (i, 0)),
            pl.BlockSpec((HALO, c), lambda i: (jnp.maximum(i * hb - 1, 0), 0)),
            pl.BlockSpec((HALO, c), lambda i: (jnp.minimum((i + 1) * hb, nblk - 1), 0)),
            _resident(conv_w8.shape),
            pl.BlockSpec((tm, V7X_LANES), lambda i: (i, 0)),
            _resident((1, V7X_LANES)),
            _resident((1, V7X_LANES)),
        ],
        out_specs=[pl.BlockSpec((tm, c), lambda i: (i, 0)),
                   pl.BlockSpec((tm, V7X_LANES), lambda i: (i, 0))],
        out_shape=[jax.ShapeDtypeStruct((t, c), F32),
                   jax.ShapeDtypeStruct((t, V7X_LANES), F32)],
        compiler_params=_cparams(("parallel",)),
        name="gdnprep",
    )(qkva, qkva, qkva, conv_w8, gates, alog_pad, dtb_pad)


def _gdn_body(qkv_f, bgc_f, bgr_f, qkv_b, bgc_b, bgr_b, *rest, n_chunks):
    of_ref, ob_ref, s_ref = rest[-3:]
    c64 = GDN_CHUNK

    @pl.when(pl.program_id(1) == 0)
    def _():
        s_ref[...] = jnp.zeros_like(s_ref)

    row = lax.broadcasted_iota(jnp.int32, (c64, c64), 0)
    col = lax.broadcasted_iota(jnp.int32, (c64, c64), 1)
    eye = (row == col).astype(F32)
    incl = (row >= col, row <= col)
    strict = (row > col, row < col)
    same_block = {}
    size = GDN_BASE
    while size <= c64:
        same_block[size] = (row // size) == (col // size)
        size *= 2

    refs = ((qkv_f, bgc_f, bgr_f, of_ref), (qkv_b, bgc_b, bgr_b, ob_ref))
    chains = [(d, h) for d in range(2) for h in range(GDN_HEADS)]

    def row0(c, d):
        return (c if d == 0 else n_chunks - 1 - c) * c64

    def prepare(c):
        gates = []
        for d in range(2):
            _, bgc_ref, bgr_ref, _ = refs[d]
            r0 = row0(c, d)
            gt = bgc_ref[r0:r0 + c64, :]
            gcs = _dot(incl[d].astype(F32), gt, HIGHEST)
            grs = _dot(bgr_ref[:, r0:r0 + c64], incl[1 - d].astype(F32), HIGHEST)
            g_last = gcs[c64 - 1:c64, :] if d == 0 else gcs[0:1, :]
            gates.append((gt, gcs, grs, jnp.exp(gcs), jnp.exp(g_last - gcs), jnp.exp(g_last)))
        q, k, k16, kb, vb, gam, egc_c, ekd_c, egl_c = ([] for _ in range(9))
        for d, h in chains:
            qkv_ref = refs[d][0]
            r0 = row0(c, d)
            gt, gcs, grs, egc, ekd, egl = gates[d]
            idx = d * GDN_HEADS + h
            gi = N_GATES + idx
            qq = qkv_ref[r0:r0 + c64, h * GDN_DK:(h + 1) * GDN_DK]
            kk_ = qkv_ref[r0:r0 + c64, GDN_QK + h * GDN_DK:GDN_QK + (h + 1) * GDN_DK]
            vv = qkv_ref[r0:r0 + c64, 2 * GDN_QK + h * GDN_DV:2 * GDN_QK + (h + 1) * GDN_DV]
            beta = gt[:, idx:idx + 1]
            diff = gcs[:, gi:gi + 1] - grs[gi:gi + 1, :]
            gam.append(jnp.where(incl[d], jnp.exp(jnp.where(incl[d], diff, 0.0)), 0.0))
            q.append(qq)
            k.append(kk_)
            k16.append(kk_.astype(BF16))
            kb.append(kk_ * beta)
            vb.append(vv * beta)
            egc_c.append(egc[:, gi:gi + 1])
            ekd_c.append(ekd[:, gi:gi + 1])
            egl_c.append(egl[:, gi:gi + 1])
        n = len(chains)
        kk = [_dot_nt(kb[i].astype(BF16), k16[i]) for i in range(n)]
        qk = [_dot_nt(q[i].astype(BF16), k16[i]) for i in range(n)]
        nn = [jnp.where(strict[chains[i][0]], kk[i] * gam[i], 0.0) for i in range(n)]
        m = [-jnp.where(same_block[GDN_BASE], nn[i], 0.0) for i in range(n)]
        x = [eye + m[i] for i in range(n)]
        for _ in range(2):
            m = [_dot_split(m[i], m[i]) for i in range(n)]
            x = [x[i] + _dot_split(x[i], m[i]) for i in range(n)]
        size = GDN_BASE
        while size < c64:
            e = [jnp.where(same_block[2 * size] & ~same_block[size], nn[i], 0.0)
                 for i in range(n)]
            ex = [_dot_split(e[i], x[i]) for i in range(n)]
            x = [x[i] - _dot_split(x[i], ex[i]) for i in range(n)]
            size *= 2
        rhs = [jnp.concatenate([vb[i], kb[i] * egc_c[i]], axis=1) for i in range(n)]
        sol = [_dot_split(x[i], rhs[i]) for i in range(n)]
        return dict(
            u=[sol[i][:, :GDN_DV] for i in range(n)],
            w=[sol[i][:, GDN_DV:].astype(BF16) for i in range(n)],
            aqk=[(qk[i] * gam[i]).astype(BF16) for i in range(n)],
            qd=[(q[i] * egc_c[i]).astype(BF16) for i in range(n)],
            kd=[(k[i] * ekd_c[i]).astype(BF16) for i in range(n)],
            egl=egl_c)

    def advance(c, pre):
        n = len(chains)
        s = [s_ref[i] for i in range(n)]
        s16 = [s[i].astype(BF16) for i in range(n)]
        ws = [_dot(pre["w"][i], s16[i]) for i in range(n)]
        qs = [_dot(pre["qd"][i], s16[i]) for i in range(n)]
        vn16 = [(pre["u"][i] - ws[i]).astype(BF16) for i in range(n)]
        av = [_dot(pre["aqk"][i], vn16[i]) for i in range(n)]
        kv = [_dot_tn(pre["kd"][i], vn16[i]) for i in range(n)]
        for i, (d, h) in enumerate(chains):
            r0 = row0(c, d)
            s_ref[i] = s[i] * pre["egl"][i] + kv[i]
            refs[d][3][r0:r0 + c64, h * GDN_DV:(h + 1) * GDN_DV] = qs[i] + av[i]

    pre = prepare(0)
    for c in range(n_chunks):
        nxt = prepare(c + 1) if c + 1 < n_chunks else None
        advance(c, pre)
        pre = nxt


def _gdn(qkvn, bg, bg_rows, row_off, batch, seqlen, ts, prev):
    t, c = qkvn.shape
    ns = seqlen // ts
    off = row_off // ts
    fwd = lambda b, s: (off + b * ns + s, 0)
    bwd = lambda b, s: (off + b * ns + ns - 1 - s, 0)
    fwd_r = lambda b, s: (0, off + b * ns + s)
    bwd_r = lambda b, s: (0, off + b * ns + ns - 1 - s)
    nr = bg_rows.shape[0]
    out = jax.ShapeDtypeStruct((t, GDN_V), F32)
    in_specs = [
        pl.BlockSpec((ts, c), fwd), pl.BlockSpec((ts, V7X_LANES), fwd),
        pl.BlockSpec((nr, ts), fwd_r),
        pl.BlockSpec((ts, c), bwd), pl.BlockSpec((ts, V7X_LANES), bwd),
        pl.BlockSpec((nr, ts), bwd_r),
    ]
    args = [qkvn, bg, bg_rows, qkvn, bg, bg_rows]
    aliases = {}
    if prev is not None:
        aliases = {len(args): 0, len(args) + 1: 1}
        in_specs += [pl.BlockSpec(memory_space=pl.ANY)] * 2
        args += list(prev)
    return pl.pallas_call(
        functools.partial(_gdn_body, n_chunks=ts // GDN_CHUNK),
        grid=(batch, ns),
        in_specs=in_specs,
        out_specs=[pl.BlockSpec((ts, GDN_V), fwd), pl.BlockSpec((ts, GDN_V), bwd)],
        out_shape=[out, out],
        scratch_shapes=[pltpu.VMEM((2 * GDN_HEADS, GDN_DK, GDN_DV), F32)],
        input_output_aliases=aliases,
        compiler_params=_cparams(("parallel", "arbitrary")),
        name="gdn",
    )(*args)


def _attn_body(far_ref, q_ref, k_ref, vt_ref, band_ref, lam_ref, nw_ref, *rest,
               lam_init, tk, nk):
    o_ref, qz_ref, sf_ref, mxf_ref, sb_ref, mxb_ref, m_ref, acc_ref = rest[-8:]
    far_buf = (sf_ref, mxf_ref)
    band_buf = (sb_ref, mxb_ref)
    h = pl.program_id(1)
    qi = pl.program_id(2)
    tq = q_ref.shape[0]
    sub = V7X_SUBLANES

    m_ref[...] = jnp.full_like(m_ref, -1e30)
    acc_ref[...] = jnp.zeros_like(acc_ref)
    q = q_ref[...]
    lane = lax.broadcasted_iota(jnp.int32, q.shape, 1)
    zero = jnp.zeros_like(q)
    qz_ref[0] = jnp.where(lane < DIFF_DH, q, zero)
    qz_ref[1] = jnp.where(lane >= DIFF_DH, q, zero)

    def scores(kt, buf, slot, bias_tile):
        s_ref, mx_ref = buf
        k = k_ref[pl.ds(pl.multiple_of(kt * tk, tk), tk), :]
        for mp in range(2):
            s = _dot_nt(k, qz_ref[mp])
            if bias_tile is not None:
                s = s + bias_tile
            s_ref[slot, mp] = s
            mx_ref[slot, mp] = jnp.max(s.reshape(tk // sub, sub, tq), axis=0)

    def accumulate(kt, buf, slot, c):
        s_ref, mx_ref = buf
        vt = vt_ref[:, pl.ds(pl.multiple_of(kt * tk, tk), tk)]
        for mp in range(2):
            m_cur = jnp.max(mx_ref[slot, mp], axis=0, keepdims=True) + c
            m_prev = m_ref[mp]
            m_new = jnp.maximum(m_prev, m_cur)
            alpha = jnp.exp2(m_prev - m_new)
            m_ref[mp] = m_new
            p = jnp.exp2(s_ref[slot, mp] - (m_new - c)).astype(BF16)
            acc_ref[mp] = alpha * acc_ref[mp] + _dot(vt, p)

    n_left = jnp.maximum(qi - 1, 0)
    right0 = jnp.minimum(qi + 2, nk)
    n_far = n_left + (nk - right0)

    def far_tile(f):
        return jnp.where(f < n_left, f, right0 + (f - n_left))

    def far_const(f):
        return jnp.where(f < n_left, far_ref[h, 0], far_ref[h, 1])

    band = [(qi, 1), (qi - 1, 0), (qi + 1, 2)]

    def band_scores(j):
        kt, dd = band[j]
        scores(jnp.clip(kt, 0, nk - 1), band_buf, j, band_ref[dd])

    def band_accumulate(j):
        kt, _ = band[j]
        valid = (kt >= 0) & (kt < nk)
        accumulate(jnp.clip(kt, 0, nk - 1), band_buf, j, jnp.where(valid, 0.0, -1e30))

    if nk >= 4:
        scores(far_tile(0), far_buf, 0, None)

        def far_step(f, slot):
            scores(far_tile(f + 1), far_buf, 1 - slot, None)
            accumulate(far_tile(f), far_buf, slot, far_const(f))

        n_steps = n_far - 1

        def far_quad(g, carry):
            for u in range(4):
                far_step(4 * g + u, u % 2)
            return carry

        lax.fori_loop(0, n_steps // 4, far_quad, 0)
        done = (n_steps // 4) * 4

        @pl.when(n_steps % 4 >= 2)
        def _():
            far_step(done, 0)
            far_step(done + 1, 1)

        @pl.when(n_steps % 2 == 1)
        def _():
            far_step(n_steps - 1, 0)

        band_scores(0)
        accumulate(far_tile(n_far - 1), far_buf, (n_far - 1) % 2, far_const(n_far - 1))
        band_scores(1)
        band_accumulate(0)
        band_scores(2)
        band_accumulate(1)
        band_accumulate(2)
    else:
        for j in range(3):
            band_scores(j)
            band_accumulate(j)

    lam = lam_ref[...]
    lam_full = (jnp.exp(jnp.sum(lam[0:1] * lam[1:2], keepdims=True))
                - jnp.exp(jnp.sum(lam[2:3] * lam[3:4], keepdims=True)) + lam_init)
    a0 = acc_ref[0]
    a1 = acc_ref[1]
    o = (a0[:DIFF_DV] / a0[DIFF_DV:DIFF_DV + 1]
         - lam_full * (a1[:DIFF_DV] / a1[DIFF_DV:DIFF_DV + 1]))
    o = o * lax.rsqrt(jnp.mean(o * o, axis=0, keepdims=True) + EPS)
    o = o * (nw_ref[...] * (1.0 - lam_init))
    o_ref[...] = o.T


def _attn(qb, kb, vbt, band, far, lam, nw_col, row_off, batch, seqlen, tq, lam_init, prev):
    t = qb.shape[0]
    nq = seqlen // tq
    off = row_off // tq
    off_seq = row_off // seqlen
    assert row_off % seqlen == 0
    qmap = lambda b, h, qi: (off + b * nq + qi, h)
    in_specs = [
        pl.BlockSpec(memory_space=pltpu.SMEM),
        pl.BlockSpec((tq, DIFF_DV), qmap),
        pl.BlockSpec((seqlen, DIFF_DV), lambda b, h, qi: (off_seq + b, h)),
        pl.BlockSpec((VT_ROWS, seqlen), lambda b, h, qi: (h, off_seq + b)),
        pl.BlockSpec((None, 3, tq, tq), lambda b, h, qi: (h, 0, 0, 0)),
        pl.BlockSpec(lam.shape, lambda b, h, qi: (0, 0)),
        pl.BlockSpec((DIFF_DV, 1), lambda b, h, qi: (0, 0)),
    ]
    args = [far, qb, kb, vbt, band, lam, nw_col]
    aliases = {}
    if prev is not None:
        aliases = {len(args): 0}
        in_specs.append(pl.BlockSpec(memory_space=pl.ANY))
        args.append(prev)
    return pl.pallas_call(
        functools.partial(_attn_body, lam_init=lam_init, tk=tq, nk=nq),
        grid=(batch, DIFF_HEADS, nq),
        in_specs=in_specs,
        input_output_aliases=aliases,
        out_specs=pl.BlockSpec((tq, DIFF_DV), qmap),
        out_shape=jax.ShapeDtypeStruct((t, DIFF_VW), F32),
        scratch_shapes=[pltpu.VMEM((2, tq, DIFF_DV), BF16),
                        pltpu.VMEM((2, 2, tq, tq), F32),
                        pltpu.VMEM((2, 2, V7X_SUBLANES, tq), F32),
                        pltpu.VMEM((3, 2, tq, tq), F32),
                        pltpu.VMEM((3, 2, V7X_SUBLANES, tq), F32),
                        pltpu.VMEM((2, 1, tq), F32),
                        pltpu.VMEM((2, VT_ROWS, tq), F32)],
        compiler_params=_cparams(("parallel", "parallel", "parallel")),
        name="diffattn",
    )(*args)


def _memkv_body(m_ref, nw_ref, w_ref, o_ref):
    h = _rms(m_ref[...], nw_ref[...]).astype(BF16)
    o_ref[...] = _dot(h, w_ref[...]).astype(BF16)


def _memkv(mem, nw, wkv):
    nb, nm, d = mem.shape
    return pl.pallas_call(
        _memkv_body,
        grid=(nb,),
        in_specs=[pl.BlockSpec((None, nm, d), lambda b: (b, 0, 0)), _resident((1, d)),
                  _resident(wkv.shape)],
        out_specs=pl.BlockSpec((None, nm, wkv.shape[1]), lambda b: (b, 0, 0)),
        out_shape=jax.ShapeDtypeStruct((nb, nm, wkv.shape[1]), BF16),
        compiler_params=_cparams(("parallel",)),
        name="memkv",
    )(mem, nw, wkv)


def _mergex_body(x_ref, of_ref, ob_ref, z_ref, oattn_ref, sg_ref, kv_ref, gn_ref, wua_ref,
                 wub_ref, wout_ref, xn_ref, wq_ref, wo_ref, o_ref):
    o = of_ref[...] + ob_ref[...]
    z = z_ref[...]
    gn = gn_ref[...]
    heads = []
    for h in range(GDN_HEADS):
        sl = slice(h * GDN_DV, (h + 1) * GDN_DV)
        zh = z[:, sl]
        heads.append(_rms(o[:, sl], gn) * (zh * jax.nn.sigmoid(zh)))
    oa = jnp.concatenate(heads, axis=1).astype(BF16)
    ya = _dot(oa, wua_ref[...])
    yb = _dot(oattn_ref[...].astype(BF16), wub_ref[...])
    sg = sg_ref[...]
    merged = sg[:, :D_MODEL] * ya + sg[:, D_MODEL:] * yb
    x = x_ref[...] + _dot(merged.astype(BF16), wout_ref[...])
    hq = _rms(x, xn_ref[...]).astype(BF16)
    q = _dot(hq, wq_ref[...]) * (X_DH ** -0.5)
    kv = kv_ref[...]
    outs = []
    for h in range(X_HEADS):
        sl = slice(h * X_DH, (h + 1) * X_DH)
        kh = kv[:, sl]
        vh = kv[:, X_HEADS * X_DH + h * X_DH:X_HEADS * X_DH + (h + 1) * X_DH]
        s = _dot_nt(q[:, sl].astype(BF16), kh)
        s = s - jnp.max(s, axis=-1, keepdims=True)
        p = jnp.exp(s)
        p = p / jnp.sum(p, axis=-1, keepdims=True)
        outs.append(_dot(p.astype(BF16), vh))
    ox = jnp.concatenate(outs, axis=1).astype(BF16)
    o_ref[...] = x + _dot(ox, wo_ref[...])


def _mergex(x, o_f, o_b, z, oattn, sg, kv, gn, wua, wub, wout, xn, wq, wo, tm, tile_batch):
    t, d = x.shape
    rows = lambda width: pl.BlockSpec((tm, width), lambda i: (i, 0))
    return pl.pallas_call(
        _mergex_body,
        grid=(t // tm,),
        in_specs=[rows(d), rows(GDN_V), rows(GDN_V), rows(GDN_V), rows(DIFF_VW), rows(2 * d),
                  pl.BlockSpec((None,) + kv.shape[1:], lambda i: (tile_batch(i), 0, 0)),
                  _resident(gn.shape), _resident(wua.shape), _resident(wub.shape),
                  _resident(wout.shape), _resident(xn.shape), _resident(wq.shape),
                  _resident(wo.shape)],
        out_specs=rows(d),
        out_shape=jax.ShapeDtypeStruct((t, d), F32),
        compiler_params=_cparams(("parallel",)),
        name="mergex",
    )(x, o_f, o_b, z, oattn, sg, kv, gn, wua, wub, wout, xn, wq, wo)


def _rel_bucket(rel):
    nb = N_BUCKETS // 2
    ret = jnp.where(rel > 0, nb, 0)
    n = jnp.abs(rel)
    max_exact = nb // 2
    nf = jnp.maximum(n, 1).astype(F32)
    large = max_exact + (jnp.log(nf / max_exact) / math.log(MAX_DISTANCE / max_exact)
                         * (nb - max_exact)).astype(jnp.int32)
    large = jnp.minimum(large, nb - 1)
    return ret + jnp.where(n < max_exact, n, large)


def _toeplitz_body(w_ref, o_ref):
    tq = o_ref.shape[-1]
    x = jnp.broadcast_to(w_ref[...], (tq, 2 * tq))
    o_ref[...] = pltpu.roll(x, 0, 1, stride=1, stride_axis=0)[:, :tq]


def _bias_tables(rel_bias, tq):
    assert tq >= MAX_DISTANCE
    rel = jnp.arange(-(2 * tq - 1), 2 * tq, dtype=jnp.int32)
    by_rel = (rel_bias[_rel_bucket(rel)].astype(F32) * LOG2E).T
    nh = by_rel.shape[0]
    gens = []
    for dd in range(3):
        lo = dd * tq
        gens.append(jnp.concatenate(
            [by_rel[:, lo:lo + tq][:, ::-1], jnp.zeros((nh, 1), F32),
             by_rel[:, lo + tq:lo + 2 * tq - 1][:, ::-1]], axis=1))
    gen = jnp.stack(gens, axis=1)[:, :, None, :]
    band = pl.pallas_call(
        _toeplitz_body,
        grid=(nh, 3),
        in_specs=[pl.BlockSpec((None, None, 1, 2 * tq), lambda h, d: (h, d, 0, 0))],
        out_specs=pl.BlockSpec((None, None, tq, tq), lambda h, d: (h, d, 0, 0)),
        out_shape=jax.ShapeDtypeStruct((nh, 3, tq, tq), F32),
        compiler_params=_cparams(("parallel", "parallel")),
        name="toeplitz",
    )(gen)
    far = jnp.stack([by_rel[:, 0], by_rel[:, -1]], axis=1)
    return band, far


def _pick(limit, n):
    tile = limit
    while n % tile:
        tile //= 2
    return tile


def _encode(xs, mems, p, tile_limits=None):
    lim = dict(ffn=512, proj=256, prep=256, gdn=512, attn=512, mergex=256)
    if tile_limits:
        lim.update(tile_limits)
    depth = p["w_in"].shape[0]
    d = D_MODEL
    groups = []
    row = 0
    bat = 0
    for x in xs:
        b, l, _ = x.shape
        groups.append(dict(b=b, l=l, row=row, bat=bat))
        row += b * l
        bat += b
    t_all = row
    seqlens = [g["l"] for g in groups]
    common = functools.reduce(math.gcd, seqlens)
    tiles = {k: _pick(v, common) for k, v in lim.items()}

    x = jnp.concatenate([xx.reshape(-1, d) for xx in xs], axis=0)
    mem = jnp.concatenate(mems, axis=0)

    def tile_batch(i):
        r0 = i * tiles["mergex"]
        bidx = 0
        for g in groups:
            bidx = jnp.where(r0 >= g["row"], g["bat"] + (r0 - g["row"]) // g["l"], bidx)
        return bidx

    seq_tiles = tuple((g["row"] // tiles["prep"], g["l"] // tiles["prep"]) for g in groups)
    band, far = _bias_tables(p["rel_bias"], tiles["attn"])
    nf = D_FF // FFN_CHUNK

    def ffn_weights(prefix, i):
        wg = p[prefix + "_w_gate"][i].reshape(d, nf, FFN_CHUNK).transpose(1, 0, 2).astype(BF16)
        wu = p[prefix + "_w_up"][i].reshape(d, nf, FFN_CHUNK).transpose(1, 0, 2).astype(BF16)
        wd = p[prefix + "_w_down"][i].reshape(nf, FFN_CHUNK, d).astype(BF16)
        return p[prefix + "_norm"][i].reshape(1, d), wg, wu, wd

    for i in range(depth):
        x = _ffn(x, *ffn_weights("ffn1", i), None, tiles["ffn"])

        w_in = p["w_in"][i]
        o_beta = 3 * GDN_QK
        o_z = o_beta + 2 * N_GATES
        o_qb = o_z + GDN_V
        gate_w = jnp.pad(w_in[:, o_beta:o_z], ((0, 0), (0, V7X_LANES - 2 * N_GATES)))
        w_perm = jnp.concatenate(
            [w_in[:, :o_beta], gate_w, w_in[:, o_z:o_qb],
             w_in[:, o_qb:o_qb + DIFF_QK] * (DIFF_DH ** -0.5 * LOG2E),
             w_in[:, o_qb + DIFF_QK:]],
            axis=1).astype(BF16)
        assert w_perm.shape[1] == _P_COLS
        qkva, gates, z, qb, kb, vb, sg = _proj(x, p["mix_norm"][i].reshape(1, d), w_perm,
                                               tiles["proj"])

        conv_w8 = jnp.pad(p["conv_w"][i], ((0, HALO - GDN_CONV), (0, 0)))
        lane_pad = (N_GATES, V7X_LANES - 2 * N_GATES)
        alog_pad = jnp.pad(p["gdn_a_log"][i].reshape(-1), lane_pad).reshape(1, V7X_LANES)
        dtb_pad = jnp.pad(p["gdn_dt_bias"][i].reshape(-1), lane_pad).reshape(1, V7X_LANES)
        qkvn, bg = _gdnprep(qkva, conv_w8, gates, alog_pad, dtb_pad, tiles["prep"], seq_tiles)
        bg_rows = bg[:, :2 * N_GATES].T

        lam_init = 0.8 - 0.6 * math.exp(-0.3 * i)
        o_fb = o_attn = None
        vbt = vb.T.reshape(DIFF_HEADS, DIFF_DV, t_all)
        ones_pad = jnp.zeros((DIFF_HEADS, VT_ROWS - DIFF_DV, t_all), BF16).at[:, 0].set(1.0)
        vbt = jnp.concatenate([vbt, ones_pad], axis=1).reshape(DIFF_HEADS * VT_ROWS, t_all)
        for g in groups:
            o_fb = _gdn(qkvn, bg, bg_rows, g["row"], g["b"], g["l"], tiles["gdn"], o_fb)
            o_attn = _attn(qb, kb, vbt, band, far, p["diff_lambda"][i],
                           p["diff_norm"][i].reshape(DIFF_DV, 1), g["row"], g["b"], g["l"],
                           tiles["attn"], lam_init, o_attn)
        o_f, o_b = o_fb

        kv = _memkv(mem, p["mem_norm"][i].reshape(1, d), p["xattn_wkv"][i].astype(BF16))
        x = _mergex(x, o_f, o_b, z, o_attn, sg, kv, p["gdn_norm"][i].reshape(1, GDN_DV),
                    p["w_up_a"][i].astype(BF16), p["w_up_b"][i].astype(BF16),
                    p["w_out"][i].astype(BF16), p["xattn_norm"][i].reshape(1, d),
                    p["xattn_wq"][i].astype(BF16), p["xattn_wo"][i].astype(BF16),
                    tiles["mergex"], tile_batch)

        final_w = p["final_norm"].reshape(1, d) if i == depth - 1 else None
        x = _ffn(x, *ffn_weights("ffn2", i), final_w, tiles["ffn"])

    return tuple(x[g["row"]:g["row"] + g["b"] * g["l"]].reshape(g["b"], g["l"], d)
                 for g in groups)


def kernel(x_prompt, x_sample, mem_prompt, mem_sample, ffn1_norm, ffn1_w_gate, ffn1_w_up, ffn1_w_down, mix_norm, w_in, conv_w, gdn_a_log, gdn_dt_bias, gdn_norm, w_up_a, diff_lambda, diff_norm, w_up_b, w_out, rel_bias, xattn_norm, mem_norm, xattn_wq, xattn_wkv, xattn_wo, ffn2_norm, ffn2_w_gate, ffn2_w_up, ffn2_w_down, final_norm):
    params = dict(
        ffn1_norm=ffn1_norm, ffn1_w_gate=ffn1_w_gate, ffn1_w_up=ffn1_w_up,
        ffn1_w_down=ffn1_w_down, mix_norm=mix_norm, w_in=w_in, conv_w=conv_w,
        gdn_a_log=gdn_a_log, gdn_dt_bias=gdn_dt_bias, gdn_norm=gdn_norm, w_up_a=w_up_a,
        diff_lambda=diff_lambda, diff_norm=diff_norm, w_up_b=w_up_b, w_out=w_out,
        rel_bias=rel_bias, xattn_norm=xattn_norm, mem_norm=mem_norm, xattn_wq=xattn_wq,
        xattn_wkv=xattn_wkv, xattn_wo=xattn_wo, ffn2_norm=ffn2_norm,
        ffn2_w_gate=ffn2_w_gate, ffn2_w_up=ffn2_w_up, ffn2_w_down=ffn2_w_down,
        final_norm=final_norm)
    y_prompt, y_sample = _encode((x_prompt, x_sample), (mem_prompt, mem_sample), params)
    return (y_prompt, y_sample)
```

```python
import functools
import math

import jax
import jax.numpy as jnp
from jax import lax
from jax.experimental import pallas as pl
from jax.experimental.pallas import tpu as pltpu

F32 = jnp.float32
BF16 = jnp.bfloat16
HIGHEST = lax.Precision.HIGHEST

EPS = 1e-6
LOG2E = math.log2(math.e)
D_MODEL = 1024
N_MEM = 256
GDN_HEADS = 4
GDN_DK = 128
GDN_DV = 128
GDN_CONV = 5
GDN_CHUNK = 64
GDN_BASE = 8
DIFF_HEADS = 8
DIFF_DH = 64
DIFF_DV = 2 * DIFF_DH
N_BUCKETS = 32
MAX_DISTANCE = 128
X_HEADS = 4
X_DH = 128
D_FF = 2816
GDN_QK = GDN_HEADS * GDN_DK
GDN_V = GDN_HEADS * GDN_DV
DIFF_QK = DIFF_HEADS * 2 * DIFF_DH
DIFF_VW = DIFF_HEADS * DIFF_DV
N_GATES = 2 * GDN_HEADS
VT_ROWS = DIFF_DV + 16

V7X_LANES = 128
V7X_SUBLANES = 8
V7X_VMEM_LIMIT = 56 * 1024 * 1024

FFN_CHUNK = 2816
FAR_UNROLL = 8
HALO = V7X_SUBLANES


def _cparams(sem):
    return pltpu.CompilerParams(dimension_semantics=sem, vmem_limit_bytes=V7X_VMEM_LIMIT)


def _dot(a, b, precision=None):
    return jnp.dot(a, b, preferred_element_type=F32, precision=precision)


def _dot_nt(a, b, precision=None):
    return lax.dot_general(a, b, (((1,), (1,)), ((), ())),
                           preferred_element_type=F32, precision=precision)


def _dot_tn(a, b, precision=None):
    return lax.dot_general(a, b, (((0,), (0,)), ((), ())),
                           preferred_element_type=F32, precision=precision)


def _dot_split(a, b):
    a_hi = a.astype(BF16)
    b_hi = b.astype(BF16)
    a_lo = (a - a_hi.astype(F32)).astype(BF16)
    b_lo = (b - b_hi.astype(F32)).astype(BF16)
    return _dot(a_hi, b_hi) + (_dot(a_hi, b_lo) + _dot(a_lo, b_hi))


def _rms(x, w):
    return x * lax.rsqrt(jnp.mean(x * x, axis=-1, keepdims=True) + EPS) * w


def _resident(shape):
    nd = len(shape)
    return pl.BlockSpec(shape, lambda *_: (0,) * nd, pipeline_mode=pl.Buffered(1))


def _ffn_body(x_ref, nw_ref, wg_ref, wu_ref, wd_ref, *rest, n_chunks, final):
    if final:
        fn_ref, o_ref, h_ref, acc_ref = rest
    else:
        o_ref, h_ref, acc_ref = rest
    x = x_ref[...]
    h_ref[...] = _rms(x, nw_ref[...]).astype(BF16)
    acc_ref[...] = jnp.zeros_like(acc_ref)

    def chunk(c, carry):
        h = h_ref[...]
        g = _dot(h, wg_ref[c])
        u = _dot(h, wu_ref[c])
        a = (g * jax.nn.sigmoid(g) * u).astype(BF16)
        acc_ref[...] += _dot(a, wd_ref[c])
        return carry

    lax.fori_loop(0, n_chunks, chunk, 0)
    y = x + 0.5 * acc_ref[...]
    if final:
        y = _rms(y, fn_ref[...])
    o_ref[...] = y


def _ffn(x, nw, wg, wu, wd, final_w, tm):
    t, d = x.shape
    nf = wg.shape[0]
    row = pl.BlockSpec((tm, d), lambda i: (i, 0))
    in_specs = [row, _resident((1, d)), _resident(wg.shape), _resident(wu.shape),
                _resident(wd.shape)]
    args = [x, nw, wg, wu, wd]
    if final_w is not None:
        in_specs.append(_resident((1, d)))
        args.append(final_w)
    return pl.pallas_call(
        functools.partial(_ffn_body, n_chunks=nf, final=final_w is not None),
        grid=(t // tm,),
        in_specs=in_specs,
        out_specs=row,
        out_shape=jax.ShapeDtypeStruct((t, d), F32),
        scratch_shapes=[pltpu.VMEM((tm, d), BF16), pltpu.VMEM((tm, d), F32)],
        compiler_params=_cparams(("parallel",)),
        name="ffn",
    )(*args)


_P_QKVA = (0, 3 * GDN_QK)
_P_GATE = (_P_QKVA[1], _P_QKVA[1] + V7X_LANES)
_P_Z = (_P_GATE[1], _P_GATE[1] + GDN_V)
_P_QB = (_P_Z[1], _P_Z[1] + DIFF_QK)
_P_KB = (_P_QB[1], _P_QB[1] + DIFF_QK)
_P_VB = (_P_KB[1], _P_KB[1] + DIFF_VW)
_P_SG = (_P_VB[1], _P_VB[1] + 2 * D_MODEL)
_P_COLS = _P_SG[1]


def _proj_body(x_ref, nw_ref, w_ref, qkva_ref, gate_ref, z_ref, qb_ref, kb_ref, vb_ref, sg_ref):
    u = _rms(x_ref[...], nw_ref[...]).astype(BF16)

    def seg(span):
        return _dot(u, w_ref[:, span[0]:span[1]])

    qkva_ref[...] = seg(_P_QKVA)
    gate_ref[...] = seg(_P_GATE)
    z_ref[...] = seg(_P_Z)
    qb_ref[...] = seg(_P_QB).astype(BF16)
    kb_ref[...] = seg(_P_KB).astype(BF16)
    vb_ref[...] = seg(_P_VB).astype(BF16)
    sg_ref[...] = jax.nn.sigmoid(seg(_P_SG))


def _proj(x, nw, w, tm):
    t, d = x.shape
    widths = [(_P_QKVA, F32), (_P_GATE, F32), (_P_Z, F32), (_P_QB, BF16), (_P_KB, BF16),
              (_P_VB, BF16), (_P_SG, F32)]
    out_shape = [jax.ShapeDtypeStruct((t, s[1] - s[0]), dt) for s, dt in widths]
    out_specs = [pl.BlockSpec((tm, s[1] - s[0]), lambda i: (i, 0)) for s, _ in widths]
    return pl.pallas_call(
        _proj_body,
        grid=(t // tm,),
        in_specs=[pl.BlockSpec((tm, d), lambda i: (i, 0)), _resident((1, d)), _resident(w.shape)],
        out_specs=out_specs,
        out_shape=out_shape,
        compiler_params=_cparams(("parallel",)),
        name="proj",
    )(x, nw, w)


def _gdnprep_body(cur_ref, prev_ref, next_ref, cw_ref, gate_ref, alog_ref, dtb_ref,
                  qkv_ref, bg_ref, *, tm, seq_tiles):
    i = pl.program_id(0)
    first = jnp.bool_(False)
    last = jnp.bool_(False)
    for start, per_seq in seq_tiles:
        rel = i - start
        first = first | ((rel >= 0) & (rel % per_seq == 0))
        last = last | ((rel >= 0) & (rel % per_seq == per_seq - 1))
    prev = jnp.where(first, 0.0, prev_ref[...])
    nxt = jnp.where(last, 0.0, next_ref[...])
    ext = jnp.concatenate([prev, cur_ref[...], nxt], axis=0)
    pad = (GDN_CONV - 1) // 2
    acc = None
    for k in range(GDN_CONV):
        lo = HALO - pad + k
        term = ext[lo:lo + tm, :] * cw_ref[k:k + 1, :]
        acc = term if acc is None else acc + term
    y = acc * jax.nn.sigmoid(acc)
    for h in range(2 * GDN_HEADS):
        lo = h * GDN_DK
        yh = y[:, lo:lo + GDN_DK]
        nrm = yh * lax.rsqrt(jnp.sum(yh * yh, axis=-1, keepdims=True) + EPS)
        if h < GDN_HEADS:
            nrm = nrm * (GDN_DK ** -0.5)
        qkv_ref[:, lo:lo + GDN_DK] = nrm
    qkv_ref[:, 2 * GDN_QK:] = y[:, 2 * GDN_QK:]
    gl = gate_ref[...]
    beta = jax.nn.sigmoid(gl)
    xa = gl + dtb_ref[...]
    softplus = jnp.maximum(xa, 0.0) + jnp.log1p(jnp.exp(-jnp.abs(xa)))
    g = -jnp.exp(alog_ref[...]) * softplus
    lane = lax.broadcasted_iota(jnp.int32, gl.shape, 1)
    bg_ref[...] = jnp.where(lane < N_GATES, beta, g)


def _gdnprep(qkva, conv_w8, gates, alog_pad, dtb_pad, tm, seq_tiles):
    t, c = qkva.shape
    hb = tm // HALO
    nblk = t // HALO
    return pl.pallas_call(
        functools.partial(_gdnprep_body, tm=tm, seq_tiles=seq_tiles),
        grid=(t // tm,),
        in_specs=[
            pl.BlockSpec((tm, c), lambda i: (i, 0)),
            pl.BlockSpec((HALO, c), lambda i: (jnp.maximum(i * hb - 1, 0), 0)),
            pl.BlockSpec((HALO, c), lambda i: (jnp.minimum((i + 1) * hb, nblk - 1), 0)),
            _resident(conv_w8.shape),
            pl.BlockSpec((tm, V7X_LANES), lambda i: (i, 0)),
            _resident((1, V7X_LANES)),
            _resident((1, V7X_LANES)),
        ],
        out_specs=[pl.BlockSpec((tm, c), lambda i: (i, 0)),
                   pl.BlockSpec((tm, V7X_LANES), lambda i: (i, 0))],
        out_shape=[jax.ShapeDtypeStruct((t, c), F32),
                   jax.ShapeDtypeStruct((t, V7X_LANES), F32)],
        compiler_params=_cparams(("parallel",)),
        name="gdnprep",
    )(qkva, qkva, qkva, conv_w8, gates, alog_pad, dtb_pad)


def _gdn_body(qkv_f, bgc_f, bgr_f, qkv_b, bgc_b, bgr_b, *rest, n_chunks):
    of_ref, ob_ref, s_ref = rest[-3:]
    c64 = GDN_CHUNK

    @pl.when(pl.program_id(1) == 0)
    def _():
        s_ref[...] = jnp.zeros_like(s_ref)

    row = lax.broadcasted_iota(jnp.int32, (c64, c64), 0)
    col = lax.broadcasted_iota(jnp.int32, (c64, c64), 1)
    eye = (row == col).astype(F32)
    incl = (row >= col, row <= col)
    strict = (row > col, row < col)
    same_block = {}
    size = GDN_BASE
    while size <= c64:
        same_block[size] = (row // size) == (col // size)
        size *= 2

    refs = ((qkv_f, bgc_f, bgr_f, of_ref), (qkv_b, bgc_b, bgr_b, ob_ref))
    chains = [(d, h) for d in range(2) for h in range(GDN_HEADS)]

    def row0(c, d):
        return (c if d == 0 else n_chunks - 1 - c) * c64

    def prepare(c):
        gates = []
        for d in range(2):
            _, bgc_ref, bgr_ref, _ = refs[d]
            r0 = row0(c, d)
            gt = bgc_ref[r0:r0 + c64, :]
            gcs = _dot(incl[d].astype(F32), gt, HIGHEST)
            grs = _dot(bgr_ref[:, r0:r0 + c64], incl[1 - d].astype(F32), HIGHEST)
            g_last = gcs[c64 - 1:c64, :] if d == 0 else gcs[0:1, :]
            gates.append((gt, gcs, grs, jnp.exp(gcs), jnp.exp(g_last - gcs), jnp.exp(g_last)))
        q, k, k16, kb, vb, gam, egc_c, ekd_c, egl_c = ([] for _ in range(9))
        for d, h in chains:
            qkv_ref = refs[d][0]
            r0 = row0(c, d)
            gt, gcs, grs, egc, ekd, egl = gates[d]
            idx = d * GDN_HEADS + h
            gi = N_GATES + idx
            qq = qkv_ref[r0:r0 + c64, h * GDN_DK:(h + 1) * GDN_DK]
            kk_ = qkv_ref[r0:r0 + c64, GDN_QK + h * GDN_DK:GDN_QK + (h + 1) * GDN_DK]
            vv = qkv_ref[r0:r0 + c64, 2 * GDN_QK + h * GDN_DV:2 * GDN_QK + (h + 1) * GDN_DV]
            beta = gt[:, idx:idx + 1]
            diff = gcs[:, gi:gi + 1] - grs[gi:gi + 1, :]
            gam.append(jnp.where(incl[d], jnp.exp(jnp.where(incl[d], diff, 0.0)), 0.0))
            q.append(qq)
            k.append(kk_)
            k16.append(kk_.astype(BF16))
            kb.append(kk_ * beta)
            vb.append(vv * beta)
            egc_c.append(egc[:, gi:gi + 1])
            ekd_c.append(ekd[:, gi:gi + 1])
            egl_c.append(egl[:, gi:gi + 1])
        n = len(chains)
        kk = [_dot_nt(kb[i].astype(BF16), k16[i]) for i in range(n)]
        qk = [_dot_nt(q[i].astype(BF16), k16[i]) for i in range(n)]
        nn = [jnp.where(strict[chains[i][0]], kk[i] * gam[i], 0.0) for i in range(n)]
        m = [-jnp.where(same_block[GDN_BASE], nn[i], 0.0) for i in range(n)]
        x = [eye + m[i] for i in range(n)]
        for _ in range(2):
            m = [_dot_split(m[i], m[i]) for i in range(n)]
            x = [x[i] + _dot_split(x[i], m[i]) for i in range(n)]
        size = GDN_BASE
        while size < c64:
            e = [jnp.where(same_block[2 * size] & ~same_block[size], nn[i], 0.0)
                 for i in range(n)]
            ex = [_dot_split(e[i], x[i]) for i in range(n)]
            x = [x[i] - _dot_split(x[i], ex[i]) for i in range(n)]
            size *= 2
        rhs = [jnp.concatenate([vb[i], kb[i] * egc_c[i]], axis=1) for i in range(n)]
        sol = [_dot_split(x[i], rhs[i]) for i in range(n)]
        return dict(
            u=[sol[i][:, :GDN_DV] for i in range(n)],
            w=[sol[i][:, GDN_DV:].astype(BF16) for i in range(n)],
            aqk=[(qk[i] * gam[i]).astype(BF16) for i in range(n)],
            qd=[(q[i] * egc_c[i]).astype(BF16) for i in range(n)],
            kd=[(k[i] * ekd_c[i]).astype(BF16) for i in range(n)],
            egl=egl_c)

    def advance(c, pre):
        n = len(chains)
        s = [s_ref[i] for i in range(n)]
        s16 = [s[i].astype(BF16) for i in range(n)]
        ws = [_dot(pre["w"][i], s16[i]) for i in range(n)]
        qs = [_dot(pre["qd"][i], s16[i]) for i in range(n)]
        vn16 = [(pre["u"][i] - ws[i]).astype(BF16) for i in range(n)]
        av = [_dot(pre["aqk"][i], vn16[i]) for i in range(n)]
        kv = [_dot_tn(pre["kd"][i], vn16[i]) for i in range(n)]
        for i, (d, h) in enumerate(chains):
            r0 = row0(c, d)
            s_ref[i] = s[i] * pre["egl"][i] + kv[i]
            refs[d][3][r0:r0 + c64, h * GDN_DV:(h + 1) * GDN_DV] = qs[i] + av[i]

    pre = prepare(0)
    for c in range(n_chunks):
        nxt = prepare(c + 1) if c + 1 < n_chunks else None
        advance(c, pre)
        pre = nxt


def _gdn(qkvn, bg, bg_rows, row_off, batch, seqlen, ts, prev):
    t, c = qkvn.shape
    ns = seqlen // ts
    off = row_off // ts
    fwd = lambda b, s: (off + b * ns + s, 0)
    bwd = lambda b, s: (off + b * ns + ns - 1 - s, 0)
    fwd_r = lambda b, s: (0, off + b * ns + s)
    bwd_r = lambda b, s: (0, off + b * ns + ns - 1 - s)
    nr = bg_rows.shape[0]
    out = jax.ShapeDtypeStruct((t, GDN_V), F32)
    in_specs = [
        pl.BlockSpec((ts, c), fwd), pl.BlockSpec((ts, V7X_LANES), fwd),
        pl.BlockSpec((nr, ts), fwd_r),
        pl.BlockSpec((ts, c), bwd), pl.BlockSpec((ts, V7X_LANES), bwd),
        pl.BlockSpec((nr, ts), bwd_r),
    ]
    args = [qkvn, bg, bg_rows, qkvn, bg, bg_rows]
    aliases = {}
    if prev is not None:
        aliases = {len(args): 0, len(args) + 1: 1}
        in_specs += [pl.BlockSpec(memory_space=pl.ANY)] * 2
        args += list(prev)
    return pl.pallas_call(
        functools.partial(_gdn_body, n_chunks=ts // GDN_CHUNK),
        grid=(batch, ns),
        in_specs=in_specs,
        out_specs=[pl.BlockSpec((ts, GDN_V), fwd), pl.BlockSpec((ts, GDN_V), bwd)],
        out_shape=[out, out],
        scratch_shapes=[pltpu.VMEM((2 * GDN_HEADS, GDN_DK, GDN_DV), F32)],
        input_output_aliases=aliases,
        compiler_params=_cparams(("parallel", "arbitrary")),
        name="gdn",
    )(*args)


def _attn_body(far_ref, q_ref, k_ref, vt_ref, band_ref, lam_ref, nw_ref, *rest,
               lam_init, tk, nk):
    o_ref, qz_ref, sf_ref, mxf_ref, sb_ref, mxb_ref, m_ref, acc_ref = rest[-8:]
    far_buf = (sf_ref, mxf_ref)
    band_buf = (sb_ref, mxb_ref)
    h = pl.program_id(1)
    qi = pl.program_id(2)
    tq = q_ref.shape[0]
    sub = V7X_SUBLANES

    m_ref[...] = jnp.full_like(m_ref, -1e30)
    acc_ref[...] = jnp.zeros_like(acc_ref)
    q = q_ref[...]
    lane = lax.broadcasted_iota(jnp.int32, q.shape, 1)
    zero = jnp.zeros_like(q)
    qz_ref[0] = jnp.where(lane < DIFF_DH, q, zero)
    qz_ref[1] = jnp.where(lane >= DIFF_DH, q, zero)

    def scores(kt, buf, slot, bias_tile):
        s_ref, mx_ref = buf
        k = k_ref[pl.ds(pl.multiple_of(kt * tk, tk), tk), :]
        for mp in range(2):
            s = _dot_nt(k, qz_ref[mp])
            if bias_tile is not None:
                s = s + bias_tile
            s_ref[slot, mp] = s
            mx_ref[slot, mp] = jnp.max(s.reshape(tk // sub, sub, tq), axis=0)

    def accumulate(kt, buf, slot, c):
        s_ref, mx_ref = buf
        vt = vt_ref[:, pl.ds(pl.multiple_of(kt * tk, tk), tk)]
        for mp in range(2):
            m_cur = jnp.max(mx_ref[slot, mp], axis=0, keepdims=True) + c
            m_prev = m_ref[mp]
            m_new = jnp.maximum(m_prev, m_cur)
            alpha = jnp.exp2(m_prev - m_new)
            m_ref[mp] = m_new
            p = jnp.exp2(s_ref[slot, mp] - (m_new - c)).astype(BF16)
            acc_ref[mp] = alpha * acc_ref[mp] + _dot(vt, p)

    n_left = jnp.maximum(qi - 1, 0)
    right0 = jnp.minimum(qi + 2, nk)
    n_far = n_left + (nk - right0)

    def far_tile(f):
        return jnp.where(f < n_left, f, right0 + (f - n_left))

    def far_const(f):
        return jnp.where(f < n_left, far_ref[h, 0], far_ref[h, 1])

    band = [(qi, 1), (qi - 1, 0), (qi + 1, 2)]

    def band_scores(j):
        kt, dd = band[j]
        scores(jnp.clip(kt, 0, nk - 1), band_buf, j, band_ref[dd])

    def band_accumulate(j):
        kt, _ = band[j]
        valid = (kt >= 0) & (kt < nk)
        accumulate(jnp.clip(kt, 0, nk - 1), band_buf, j, jnp.where(valid, 0.0, -1e30))

    if nk >= 4:
        scores(far_tile(0), far_buf, 0, None)

        def far_step(f, slot):
            scores(far_tile(f + 1), far_buf, 1 - slot, None)
            accumulate(far_tile(f), far_buf, slot, far_const(f))

        n_steps = n_far - 1

        def far_trip(g, carry):
            for u in range(FAR_UNROLL):
                far_step(FAR_UNROLL * g + u, u % 2)
            return carry

        lax.fori_loop(0, n_steps // FAR_UNROLL, far_trip, 0)
        done = (n_steps // FAR_UNROLL) * FAR_UNROLL
        size = FAR_UNROLL // 2
        while size >= 1:
            taken = (n_steps % (2 * size)) >= size

            @pl.when(taken)
            def _(done=done, size=size):
                for u in range(size):
                    far_step(done + u, u % 2)

            done = done + jnp.where(taken, size, 0)
            size //= 2

        band_scores(0)
        accumulate(far_tile(n_far - 1), far_buf, (n_far - 1) % 2, far_const(n_far - 1))
        band_scores(1)
        band_accumulate(0)
        band_scores(2)
        band_accumulate(1)
        band_accumulate(2)
    else:
        for j in range(3):
            band_scores(j)
            band_accumulate(j)

    lam = lam_ref[...]
    lam_full = (jnp.exp(jnp.sum(lam[0:1] * lam[1:2], keepdims=True))
                - jnp.exp(jnp.sum(lam[2:3] * lam[3:4], keepdims=True)) + lam_init)
    a0 = acc_ref[0]
    a1 = acc_ref[1]
    o = (a0[:DIFF_DV] / a0[DIFF_DV:DIFF_DV + 1]
         - lam_full * (a1[:DIFF_DV] / a1[DIFF_DV:DIFF_DV + 1]))
    o = o * lax.rsqrt(jnp.mean(o * o, axis=0, keepdims=True) + EPS)
    o = o * (nw_ref[...] * (1.0 - lam_init))
    o_ref[...] = o.T


def _attn(qb, kb, vbt, band, far, lam, nw_col, row_off, batch, seqlen, tq, lam_init, prev):
    t = qb.shape[0]
    nq = seqlen // tq
    off = row_off // tq
    off_seq = row_off // seqlen
    assert row_off % seqlen == 0
    qmap = lambda b, h, qi: (off + b * nq + qi, h)
    in_specs = [
        pl.BlockSpec(memory_space=pltpu.SMEM),
        pl.BlockSpec((tq, DIFF_DV), qmap),
        pl.BlockSpec((seqlen, DIFF_DV), lambda b, h, qi: (off_seq + b, h)),
        pl.BlockSpec((VT_ROWS, seqlen), lambda b, h, qi: (h, off_seq + b)),
        pl.BlockSpec((None, 3, tq, tq), lambda b, h, qi: (h, 0, 0, 0)),
        pl.BlockSpec(lam.shape, lambda b, h, qi: (0, 0)),
        pl.BlockSpec((DIFF_DV, 1), lambda b, h, qi: (0, 0)),
    ]
    args = [far, qb, kb, vbt, band, lam, nw_col]
    aliases = {}
    if prev is not None:
        aliases = {len(args): 0}
        in_specs.append(pl.BlockSpec(memory_space=pl.ANY))
        args.append(prev)
    return pl.pallas_call(
        functools.partial(_attn_body, lam_init=lam_init, tk=tq, nk=nq),
        grid=(batch, DIFF_HEADS, nq),
        in_specs=in_specs,
        input_output_aliases=aliases,
        out_specs=pl.BlockSpec((tq, DIFF_DV), qmap),
        out_shape=jax.ShapeDtypeStruct((t, DIFF_VW), F32),
        scratch_shapes=[pltpu.VMEM((2, tq, DIFF_DV), BF16),
                        pltpu.VMEM((2, 2, tq, tq), F32),
                        pltpu.VMEM((2, 2, V7X_SUBLANES, tq), F32),
                        pltpu.VMEM((3, 2, tq, tq), F32),
                        pltpu.VMEM((3, 2, V7X_SUBLANES, tq), F32),
                        pltpu.VMEM((2, 1, tq), F32),
                        pltpu.VMEM((2, VT_ROWS, tq), F32)],
        compiler_params=_cparams(("parallel", "parallel", "parallel")),
        name="diffattn",
    )(*args)


def _memkv_body(m_ref, nw_ref, w_ref, o_ref):
    h = _rms(m_ref[...], nw_ref[...]).astype(BF16)
    o_ref[...] = _dot(h, w_ref[...]).astype(BF16)


def _memkv(mem, nw, wkv):
    nb, nm, d = mem.shape
    return pl.pallas_call(
        _memkv_body,
        grid=(nb,),
        in_specs=[pl.BlockSpec((None, nm, d), lambda b: (b, 0, 0)), _resident((1, d)),
                  _resident(wkv.shape)],
        out_specs=pl.BlockSpec((None, nm, wkv.shape[1]), lambda b: (b, 0, 0)),
        out_shape=jax.ShapeDtypeStruct((nb, nm, wkv.shape[1]), BF16),
        compiler_params=_cparams(("parallel",)),
        name="memkv",
    )(mem, nw, wkv)


def _mergex_body(x_ref, of_ref, ob_ref, z_ref, oattn_ref, sg_ref, kv_ref, gn_ref, wua_ref,
                 wub_ref, wout_ref, xn_ref, wq_ref, wo_ref, o_ref):
    o = of_ref[...] + ob_ref[...]
    z = z_ref[...]
    gn = gn_ref[...]
    heads = []
    for h in range(GDN_HEADS):
        sl = slice(h * GDN_DV, (h + 1) * GDN_DV)
        zh = z[:, sl]
        heads.append(_rms(o[:, sl], gn) * (zh * jax.nn.sigmoid(zh)))
    oa = jnp.concatenate(heads, axis=1).astype(BF16)
    ya = _dot(oa, wua_ref[...])
    yb = _dot(oattn_ref[...].astype(BF16), wub_ref[...])
    sg = sg_ref[...]
    merged = sg[:, :D_MODEL] * ya + sg[:, D_MODEL:] * yb
    x = x_ref[...] + _dot(merged.astype(BF16), wout_ref[...])
    hq = _rms(x, xn_ref[...]).astype(BF16)
    q = _dot(hq, wq_ref[...]) * (X_DH ** -0.5)
    kv = kv_ref[...]
    outs = []
    for h in range(X_HEADS):
        sl = slice(h * X_DH, (h + 1) * X_DH)
        kh = kv[:, sl]
        vh = kv[:, X_HEADS * X_DH + h * X_DH:X_HEADS * X_DH + (h + 1) * X_DH]
        s = _dot_nt(q[:, sl].astype(BF16), kh)
        s = s - jnp.max(s, axis=-1, keepdims=True)
        p = jnp.exp(s)
        p = p / jnp.sum(p, axis=-1, keepdims=True)
        outs.append(_dot(p.astype(BF16), vh))
    ox = jnp.concatenate(outs, axis=1).astype(BF16)
    o_ref[...] = x + _dot(ox, wo_ref[...])


def _mergex(x, o_f, o_b, z, oattn, sg, kv, gn, wua, wub, wout, xn, wq, wo, tm, tile_batch):
    t, d = x.shape
    rows = lambda width: pl.BlockSpec((tm, width), lambda i: (i, 0))
    return pl.pallas_call(
        _mergex_body,
        grid=(t // tm,),
        in_specs=[rows(d), rows(GDN_V), rows(GDN_V), rows(GDN_V), rows(DIFF_VW), rows(2 * d),
                  pl.BlockSpec((None,) + kv.shape[1:], lambda i: (tile_batch(i), 0, 0)),
                  _resident(gn.shape), _resident(wua.shape), _resident(wub.shape),
                  _resident(wout.shape), _resident(xn.shape), _resident(wq.shape),
                  _resident(wo.shape)],
        out_specs=rows(d),
        out_shape=jax.ShapeDtypeStruct((t, d), F32),
        compiler_params=_cparams(("parallel",)),
        name="mergex",
    )(x, o_f, o_b, z, oattn, sg, kv, gn, wua, wub, wout, xn, wq, wo)


def _rel_bucket(rel):
    nb = N_BUCKETS // 2
    ret = jnp.where(rel > 0, nb, 0)
    n = jnp.abs(rel)
    max_exact = nb // 2
    nf = jnp.maximum(n, 1).astype(F32)
    large = max_exact + (jnp.log(nf / max_exact) / math.log(MAX_DISTANCE / max_exact)
                         * (nb - max_exact)).astype(jnp.int32)
    large = jnp.minimum(large, nb - 1)
    return ret + jnp.where(n < max_exact, n, large)


def _toeplitz_body(w_ref, o_ref):
    tq = o_ref.shape[-1]
    x = jnp.broadcast_to(w_ref[...], (tq, 2 * tq))
    o_ref[...] = pltpu.roll(x, 0, 1, stride=1, stride_axis=0)[:, :tq]


def _bias_tables(rel_bias, tq):
    assert tq >= MAX_DISTANCE
    rel = jnp.arange(-(2 * tq - 1), 2 * tq, dtype=jnp.int32)
    by_rel = (rel_bias[_rel_bucket(rel)].astype(F32) * LOG2E).T
    nh = by_rel.shape[0]
    gens = []
    for dd in range(3):
        lo = dd * tq
        gens.append(jnp.concatenate(
            [by_rel[:, lo:lo + tq][:, ::-1], jnp.zeros((nh, 1), F32),
             by_rel[:, lo + tq:lo + 2 * tq - 1][:, ::-1]], axis=1))
    gen = jnp.stack(gens, axis=1)[:, :, None, :]
    band = pl.pallas_call(
        _toeplitz_body,
        grid=(nh, 3),
        in_specs=[pl.BlockSpec((None, None, 1, 2 * tq), lambda h, d: (h, d, 0, 0))],
        out_specs=pl.BlockSpec((None, None, tq, tq), lambda h, d: (h, d, 0, 0)),
        out_shape=jax.ShapeDtypeStruct((nh, 3, tq, tq), F32),
        compiler_params=_cparams(("parallel", "parallel")),
        name="toeplitz",
    )(gen)
    far = jnp.stack([by_rel[:, 0], by_rel[:, -1]], axis=1)
    return band, far


def _pick(limit, n):
    tile = limit
    while n % tile:
        tile //= 2
    return tile


def _encode(xs, mems, p, tile_limits=None):
    lim = dict(ffn=512, proj=256, prep=256, gdn=512, attn=512, mergex=512)
    if tile_limits:
        lim.update(tile_limits)
    depth = p["w_in"].shape[0]
    d = D_MODEL
    groups = []
    row = 0
    bat = 0
    for x in xs:
        b, l, _ = x.shape
        groups.append(dict(b=b, l=l, row=row, bat=bat))
        row += b * l
        bat += b
    t_all = row
    seqlens = [g["l"] for g in groups]
    common = functools.reduce(math.gcd, seqlens)
    tiles = {k: _pick(v, common) for k, v in lim.items()}

    x = jnp.concatenate([xx.reshape(-1, d) for xx in xs], axis=0)
    mem = jnp.concatenate(mems, axis=0)

    def tile_batch(i):
        r0 = i * tiles["mergex"]
        bidx = 0
        for g in groups:
            bidx = jnp.where(r0 >= g["row"], g["bat"] + (r0 - g["row"]) // g["l"], bidx)
        return bidx

    seq_tiles = tuple((g["row"] // tiles["prep"], g["l"] // tiles["prep"]) for g in groups)
    band, far = _bias_tables(p["rel_bias"], tiles["attn"])
    nf = D_FF // FFN_CHUNK

    def ffn_weights(prefix, i):
        wg = p[prefix + "_w_gate"][i].reshape(d, nf, FFN_CHUNK).transpose(1, 0, 2).astype(BF16)
        wu = p[prefix + "_w_up"][i].reshape(d, nf, FFN_CHUNK).transpose(1, 0, 2).astype(BF16)
        wd = p[prefix + "_w_down"][i].reshape(nf, FFN_CHUNK, d).astype(BF16)
        return p[prefix + "_norm"][i].reshape(1, d), wg, wu, wd

    for i in range(depth):
        x = _ffn(x, *ffn_weights("ffn1", i), None, tiles["ffn"])

        w_in = p["w_in"][i]
        o_beta = 3 * GDN_QK
        o_z = o_beta + 2 * N_GATES
        o_qb = o_z + GDN_V
        gate_w = jnp.pad(w_in[:, o_beta:o_z], ((0, 0), (0, V7X_LANES - 2 * N_GATES)))
        w_perm = jnp.concatenate(
            [w_in[:, :o_beta], gate_w, w_in[:, o_z:o_qb],
             w_in[:, o_qb:o_qb + DIFF_QK] * (DIFF_DH ** -0.5 * LOG2E),
             w_in[:, o_qb + DIFF_QK:]],
            axis=1).astype(BF16)
        assert w_perm.shape[1] == _P_COLS
        qkva, gates, z, qb, kb, vb, sg = _proj(x, p["mix_norm"][i].reshape(1, d), w_perm,
                                               tiles["proj"])

        conv_w8 = jnp.pad(p["conv_w"][i], ((0, HALO - GDN_CONV), (0, 0)))
        lane_pad = (N_GATES, V7X_LANES - 2 * N_GATES)
        alog_pad = jnp.pad(p["gdn_a_log"][i].reshape(-1), lane_pad).reshape(1, V7X_LANES)
        dtb_pad = jnp.pad(p["gdn_dt_bias"][i].reshape(-1), lane_pad).reshape(1, V7X_LANES)
        qkvn, bg = _gdnprep(qkva, conv_w8, gates, alog_pad, dtb_pad, tiles["prep"], seq_tiles)
        bg_rows = bg[:, :2 * N_GATES].T

        lam_init = 0.8 - 0.6 * math.exp(-0.3 * i)
        o_fb = o_attn = None
        vbt = vb.T.reshape(DIFF_HEADS, DIFF_DV, t_all)
        ones_pad = jnp.zeros((DIFF_HEADS, VT_ROWS - DIFF_DV, t_all), BF16).at[:, 0].set(1.0)
        vbt = jnp.concatenate([vbt, ones_pad], axis=1).reshape(DIFF_HEADS * VT_ROWS, t_all)
        for g in groups:
            o_fb = _gdn(qkvn, bg, bg_rows, g["row"], g["b"], g["l"], tiles["gdn"], o_fb)
            o_attn = _attn(qb, kb, vbt, band, far, p["diff_lambda"][i],
                           p["diff_norm"][i].reshape(DIFF_DV, 1), g["row"], g["b"], g["l"],
                           tiles["attn"], lam_init, o_attn)
        o_f, o_b = o_fb

        kv = _memkv(mem, p["mem_norm"][i].reshape(1, d), p["xattn_wkv"][i].astype(BF16))
        x = _mergex(x, o_f, o_b, z, o_attn, sg, kv, p["gdn_norm"][i].reshape(1, GDN_DV),
                    p["w_up_a"][i].astype(BF16), p["w_up_b"][i].astype(BF16),
                    p["w_out"][i].astype(BF16), p["xattn_norm"][i].reshape(1, d),
                    p["xattn_wq"][i].astype(BF16), p["xattn_wo"][i].astype(BF16),
                    tiles["mergex"], tile_batch)

        final_w = p["final_norm"].reshape(1, d) if i == depth - 1 else None
        x = _ffn(x, *ffn_weights("ffn2", i), final_w, tiles["ffn"])

    return tuple(x[g["row"]:g["row"] + g["b"] * g["l"]].reshape(g["b"], g["l"], d)
                 for g in groups)


def kernel(x_prompt, x_sample, mem_prompt, mem_sample, ffn1_norm, ffn1_w_gate, ffn1_w_up, ffn1_w_down, mix_norm, w_in, conv_w, gdn_a_log, gdn_dt_bias, gdn_norm, w_up_a, diff_lambda, diff_norm, w_up_b, w_out, rel_bias, xattn_norm, mem_norm, xattn_wq, xattn_wkv, xattn_wo, ffn2_norm, ffn2_w_gate, ffn2_w_up, ffn2_w_down, final_norm):
    params = dict(
        ffn1_norm=ffn1_norm, ffn1_w_gate=ffn1_w_gate, ffn1_w_up=ffn1_w_up,
        ffn1_w_down=ffn1_w_down, mix_norm=mix_norm, w_in=w_in, conv_w=conv_w,
        gdn_a_log=gdn_a_log, gdn_dt_bias=gdn_dt_bias, gdn_norm=gdn_norm, w_up_a=w_up_a,
        diff_lambda=diff_lambda, diff_norm=diff_norm, w_up_b=w_up_b, w_out=w_out,
        rel_bias=rel_bias, xattn_norm=xattn_norm, mem_norm=mem_norm, xattn_wq=xattn_wq,
        xattn_wkv=xattn_wkv, xattn_wo=xattn_wo, ffn2_norm=ffn2_norm,
        ffn2_w_gate=ffn2_w_gate, ffn2_w_up=ffn2_w_up, ffn2_w_down=ffn2_w_down,
        final_norm=final_norm)
    y_prompt, y_sample = _encode((x_prompt, x_sample), (mem_prompt, mem_sample), params)
    return (y_prompt, y_sample)
```

```python
import functools
import math

import jax
import jax.numpy as jnp
from jax import lax
from jax.experimental import pallas as pl
from jax.experimental.pallas import tpu as pltpu

F32 = jnp.float32
BF16 = jnp.bfloat16
HIGHEST = lax.Precision.HIGHEST

EPS = 1e-6
LOG2E = math.log2(math.e)
D_MODEL = 1024
N_MEM = 256
GDN_HEADS = 4
GDN_DK = 128
GDN_DV = 128
GDN_CONV = 5
GDN_CHUNK = 64
GDN_BASE = 8
DIFF_HEADS = 8
DIFF_DH = 64
DIFF_DV = 2 * DIFF_DH
N_BUCKETS = 32
MAX_DISTANCE = 128
X_HEADS = 4
X_DH = 128
D_FF = 2816
GDN_QK = GDN_HEADS * GDN_DK
GDN_V = GDN_HEADS * GDN_DV
DIFF_QK = DIFF_HEADS * 2 * DIFF_DH
DIFF_VW = DIFF_HEADS * DIFF_DV
N_GATES = 2 * GDN_HEADS
VT_ROWS = DIFF_DV + 16

V7X_LANES = 128
V7X_SUBLANES = 8
V7X_VMEM_LIMIT = 56 * 1024 * 1024

FAR_UNROLL = 8
HALO = V7X_SUBLANES


def _cparams(sem):
    return pltpu.CompilerParams(dimension_semantics=sem, vmem_limit_bytes=V7X_VMEM_LIMIT)


def _dot(a, b, precision=None):
    return jnp.dot(a, b, preferred_element_type=F32, precision=precision)


def _dot_nt(a, b, precision=None):
    return lax.dot_general(a, b, (((1,), (1,)), ((), ())),
                           preferred_element_type=F32, precision=precision)


def _dot_tn(a, b, precision=None):
    return lax.dot_general(a, b, (((0,), (0,)), ((), ())),
                           preferred_element_type=F32, precision=precision)


def _dot_split(a, b):
    a_hi = a.astype(BF16)
    b_hi = b.astype(BF16)
    a_lo = (a - a_hi.astype(F32)).astype(BF16)
    b_lo = (b - b_hi.astype(F32)).astype(BF16)
    return _dot(a_hi, b_hi) + (_dot(a_hi, b_lo) + _dot(a_lo, b_hi))


def _rms(x, w):
    return x * lax.rsqrt(jnp.mean(x * x, axis=-1, keepdims=True) + EPS) * w


def _resident(shape):
    nd = len(shape)
    return pl.BlockSpec(shape, lambda *_: (0,) * nd, pipeline_mode=pl.Buffered(1))


def _ffn_body(x_ref, nw_ref, wg_ref, wu_ref, wd_ref, *rest, final):
    o_ref = rest[-1]
    x = x_ref[...]
    h = _rms(x, nw_ref[...]).astype(BF16)
    g = _dot(h, wg_ref[...])
    u = _dot(h, wu_ref[...])
    a = (g * jax.nn.sigmoid(g) * u).astype(BF16)
    y = x + 0.5 * _dot(a, wd_ref[...])
    if final:
        y = _rms(y, rest[0][...])
    o_ref[...] = y


def _ffn(x, nw, wg, wu, wd, final_w, tm):
    t, d = x.shape
    row = pl.BlockSpec((tm, d), lambda i: (i, 0))
    in_specs = [row, _resident((1, d)), _resident(wg.shape), _resident(wu.shape),
                _resident(wd.shape)]
    args = [x, nw, wg, wu, wd]
    if final_w is not None:
        in_specs.append(_resident((1, d)))
        args.append(final_w)
    return pl.pallas_call(
        functools.partial(_ffn_body, final=final_w is not None),
        grid=(t // tm,),
        in_specs=in_specs,
        out_specs=row,
        out_shape=jax.ShapeDtypeStruct((t, d), F32),
        compiler_params=_cparams(("parallel",)),
        name="ffn",
    )(*args)


_P_QKVA = (0, 3 * GDN_QK)
_P_GATE = (_P_QKVA[1], _P_QKVA[1] + V7X_LANES)
_P_Z = (_P_GATE[1], _P_GATE[1] + GDN_V)
_P_QB = (_P_Z[1], _P_Z[1] + DIFF_QK)
_P_KB = (_P_QB[1], _P_QB[1] + DIFF_QK)
_P_VB = (_P_KB[1], _P_KB[1] + DIFF_VW)
_P_SG = (_P_VB[1], _P_VB[1] + 2 * D_MODEL)
_P_COLS = _P_SG[1]


def _proj_body(x_ref, nw_ref, w_ref, qkva_ref, gate_ref, z_ref, qb_ref, kb_ref, vb_ref, sg_ref):
    u = _rms(x_ref[...], nw_ref[...]).astype(BF16)

    def seg(span):
        return _dot(u, w_ref[:, span[0]:span[1]])

    qkva_ref[...] = seg(_P_QKVA)
    gate_ref[...] = seg(_P_GATE)
    z_ref[...] = seg(_P_Z)
    qb_ref[...] = seg(_P_QB).astype(BF16)
    kb_ref[...] = seg(_P_KB).astype(BF16)
    vb_ref[...] = seg(_P_VB).astype(BF16)
    sg_ref[...] = jax.nn.sigmoid(seg(_P_SG))


def _proj(x, nw, w, tm):
    t, d = x.shape
    widths = [(_P_QKVA, F32), (_P_GATE, F32), (_P_Z, F32), (_P_QB, BF16), (_P_KB, BF16),
              (_P_VB, BF16), (_P_SG, F32)]
    out_shape = [jax.ShapeDtypeStruct((t, s[1] - s[0]), dt) for s, dt in widths]
    out_specs = [pl.BlockSpec((tm, s[1] - s[0]), lambda i: (i, 0)) for s, _ in widths]
    return pl.pallas_call(
        _proj_body,
        grid=(t // tm,),
        in_specs=[pl.BlockSpec((tm, d), lambda i: (i, 0)), _resident((1, d)), _resident(w.shape)],
        out_specs=out_specs,
        out_shape=out_shape,
        compiler_params=_cparams(("parallel",)),
        name="proj",
    )(x, nw, w)


def _gdnprep_body(cur_ref, prev_ref, next_ref, cw_ref, gate_ref, alog_ref, dtb_ref,
                  qkv_ref, bg_ref, *, tm, seq_tiles):
    i = pl.program_id(0)
    first = jnp.bool_(False)
    last = jnp.bool_(False)
    for start, per_seq in seq_tiles:
        rel = i - start
        first = first | ((rel >= 0) & (rel % per_seq == 0))
        last = last | ((rel >= 0) & (rel % per_seq == per_seq - 1))
    prev = jnp.where(first, 0.0, prev_ref[...])
    nxt = jnp.where(last, 0.0, next_ref[...])
    ext = jnp.concatenate([prev, cur_ref[...], nxt], axis=0)
    pad = (GDN_CONV - 1) // 2
    acc = None
    for k in range(GDN_CONV):
        lo = HALO - pad + k
        term = ext[lo:lo + tm, :] * cw_ref[k:k + 1, :]
        acc = term if acc is None else acc + term
    y = acc * jax.nn.sigmoid(acc)
    for h in range(2 * GDN_HEADS):
        lo = h * GDN_DK
        yh = y[:, lo:lo + GDN_DK]
        nrm = yh * lax.rsqrt(jnp.sum(yh * yh, axis=-1, keepdims=True) + EPS)
        if h < GDN_HEADS:
            nrm = nrm * (GDN_DK ** -0.5)
        qkv_ref[:, lo:lo + GDN_DK] = nrm
    qkv_ref[:, 2 * GDN_QK:] = y[:, 2 * GDN_QK:]
    gl = gate_ref[...]
    beta = jax.nn.sigmoid(gl)
    xa = gl + dtb_ref[...]
    softplus = jnp.maximum(xa, 0.0) + jnp.log1p(jnp.exp(-jnp.abs(xa)))
    g = -jnp.exp(alog_ref[...]) * softplus
    lane = lax.broadcasted_iota(jnp.int32, gl.shape, 1)
    bg_ref[...] = jnp.where(lane < N_GATES, beta, g)


def _gdnprep(qkva, conv_w8, gates, alog_pad, dtb_pad, tm, seq_tiles):
    t, c = qkva.shape
    hb = tm // HALO
    nblk = t // HALO
    return pl.pallas_call(
        functools.partial(_gdnprep_body, tm=tm, seq_tiles=seq_tiles),
        grid=(t // tm,),
        in_specs=[
            pl.BlockSpec((tm, c), lambda i: (i, 0)),
            pl.BlockSpec((HALO, c), lambda i: (jnp.maximum(i * hb - 1, 0), 0)),
            pl.BlockSpec((HALO, c), lambda i: (jnp.minimum((i + 1) * hb, nblk - 1), 0)),
            _resident(conv_w8.shape),
            pl.BlockSpec((tm, V7X_LANES), lambda i: (i, 0)),
            _resident((1, V7X_LANES)),
            _resident((1, V7X_LANES)),
        ],
        out_specs=[pl.BlockSpec((tm, c), lambda i: (i, 0)),
                   pl.BlockSpec((tm, V7X_LANES), lambda i: (i, 0))],
        out_shape=[jax.ShapeDtypeStruct((t, c), F32),
                   jax.ShapeDtypeStruct((t, V7X_LANES), F32)],
        compiler_params=_cparams(("parallel",)),
        name="gdnprep",
    )(qkva, qkva, qkva, conv_w8, gates, alog_pad, dtb_pad)


def _gdn_body(qkv_f, bgc_f, bgr_f, qkv_b, bgc_b, bgr_b, *rest, n_chunks):
    of_ref, ob_ref, s_ref = rest[-3:]
    c64 = GDN_CHUNK

    @pl.when(pl.program_id(1) == 0)
    def _():
        s_ref[...] = jnp.zeros_like(s_ref)

    row = lax.broadcasted_iota(jnp.int32, (c64, c64), 0)
    col = lax.broadcasted_iota(jnp.int32, (c64, c64), 1)
    eye = (row == col).astype(F32)
    incl = (row >= col, row <= col)
    strict = (row > col, row < col)
    same_block = {}
    size = GDN_BASE
    while size <= c64:
        same_block[size] = (row // size) == (col // size)
        size *= 2

    refs = ((qkv_f, bgc_f, bgr_f, of_ref), (qkv_b, bgc_b, bgr_b, ob_ref))
    chains = [(d, h) for d in range(2) for h in range(GDN_HEADS)]

    def row0(c, d):
        return (c if d == 0 else n_chunks - 1 - c) * c64

    def prepare(c):
        gates = []
        for d in range(2):
            _, bgc_ref, bgr_ref, _ = refs[d]
            r0 = row0(c, d)
            gt = bgc_ref[r0:r0 + c64, :]
            gcs = _dot(incl[d].astype(F32), gt, HIGHEST)
            grs = _dot(bgr_ref[:, r0:r0 + c64], incl[1 - d].astype(F32), HIGHEST)
            g_last = gcs[c64 - 1:c64, :] if d == 0 else gcs[0:1, :]
            gates.append((gt, gcs, grs, jnp.exp(gcs), jnp.exp(g_last - gcs), jnp.exp(g_last)))
        q, k, k16, kb, vb, gam, egc_c, ekd_c, egl_c = ([] for _ in range(9))
        for d, h in chains:
            qkv_ref = refs[d][0]
            r0 = row0(c, d)
            gt, gcs, grs, egc, ekd, egl = gates[d]
            idx = d * GDN_HEADS + h
            gi = N_GATES + idx
            qq = qkv_ref[r0:r0 + c64, h * GDN_DK:(h + 1) * GDN_DK]
            kk_ = qkv_ref[r0:r0 + c64, GDN_QK + h * GDN_DK:GDN_QK + (h + 1) * GDN_DK]
            vv = qkv_ref[r0:r0 + c64, 2 * GDN_QK + h * GDN_DV:2 * GDN_QK + (h + 1) * GDN_DV]
            beta = gt[:, idx:idx + 1]
            diff = gcs[:, gi:gi + 1] - grs[gi:gi + 1, :]
            gam.append(jnp.where(incl[d], jnp.exp(jnp.where(incl[d], diff, 0.0)), 0.0))
            q.append(qq)
            k.append(kk_)
            k16.append(kk_.astype(BF16))
            kb.append(kk_ * beta)
            vb.append(vv * beta)
            egc_c.append(egc[:, gi:gi + 1])
            ekd_c.append(ekd[:, gi:gi + 1])
            egl_c.append(egl[:, gi:gi + 1])
        n = len(chains)
        kk = [_dot_nt(kb[i].astype(BF16), k16[i]) for i in range(n)]
        qk = [_dot_nt(q[i].astype(BF16), k16[i]) for i in range(n)]
        nn = [jnp.where(strict[chains[i][0]], kk[i] * gam[i], 0.0) for i in range(n)]
        m = [-jnp.where(same_block[GDN_BASE], nn[i], 0.0) for i in range(n)]
        x = [eye + m[i] for i in range(n)]
        for _ in range(2):
            m = [_dot_split(m[i], m[i]) for i in range(n)]
            x = [x[i] + _dot_split(x[i], m[i]) for i in range(n)]
        size = GDN_BASE
        while size < c64:
            e = [jnp.where(same_block[2 * size] & ~same_block[size], nn[i], 0.0)
                 for i in range(n)]
            ex = [_dot_split(e[i], x[i]) for i in range(n)]
            x = [x[i] - _dot_split(x[i], ex[i]) for i in range(n)]
            size *= 2
        rhs = [jnp.concatenate([vb[i], kb[i] * egc_c[i]], axis=1) for i in range(n)]
        sol = [_dot_split(x[i], rhs[i]) for i in range(n)]
        return dict(
            u=[sol[i][:, :GDN_DV] for i in range(n)],
            w=[sol[i][:, GDN_DV:].astype(BF16) for i in range(n)],
            aqk=[(qk[i] * gam[i]).astype(BF16) for i in range(n)],
            qd=[(q[i] * egc_c[i]).astype(BF16) for i in range(n)],
            kd=[(k[i] * ekd_c[i]).astype(BF16) for i in range(n)],
            egl=egl_c)

    def advance(c, pre):
        n = len(chains)
        s = [s_ref[i] for i in range(n)]
        s16 = [s[i].astype(BF16) for i in range(n)]
        ws = [_dot(pre["w"][i], s16[i]) for i in range(n)]
        qs = [_dot(pre["qd"][i], s16[i]) for i in range(n)]
        vn16 = [(pre["u"][i] - ws[i]).astype(BF16) for i in range(n)]
        av = [_dot(pre["aqk"][i], vn16[i]) for i in range(n)]
        kv = [_dot_tn(pre["kd"][i], vn16[i]) for i in range(n)]
        for i, (d, h) in enumerate(chains):
            r0 = row0(c, d)
            s_ref[i] = s[i] * pre["egl"][i] + kv[i]
            refs[d][3][r0:r0 + c64, h * GDN_DV:(h + 1) * GDN_DV] = qs[i] + av[i]

    pre = prepare(0)
    for c in range(n_chunks):
        nxt = prepare(c + 1) if c + 1 < n_chunks else None
        advance(c, pre)
        pre = nxt


def _gdn(qkvn, bg, bg_rows, row_off, batch, seqlen, ts, prev):
    t, c = qkvn.shape
    ns = seqlen // ts
    off = row_off // ts
    fwd = lambda b, s: (off + b * ns + s, 0)
    bwd = lambda b, s: (off + b * ns + ns - 1 - s, 0)
    fwd_r = lambda b, s: (0, off + b * ns + s)
    bwd_r = lambda b, s: (0, off + b * ns + ns - 1 - s)
    nr = bg_rows.shape[0]
    out = jax.ShapeDtypeStruct((t, GDN_V), F32)
    in_specs = [
        pl.BlockSpec((ts, c), fwd), pl.BlockSpec((ts, V7X_LANES), fwd),
        pl.BlockSpec((nr, ts), fwd_r),
        pl.BlockSpec((ts, c), bwd), pl.BlockSpec((ts, V7X_LANES), bwd),
        pl.BlockSpec((nr, ts), bwd_r),
    ]
    args = [qkvn, bg, bg_rows, qkvn, bg, bg_rows]
    aliases = {}
    if prev is not None:
        aliases = {len(args): 0, len(args) + 1: 1}
        in_specs += [pl.BlockSpec(memory_space=pl.ANY)] * 2
        args += list(prev)
    return pl.pallas_call(
        functools.partial(_gdn_body, n_chunks=ts // GDN_CHUNK),
        grid=(batch, ns),
        in_specs=in_specs,
        out_specs=[pl.BlockSpec((ts, GDN_V), fwd), pl.BlockSpec((ts, GDN_V), bwd)],
        out_shape=[out, out],
        scratch_shapes=[pltpu.VMEM((2 * GDN_HEADS, GDN_DK, GDN_DV), F32)],
        input_output_aliases=aliases,
        compiler_params=_cparams(("parallel", "arbitrary")),
        name="gdn",
    )(*args)


def _attn_body(far_ref, q_ref, k_ref, vt_ref, band_ref, lam_ref, nw_ref, *rest,
               lam_init, tk, nk):
    o_ref, qz_ref, sf_ref, mxf_ref, sb_ref, mxb_ref, m_ref, acc_ref = rest[-8:]
    far_buf = (sf_ref, mxf_ref)
    band_buf = (sb_ref, mxb_ref)
    h = pl.program_id(1)
    qi = pl.program_id(2)
    tq = q_ref.shape[0]
    sub = V7X_SUBLANES

    m_ref[...] = jnp.full_like(m_ref, -1e30)
    acc_ref[...] = jnp.zeros_like(acc_ref)
    q = q_ref[...]
    lane = lax.broadcasted_iota(jnp.int32, q.shape, 1)
    zero = jnp.zeros_like(q)
    qz_ref[0] = jnp.where(lane < DIFF_DH, q, zero)
    qz_ref[1] = jnp.where(lane >= DIFF_DH, q, zero)

    def scores(kt, buf, slot, bias_tile):
        s_ref, mx_ref = buf
        k = k_ref[pl.ds(pl.multiple_of(kt * tk, tk), tk), :]
        for mp in range(2):
            s = _dot_nt(k, qz_ref[mp])
            if bias_tile is not None:
                s = s + bias_tile
            s_ref[slot, mp] = s
            mx_ref[slot, mp] = jnp.max(s.reshape(tk // sub, sub, tq), axis=0)

    def accumulate(kt, buf, slot, c):
        s_ref, mx_ref = buf
        vt = vt_ref[:, pl.ds(pl.multiple_of(kt * tk, tk), tk)]
        for mp in range(2):
            m_cur = jnp.max(mx_ref[slot, mp], axis=0, keepdims=True) + c
            m_prev = m_ref[mp]
            m_new = jnp.maximum(m_prev, m_cur)
            alpha = jnp.exp2(m_prev - m_new)
            m_ref[mp] = m_new
            p = jnp.exp2(s_ref[slot, mp] - (m_new - c)).astype(BF16)
            acc_ref[mp] = alpha * acc_ref[mp] + _dot(vt, p)

    n_left = jnp.maximum(qi - 1, 0)
    right0 = jnp.minimum(qi + 2, nk)
    n_far = n_left + (nk - right0)

    def far_tile(f):
        return jnp.where(f < n_left, f, right0 + (f - n_left))

    def far_const(f):
        return jnp.where(f < n_left, far_ref[h, 0], far_ref[h, 1])

    band = [(qi, 1), (qi - 1, 0), (qi + 1, 2)]

    def band_scores(j):
        kt, dd = band[j]
        scores(jnp.clip(kt, 0, nk - 1), band_buf, j, band_ref[dd])

    def band_accumulate(j):
        kt, _ = band[j]
        valid = (kt >= 0) & (kt < nk)
        accumulate(jnp.clip(kt, 0, nk - 1), band_buf, j, jnp.where(valid, 0.0, -1e30))

    if nk >= 4:
        scores(far_tile(0), far_buf, 0, None)

        def far_step(f, slot):
            scores(far_tile(f + 1), far_buf, 1 - slot, None)
            accumulate(far_tile(f), far_buf, slot, far_const(f))

        n_steps = n_far - 1

        def far_trip(g, carry):
            for u in range(FAR_UNROLL):
                far_step(FAR_UNROLL * g + u, u % 2)
            return carry

        lax.fori_loop(0, n_steps // FAR_UNROLL, far_trip, 0)
        done = (n_steps // FAR_UNROLL) * FAR_UNROLL
        size = FAR_UNROLL // 2
        while size >= 1:
            taken = (n_steps % (2 * size)) >= size

            @pl.when(taken)
            def _(done=done, size=size):
                for u in range(size):
                    far_step(done + u, u % 2)

            done = done + jnp.where(taken, size, 0)
            size //= 2

        band_scores(0)
        accumulate(far_tile(n_far - 1), far_buf, (n_far - 1) % 2, far_const(n_far - 1))
        band_scores(1)
        band_accumulate(0)
        band_scores(2)
        band_accumulate(1)
        band_accumulate(2)
    else:
        for j in range(3):
            band_scores(j)
            band_accumulate(j)

    lam = lam_ref[...]
    lam_full = (jnp.exp(jnp.sum(lam[0:1] * lam[1:2], keepdims=True))
                - jnp.exp(jnp.sum(lam[2:3] * lam[3:4], keepdims=True)) + lam_init)
    a0 = acc_ref[0]
    a1 = acc_ref[1]
    o = (a0[:DIFF_DV] / a0[DIFF_DV:DIFF_DV + 1]
         - lam_full * (a1[:DIFF_DV] / a1[DIFF_DV:DIFF_DV + 1]))
    o = o * lax.rsqrt(jnp.mean(o * o, axis=0, keepdims=True) + EPS)
    o = o * (nw_ref[...] * (1.0 - lam_init))
    o_ref[...] = o.T


def _attn(qb, kb, vbt, band, far, lam, nw_col, row_off, batch, seqlen, tq, lam_init, prev):
    t = qb.shape[0]
    nq = seqlen // tq
    off = row_off // tq
    off_seq = row_off // seqlen
    assert row_off % seqlen == 0
    qmap = lambda b, h, qi: (off + b * nq + qi, h)
    in_specs = [
        pl.BlockSpec(memory_space=pltpu.SMEM),
        pl.BlockSpec((tq, DIFF_DV), qmap),
        pl.BlockSpec((seqlen, DIFF_DV), lambda b, h, qi: (off_seq + b, h)),
        pl.BlockSpec((VT_ROWS, seqlen), lambda b, h, qi: (h, off_seq + b)),
        pl.BlockSpec((None, 3, tq, tq), lambda b, h, qi: (h, 0, 0, 0)),
        pl.BlockSpec(lam.shape, lambda b, h, qi: (0, 0)),
        pl.BlockSpec((DIFF_DV, 1), lambda b, h, qi: (0, 0)),
    ]
    args = [far, qb, kb, vbt, band, lam, nw_col]
    aliases = {}
    if prev is not None:
        aliases = {len(args): 0}
        in_specs.append(pl.BlockSpec(memory_space=pl.ANY))
        args.append(prev)
    return pl.pallas_call(
        functools.partial(_attn_body, lam_init=lam_init, tk=tq, nk=nq),
        grid=(batch, DIFF_HEADS, nq),
        in_specs=in_specs,
        input_output_aliases=aliases,
        out_specs=pl.BlockSpec((tq, DIFF_DV), qmap),
        out_shape=jax.ShapeDtypeStruct((t, DIFF_VW), F32),
        scratch_shapes=[pltpu.VMEM((2, tq, DIFF_DV), BF16),
                        pltpu.VMEM((2, 2, tq, tq), F32),
                        pltpu.VMEM((2, 2, V7X_SUBLANES, tq), F32),
                        pltpu.VMEM((3, 2, tq, tq), F32),
                        pltpu.VMEM((3, 2, V7X_SUBLANES, tq), F32),
                        pltpu.VMEM((2, 1, tq), F32),
                        pltpu.VMEM((2, VT_ROWS, tq), F32)],
        compiler_params=_cparams(("parallel", "parallel", "parallel")),
        name="diffattn",
    )(*args)


def _memkv_body(m_ref, nw_ref, w_ref, o_ref):
    h = _rms(m_ref[...], nw_ref[...]).astype(BF16)
    o_ref[...] = _dot(h, w_ref[...]).astype(BF16)


def _memkv(mem, nw, wkv):
    nb, nm, d = mem.shape
    return pl.pallas_call(
        _memkv_body,
        grid=(nb,),
        in_specs=[pl.BlockSpec((None, nm, d), lambda b: (b, 0, 0)), _resident((1, d)),
                  _resident(wkv.shape)],
        out_specs=pl.BlockSpec((None, nm, wkv.shape[1]), lambda b: (b, 0, 0)),
        out_shape=jax.ShapeDtypeStruct((nb, nm, wkv.shape[1]), BF16),
        compiler_params=_cparams(("parallel",)),
        name="memkv",
    )(mem, nw, wkv)


def _mergex_body(x_ref, of_ref, ob_ref, z_ref, oattn_ref, sg_ref, kv_ref, gn_ref, wua_ref,
                 wub_ref, wout_ref, xn_ref, wq_ref, wo_ref, o_ref):
    o = of_ref[...] + ob_ref[...]
    z = z_ref[...]
    gn = gn_ref[...]
    heads = []
    for h in range(GDN_HEADS):
        sl = slice(h * GDN_DV, (h + 1) * GDN_DV)
        zh = z[:, sl]
        heads.append(_rms(o[:, sl], gn) * (zh * jax.nn.sigmoid(zh)))
    oa = jnp.concatenate(heads, axis=1).astype(BF16)
    ya = _dot(oa, wua_ref[...])
    yb = _dot(oattn_ref[...].astype(BF16), wub_ref[...])
    sg = sg_ref[...]
    merged = sg[:, :D_MODEL] * ya + sg[:, D_MODEL:] * yb
    x = x_ref[...] + _dot(merged.astype(BF16), wout_ref[...])
    hq = _rms(x, xn_ref[...]).astype(BF16)
    q = _dot(hq, wq_ref[...]) * (X_DH ** -0.5)
    kv = kv_ref[...]
    outs = []
    for h in range(X_HEADS):
        sl = slice(h * X_DH, (h + 1) * X_DH)
        kh = kv[:, sl]
        vh = kv[:, X_HEADS * X_DH + h * X_DH:X_HEADS * X_DH + (h + 1) * X_DH]
        s = _dot_nt(q[:, sl].astype(BF16), kh)
        s = s - jnp.max(s, axis=-1, keepdims=True)
        p = jnp.exp(s)
        p = p / jnp.sum(p, axis=-1, keepdims=True)
        outs.append(_dot(p.astype(BF16), vh))
    ox = jnp.concatenate(outs, axis=1).astype(BF16)
    o_ref[...] = x + _dot(ox, wo_ref[...])


def _mergex(x, o_f, o_b, z, oattn, sg, kv, gn, wua, wub, wout, xn, wq, wo, tm, tile_batch):
    t, d = x.shape
    rows = lambda width: pl.BlockSpec((tm, width), lambda i: (i, 0))
    return pl.pallas_call(
        _mergex_body,
        grid=(t // tm,),
        in_specs=[rows(d), rows(GDN_V), rows(GDN_V), rows(GDN_V), rows(DIFF_VW), rows(2 * d),
                  pl.BlockSpec((None,) + kv.shape[1:], lambda i: (tile_batch(i), 0, 0)),
                  _resident(gn.shape), _resident(wua.shape), _resident(wub.shape),
                  _resident(wout.shape), _resident(xn.shape), _resident(wq.shape),
                  _resident(wo.shape)],
        out_specs=rows(d),
        out_shape=jax.ShapeDtypeStruct((t, d), F32),
        compiler_params=_cparams(("parallel",)),
        name="mergex",
    )(x, o_f, o_b, z, oattn, sg, kv, gn, wua, wub, wout, xn, wq, wo)


def _rel_bucket(rel):
    nb = N_BUCKETS // 2
    ret = jnp.where(rel > 0, nb, 0)
    n = jnp.abs(rel)
    max_exact = nb // 2
    nf = jnp.maximum(n, 1).astype(F32)
    large = max_exact + (jnp.log(nf / max_exact) / math.log(MAX_DISTANCE / max_exact)
                         * (nb - max_exact)).astype(jnp.int32)
    large = jnp.minimum(large, nb - 1)
    return ret + jnp.where(n < max_exact, n, large)


def _toeplitz_body(w_ref, o_ref):
    tq = o_ref.shape[-1]
    x = jnp.broadcast_to(w_ref[...], (tq, 2 * tq))
    o_ref[...] = pltpu.roll(x, 0, 1, stride=1, stride_axis=0)[:, :tq]


def _bias_tables(rel_bias, tq):
    assert tq >= MAX_DISTANCE
    rel = jnp.arange(-(2 * tq - 1), 2 * tq, dtype=jnp.int32)
    by_rel = (rel_bias[_rel_bucket(rel)].astype(F32) * LOG2E).T
    nh = by_rel.shape[0]
    gens = []
    for dd in range(3):
        lo = dd * tq
        gens.append(jnp.concatenate(
            [by_rel[:, lo:lo + tq][:, ::-1], jnp.zeros((nh, 1), F32),
             by_rel[:, lo + tq:lo + 2 * tq - 1][:, ::-1]], axis=1))
    gen = jnp.stack(gens, axis=1)[:, :, None, :]
    band = pl.pallas_call(
        _toeplitz_body,
        grid=(nh, 3),
        in_specs=[pl.BlockSpec((None, None, 1, 2 * tq), lambda h, d: (h, d, 0, 0))],
        out_specs=pl.BlockSpec((None, None, tq, tq), lambda h, d: (h, d, 0, 0)),
        out_shape=jax.ShapeDtypeStruct((nh, 3, tq, tq), F32),
        compiler_params=_cparams(("parallel", "parallel")),
        name="toeplitz",
    )(gen)
    far = jnp.stack([by_rel[:, 0], by_rel[:, -1]], axis=1)
    return band, far


def _pick(limit, n):
    tile = limit
    while n % tile:
        tile //= 2
    return tile


def _encode(xs, mems, p, tile_limits=None):
    lim = dict(ffn=512, proj=256, prep=256, gdn=512, attn=512, mergex=512)
    if tile_limits:
        lim.update(tile_limits)
    depth = p["w_in"].shape[0]
    d = D_MODEL
    groups = []
    row = 0
    bat = 0
    for x in xs:
        b, l, _ = x.shape
        groups.append(dict(b=b, l=l, row=row, bat=bat))
        row += b * l
        bat += b
    t_all = row
    seqlens = [g["l"] for g in groups]
    common = functools.reduce(math.gcd, seqlens)
    tiles = {k: _pick(v, common) for k, v in lim.items()}

    x = jnp.concatenate([xx.reshape(-1, d) for xx in xs], axis=0)
    mem = jnp.concatenate(mems, axis=0)

    def tile_batch(i):
        r0 = i * tiles["mergex"]
        bidx = 0
        for g in groups:
            bidx = jnp.where(r0 >= g["row"], g["bat"] + (r0 - g["row"]) // g["l"], bidx)
        return bidx

    seq_tiles = tuple((g["row"] // tiles["prep"], g["l"] // tiles["prep"]) for g in groups)
    band, far = _bias_tables(p["rel_bias"], tiles["attn"])

    def ffn_weights(prefix, i):
        return (p[prefix + "_norm"][i].reshape(1, d), p[prefix + "_w_gate"][i].astype(BF16),
                p[prefix + "_w_up"][i].astype(BF16), p[prefix + "_w_down"][i].astype(BF16))

    for i in range(depth):
        x = _ffn(x, *ffn_weights("ffn1", i), None, tiles["ffn"])

        w_in = p["w_in"][i]
        o_beta = 3 * GDN_QK
        o_z = o_beta + 2 * N_GATES
        o_qb = o_z + GDN_V
        gate_w = jnp.pad(w_in[:, o_beta:o_z], ((0, 0), (0, V7X_LANES - 2 * N_GATES)))
        w_perm = jnp.concatenate(
            [w_in[:, :o_beta], gate_w, w_in[:, o_z:o_qb],
             w_in[:, o_qb:o_qb + DIFF_QK] * (DIFF_DH ** -0.5 * LOG2E),
             w_in[:, o_qb + DIFF_QK:]],
            axis=1).astype(BF16)
        assert w_perm.shape[1] == _P_COLS
        qkva, gates, z, qb, kb, vb, sg = _proj(x, p["mix_norm"][i].reshape(1, d), w_perm,
                                               tiles["proj"])

        conv_w8 = jnp.pad(p["conv_w"][i], ((0, HALO - GDN_CONV), (0, 0)))
        lane_pad = (N_GATES, V7X_LANES - 2 * N_GATES)
        alog_pad = jnp.pad(p["gdn_a_log"][i].reshape(-1), lane_pad).reshape(1, V7X_LANES)
        dtb_pad = jnp.pad(p["gdn_dt_bias"][i].reshape(-1), lane_pad).reshape(1, V7X_LANES)
        qkvn, bg = _gdnprep(qkva, conv_w8, gates, alog_pad, dtb_pad, tiles["prep"], seq_tiles)
        bg_rows = bg[:, :2 * N_GATES].T

        lam_init = 0.8 - 0.6 * math.exp(-0.3 * i)
        o_fb = o_attn = None
        vbt = vb.T.reshape(DIFF_HEADS, DIFF_DV, t_all)
        ones_pad = jnp.zeros((DIFF_HEADS, VT_ROWS - DIFF_DV, t_all), BF16).at[:, 0].set(1.0)
        vbt = jnp.concatenate([vbt, ones_pad], axis=1).reshape(DIFF_HEADS * VT_ROWS, t_all)
        for g in groups:
            o_fb = _gdn(qkvn, bg, bg_rows, g["row"], g["b"], g["l"], tiles["gdn"], o_fb)
            o_attn = _attn(qb, kb, vbt, band, far, p["diff_lambda"][i],
                           p["diff_norm"][i].reshape(DIFF_DV, 1), g["row"], g["b"], g["l"],
                           tiles["attn"], lam_init, o_attn)
        o_f, o_b = o_fb

        kv = _memkv(mem, p["mem_norm"][i].reshape(1, d), p["xattn_wkv"][i].astype(BF16))
        x = _mergex(x, o_f, o_b, z, o_attn, sg, kv, p["gdn_norm"][i].reshape(1, GDN_DV),
                    p["w_up_a"][i].astype(BF16), p["w_up_b"][i].astype(BF16),
                    p["w_out"][i].astype(BF16), p["xattn_norm"][i].reshape(1, d),
                    p["xattn_wq"][i].astype(BF16), p["xattn_wo"][i].astype(BF16),
                    tiles["mergex"], tile_batch)

        final_w = p["final_norm"].reshape(1, d) if i == depth - 1 else None
        x = _ffn(x, *ffn_weights("ffn2", i), final_w, tiles["ffn"])

    return tuple(x[g["row"]:g["row"] + g["b"] * g["l"]].reshape(g["b"], g["l"], d)
                 for g in groups)


def kernel(x_prompt, x_sample, mem_prompt, mem_sample, ffn1_norm, ffn1_w_gate, ffn1_w_up, ffn1_w_down, mix_norm, w_in, conv_w, gdn_a_log, gdn_dt_bias, gdn_norm, w_up_a, diff_lambda, diff_norm, w_up_b, w_out, rel_bias, xattn_norm, mem_norm, xattn_wq, xattn_wkv, xattn_wo, ffn2_norm, ffn2_w_gate, ffn2_w_up, ffn2_w_down, final_norm):
    params = dict(
        ffn1_norm=ffn1_norm, ffn1_w_gate=ffn1_w_gate, ffn1_w_up=ffn1_w_up,
        ffn1_w_down=ffn1_w_down, mix_norm=mix_norm, w_in=w_in, conv_w=conv_w,
        gdn_a_log=gdn_a_log, gdn_dt_bias=gdn_dt_bias, gdn_norm=gdn_norm, w_up_a=w_up_a,
        diff_lambda=diff_lambda, diff_norm=diff_norm, w_up_b=w_up_b, w_out=w_out,
        rel_bias=rel_bias, xattn_norm=xattn_norm, mem_norm=mem_norm, xattn_wq=xattn_wq,
        xattn_wkv=xattn_wkv, xattn_wo=xattn_wo, ffn2_norm=ffn2_norm,
        ffn2_w_gate=ffn2_w_gate, ffn2_w_up=ffn2_w_up, ffn2_w_down=ffn2_w_down,
        final_norm=final_norm)
    y_prompt, y_sample = _encode((x_prompt, x_sample), (mem_prompt, mem_sample), params)
    return (y_prompt, y_sample)
```

```python
import functools
import math

import jax
import jax.numpy as jnp
from jax import lax
from jax.experimental import pallas as pl
from jax.experimental.pallas import tpu as pltpu

F32 = jnp.float32
BF16 = jnp.bfloat16
HIGHEST = lax.Precision.HIGHEST

EPS = 1e-6
LOG2E = math.log2(math.e)
D_MODEL = 1024
N_MEM = 256
GDN_HEADS = 4
GDN_DK = 128
GDN_DV = 128
GDN_CONV = 5
GDN_CHUNK = 64
GDN_BASE = 8
DIFF_HEADS = 8
DIFF_DH = 64
DIFF_DV = 2 * DIFF_DH
N_BUCKETS = 32
MAX_DISTANCE = 128
X_HEADS = 4
X_DH = 128
D_FF = 2816
GDN_QK = GDN_HEADS * GDN_DK
GDN_V = GDN_HEADS * GDN_DV
DIFF_QK = DIFF_HEADS * 2 * DIFF_DH
DIFF_VW = DIFF_HEADS * DIFF_DV
N_GATES = 2 * GDN_HEADS
VT_ROWS = DIFF_DV + 16

V7X_LANES = 128
V7X_SUBLANES = 8
V7X_VMEM_LIMIT = 56 * 1024 * 1024

FAR_UNROLL = 8
HALO = V7X_SUBLANES


def _cparams(sem):
    return pltpu.CompilerParams(dimension_semantics=sem, vmem_limit_bytes=V7X_VMEM_LIMIT)


def _dot(a, b, precision=None):
    return jnp.dot(a, b, preferred_element_type=F32, precision=precision)


def _dot_nt(a, b, precision=None):
    return lax.dot_general(a, b, (((1,), (1,)), ((), ())),
                           preferred_element_type=F32, precision=precision)


def _dot_tn(a, b, precision=None):
    return lax.dot_general(a, b, (((0,), (0,)), ((), ())),
                           preferred_element_type=F32, precision=precision)


def _dot_split(a, b):
    a_hi = a.astype(BF16)
    b_hi = b.astype(BF16)
    a_lo = (a - a_hi.astype(F32)).astype(BF16)
    b_lo = (b - b_hi.astype(F32)).astype(BF16)
    return _dot(a_hi, b_hi) + (_dot(a_hi, b_lo) + _dot(a_lo, b_hi))


def _rms(x, w):
    return x * lax.rsqrt(jnp.mean(x * x, axis=-1, keepdims=True) + EPS) * w


def _resident(shape):
    nd = len(shape)
    return pl.BlockSpec(shape, lambda *_: (0,) * nd, pipeline_mode=pl.Buffered(1))


def _ffn_body(x_ref, nw_ref, wg_ref, wu_ref, wd_ref, *rest, final):
    o_ref = rest[-1]
    x = x_ref[...]
    h = _rms(x, nw_ref[...]).astype(BF16)
    g = _dot(h, wg_ref[...])
    u = _dot(h, wu_ref[...])
    a = (g * jax.nn.sigmoid(g) * u).astype(BF16)
    y = x + 0.5 * _dot(a, wd_ref[...])
    if final:
        y = _rms(y, rest[0][...])
    o_ref[...] = y


def _ffn(x, nw, wg, wu, wd, final_w, tm):
    t, d = x.shape
    row = pl.BlockSpec((tm, d), lambda i: (i, 0))
    in_specs = [row, _resident((1, d)), _resident(wg.shape), _resident(wu.shape),
                _resident(wd.shape)]
    args = [x, nw, wg, wu, wd]
    if final_w is not None:
        in_specs.append(_resident((1, d)))
        args.append(final_w)
    return pl.pallas_call(
        functools.partial(_ffn_body, final=final_w is not None),
        grid=(t // tm,),
        in_specs=in_specs,
        out_specs=row,
        out_shape=jax.ShapeDtypeStruct((t, d), F32),
        compiler_params=_cparams(("parallel",)),
        name="ffn",
    )(*args)


_P_QKVA = (0, 3 * GDN_QK)
_P_GATE = (_P_QKVA[1], _P_QKVA[1] + V7X_LANES)
_P_Z = (_P_GATE[1], _P_GATE[1] + GDN_V)
_P_QB = (_P_Z[1], _P_Z[1] + DIFF_QK)
_P_KB = (_P_QB[1], _P_QB[1] + DIFF_QK)
_P_VB = (_P_KB[1], _P_KB[1] + DIFF_VW)
_P_SG = (_P_VB[1], _P_VB[1] + 2 * D_MODEL)
_P_COLS = _P_SG[1]


def _proj_body(x_ref, nw_ref, w_ref, qkva_ref, gate_ref, z_ref, qb_ref, kb_ref, vb_ref, sg_ref):
    u = _rms(x_ref[...], nw_ref[...]).astype(BF16)

    def seg(span):
        return _dot(u, w_ref[:, span[0]:span[1]])

    qkva_ref[...] = seg(_P_QKVA)
    gate_ref[...] = seg(_P_GATE)
    z_ref[...] = seg(_P_Z)
    qb_ref[...] = seg(_P_QB).astype(BF16)
    kb_ref[...] = seg(_P_KB).astype(BF16)
    vb_ref[...] = seg(_P_VB).astype(BF16)
    sg_ref[...] = jax.nn.sigmoid(seg(_P_SG))


def _proj(x, nw, w, tm):
    t, d = x.shape
    widths = [(_P_QKVA, F32), (_P_GATE, F32), (_P_Z, F32), (_P_QB, BF16), (_P_KB, BF16),
              (_P_VB, BF16), (_P_SG, F32)]
    out_shape = [jax.ShapeDtypeStruct((t, s[1] - s[0]), dt) for s, dt in widths]
    out_specs = [pl.BlockSpec((tm, s[1] - s[0]), lambda i: (i, 0)) for s, _ in widths]
    return pl.pallas_call(
        _proj_body,
        grid=(t // tm,),
        in_specs=[pl.BlockSpec((tm, d), lambda i: (i, 0)), _resident((1, d)), _resident(w.shape)],
        out_specs=out_specs,
        out_shape=out_shape,
        compiler_params=_cparams(("parallel",)),
        name="proj",
    )(x, nw, w)


def _gdnprep_body(cur_ref, prev_ref, next_ref, cw_ref, gate_ref, alog_ref, dtb_ref,
                  qkv_ref, bg_ref, *, tm, seq_tiles):
    i = pl.program_id(0)
    first = jnp.bool_(False)
    last = jnp.bool_(False)
    for start, per_seq in seq_tiles:
        rel = i - start
        first = first | ((rel >= 0) & (rel % per_seq == 0))
        last = last | ((rel >= 0) & (rel % per_seq == per_seq - 1))
    prev = jnp.where(first, 0.0, prev_ref[...])
    nxt = jnp.where(last, 0.0, next_ref[...])
    ext = jnp.concatenate([prev, cur_ref[...], nxt], axis=0)
    pad = (GDN_CONV - 1) // 2
    acc = None
    for k in range(GDN_CONV):
        lo = HALO - pad + k
        term = ext[lo:lo + tm, :] * cw_ref[k:k + 1, :]
        acc = term if acc is None else acc + term
    y = acc * jax.nn.sigmoid(acc)
    for h in range(2 * GDN_HEADS):
        lo = h * GDN_DK
        yh = y[:, lo:lo + GDN_DK]
        nrm = yh * lax.rsqrt(jnp.sum(yh * yh, axis=-1, keepdims=True) + EPS)
        if h < GDN_HEADS:
            nrm = nrm * (GDN_DK ** -0.5)
        qkv_ref[:, lo:lo + GDN_DK] = nrm
    qkv_ref[:, 2 * GDN_QK:] = y[:, 2 * GDN_QK:]
    gl = gate_ref[...]
    beta = jax.nn.sigmoid(gl)
    xa = gl + dtb_ref[...]
    softplus = jnp.maximum(xa, 0.0) + jnp.log1p(jnp.exp(-jnp.abs(xa)))
    g = -jnp.exp(alog_ref[...]) * softplus
    lane = lax.broadcasted_iota(jnp.int32, gl.shape, 1)
    bg_ref[...] = jnp.where(lane < N_GATES, beta, g)


def _gdnprep(qkva, conv_w8, gates, alog_pad, dtb_pad, tm, seq_tiles):
    t, c = qkva.shape
    hb = tm // HALO
    nblk = t // HALO
    return pl.pallas_call(
        functools.partial(_gdnprep_body, tm=tm, seq_tiles=seq_tiles),
        grid=(t // tm,),
        in_specs=[
            pl.BlockSpec((tm, c), lambda i: (i, 0)),
            pl.BlockSpec((HALO, c), lambda i: (jnp.maximum(i * hb - 1, 0), 0)),
            pl.BlockSpec((HALO, c), lambda i: (jnp.minimum((i + 1) * hb, nblk - 1), 0)),
            _resident(conv_w8.shape),
            pl.BlockSpec((tm, V7X_LANES), lambda i: (i, 0)),
            _resident((1, V7X_LANES)),
            _resident((1, V7X_LANES)),
        ],
        out_specs=[pl.BlockSpec((tm, c), lambda i: (i, 0)),
                   pl.BlockSpec((tm, V7X_LANES), lambda i: (i, 0))],
        out_shape=[jax.ShapeDtypeStruct((t, c), F32),
                   jax.ShapeDtypeStruct((t, V7X_LANES), F32)],
        compiler_params=_cparams(("parallel",)),
        name="gdnprep",
    )(qkva, qkva, qkva, conv_w8, gates, alog_pad, dtb_pad)


def _gdn_body(qkv_f, bgc_f, bgr_f, qkv_b, bgc_b, bgr_b, of_ref, ob_ref, s_ref, *, n_chunks):
    c64 = GDN_CHUNK

    @pl.when(pl.program_id(1) == 0)
    def _():
        s_ref[...] = jnp.zeros_like(s_ref)

    row = lax.broadcasted_iota(jnp.int32, (c64, c64), 0)
    col = lax.broadcasted_iota(jnp.int32, (c64, c64), 1)
    eye = (row == col).astype(F32)
    incl = (row >= col, row <= col)
    strict = (row > col, row < col)
    same_block = {}
    size = GDN_BASE
    while size <= c64:
        same_block[size] = (row // size) == (col // size)
        size *= 2

    refs = ((qkv_f, bgc_f, bgr_f, of_ref), (qkv_b, bgc_b, bgr_b, ob_ref))
    chains = [(d, h) for d in range(2) for h in range(GDN_HEADS)]

    def row0(c, d):
        return (c if d == 0 else n_chunks - 1 - c) * c64

    def prepare(c):
        gates = []
        for d in range(2):
            _, bgc_ref, bgr_ref, _ = refs[d]
            r0 = row0(c, d)
            gt = bgc_ref[r0:r0 + c64, :]
            gcs = _dot(incl[d].astype(F32), gt, HIGHEST)
            grs = _dot(bgr_ref[:, r0:r0 + c64], incl[1 - d].astype(F32), HIGHEST)
            g_last = gcs[c64 - 1:c64, :] if d == 0 else gcs[0:1, :]
            gates.append((gt, gcs, grs, jnp.exp(gcs), jnp.exp(g_last - gcs), jnp.exp(g_last)))
        q, k, k16, kb, vb, gam, egc_c, ekd_c, egl_c = ([] for _ in range(9))
        for d, h in chains:
            qkv_ref = refs[d][0]
            r0 = row0(c, d)
            gt, gcs, grs, egc, ekd, egl = gates[d]
            idx = d * GDN_HEADS + h
            gi = N_GATES + idx
            qq = qkv_ref[r0:r0 + c64, h * GDN_DK:(h + 1) * GDN_DK]
            kk_ = qkv_ref[r0:r0 + c64, GDN_QK + h * GDN_DK:GDN_QK + (h + 1) * GDN_DK]
            vv = qkv_ref[r0:r0 + c64, 2 * GDN_QK + h * GDN_DV:2 * GDN_QK + (h + 1) * GDN_DV]
            beta = gt[:, idx:idx + 1]
            diff = gcs[:, gi:gi + 1] - grs[gi:gi + 1, :]
            gam.append(jnp.where(incl[d], jnp.exp(jnp.where(incl[d], diff, 0.0)), 0.0))
            q.append(qq)
            k.append(kk_)
            k16.append(kk_.astype(BF16))
            kb.append(kk_ * beta)
            vb.append(vv * beta)
            egc_c.append(egc[:, gi:gi + 1])
            ekd_c.append(ekd[:, gi:gi + 1])
            egl_c.append(egl[:, gi:gi + 1])
        n = len(chains)
        kk = [_dot_nt(kb[i].astype(BF16), k16[i]) for i in range(n)]
        qk = [_dot_nt(q[i].astype(BF16), k16[i]) for i in range(n)]
        nn = [jnp.where(strict[chains[i][0]], kk[i] * gam[i], 0.0) for i in range(n)]
        m = [-jnp.where(same_block[GDN_BASE], nn[i], 0.0) for i in range(n)]
        x = [eye + m[i] for i in range(n)]
        for _ in range(2):
            m = [_dot_split(m[i], m[i]) for i in range(n)]
            x = [x[i] + _dot_split(x[i], m[i]) for i in range(n)]
        size = GDN_BASE
        while size < c64:
            e = [jnp.where(same_block[2 * size] & ~same_block[size], nn[i], 0.0)
                 for i in range(n)]
            ex = [_dot_split(e[i], x[i]) for i in range(n)]
            x = [x[i] - _dot_split(x[i], ex[i]) for i in range(n)]
            size *= 2
        rhs = [jnp.concatenate([vb[i], kb[i] * egc_c[i]], axis=1) for i in range(n)]
        sol = [_dot_split(x[i], rhs[i]) for i in range(n)]
        return dict(
            u=[sol[i][:, :GDN_DV] for i in range(n)],
            w=[sol[i][:, GDN_DV:].astype(BF16) for i in range(n)],
            aqk=[(qk[i] * gam[i]).astype(BF16) for i in range(n)],
            qd=[(q[i] * egc_c[i]).astype(BF16) for i in range(n)],
            kd=[(k[i] * ekd_c[i]).astype(BF16) for i in range(n)],
            egl=egl_c)

    def advance(c, pre):
        n = len(chains)
        s = [s_ref[i] for i in range(n)]
        s16 = [s[i].astype(BF16) for i in range(n)]
        ws = [_dot(pre["w"][i], s16[i]) for i in range(n)]
        qs = [_dot(pre["qd"][i], s16[i]) for i in range(n)]
        vn16 = [(pre["u"][i] - ws[i]).astype(BF16) for i in range(n)]
        av = [_dot(pre["aqk"][i], vn16[i]) for i in range(n)]
        kv = [_dot_tn(pre["kd"][i], vn16[i]) for i in range(n)]
        for i, (d, h) in enumerate(chains):
            r0 = row0(c, d)
            s_ref[i] = s[i] * pre["egl"][i] + kv[i]
            refs[d][3][r0:r0 + c64, h * GDN_DV:(h + 1) * GDN_DV] = qs[i] + av[i]

    pre = prepare(0)
    for c in range(n_chunks):
        nxt = prepare(c + 1) if c + 1 < n_chunks else None
        advance(c, pre)
        pre = nxt


def _gdn(qkvn, bg, bg_rows, row_off, batch, seqlen, ts):
    c = qkvn.shape[1]
    ns = seqlen // ts
    off = row_off // ts
    fwd = lambda b, s: (off + b * ns + s, 0)
    bwd = lambda b, s: (off + b * ns + ns - 1 - s, 0)
    fwd_r = lambda b, s: (0, off + b * ns + s)
    bwd_r = lambda b, s: (0, off + b * ns + ns - 1 - s)
    nr = bg_rows.shape[0]
    out = jax.ShapeDtypeStruct((batch * seqlen, GDN_V), F32)
    return pl.pallas_call(
        functools.partial(_gdn_body, n_chunks=ts // GDN_CHUNK),
        grid=(batch, ns),
        in_specs=[
            pl.BlockSpec((ts, c), fwd), pl.BlockSpec((ts, V7X_LANES), fwd),
            pl.BlockSpec((nr, ts), fwd_r),
            pl.BlockSpec((ts, c), bwd), pl.BlockSpec((ts, V7X_LANES), bwd),
            pl.BlockSpec((nr, ts), bwd_r),
        ],
        out_specs=[pl.BlockSpec((ts, GDN_V), lambda b, s: (b * ns + s, 0)),
                   pl.BlockSpec((ts, GDN_V), lambda b, s: (b * ns + ns - 1 - s, 0))],
        out_shape=[out, out],
        scratch_shapes=[pltpu.VMEM((2 * GDN_HEADS, GDN_DK, GDN_DV), F32)],
        compiler_params=_cparams(("parallel", "arbitrary")),
        name="gdn",
    )(qkvn, bg, bg_rows, qkvn, bg, bg_rows)


def _attn_body(far_ref, q_ref, k_ref, vt_ref, band_ref, lam_ref, nw_ref, o_ref,
               qz_ref, sf_ref, mxf_ref, sb_ref, mxb_ref, m_ref, acc_ref, *, lam_init, tk, nk):
    far_buf = (sf_ref, mxf_ref)
    band_buf = (sb_ref, mxb_ref)
    h = pl.program_id(1)
    qi = pl.program_id(2)
    tq = q_ref.shape[0]
    sub = V7X_SUBLANES

    m_ref[...] = jnp.full_like(m_ref, -1e30)
    acc_ref[...] = jnp.zeros_like(acc_ref)
    q = q_ref[...]
    lane = lax.broadcasted_iota(jnp.int32, q.shape, 1)
    zero = jnp.zeros_like(q)
    qz_ref[0] = jnp.where(lane < DIFF_DH, q, zero)
    qz_ref[1] = jnp.where(lane >= DIFF_DH, q, zero)

    def scores(kt, buf, slot, bias_tile):
        s_ref, mx_ref = buf
        k = k_ref[pl.ds(pl.multiple_of(kt * tk, tk), tk), :]
        for mp in range(2):
            s = _dot_nt(k, qz_ref[mp])
            if bias_tile is not None:
                s = s + bias_tile
            s_ref[slot, mp] = s
            mx_ref[slot, mp] = jnp.max(s.reshape(tk // sub, sub, tq), axis=0)

    def accumulate(kt, buf, slot, c):
        s_ref, mx_ref = buf
        vt = vt_ref[:, pl.ds(pl.multiple_of(kt * tk, tk), tk)]
        for mp in range(2):
            m_cur = jnp.max(mx_ref[slot, mp], axis=0, keepdims=True) + c
            m_prev = m_ref[mp]
            m_new = jnp.maximum(m_prev, m_cur)
            alpha = jnp.exp2(m_prev - m_new)
            m_ref[mp] = m_new
            p = jnp.exp2(s_ref[slot, mp] - (m_new - c)).astype(BF16)
            acc_ref[mp] = alpha * acc_ref[mp] + _dot(vt, p)

    n_left = jnp.maximum(qi - 1, 0)
    right0 = jnp.minimum(qi + 2, nk)
    n_far = n_left + (nk - right0)

    def far_tile(f):
        return jnp.where(f < n_left, f, right0 + (f - n_left))

    def far_const(f):
        return jnp.where(f < n_left, far_ref[h, 0], far_ref[h, 1])

    band = [(qi, 1), (qi - 1, 0), (qi + 1, 2)]

    def band_scores(j):
        kt, dd = band[j]
        scores(jnp.clip(kt, 0, nk - 1), band_buf, j, band_ref[dd])

    def band_accumulate(j):
        kt, _ = band[j]
        valid = (kt >= 0) & (kt < nk)
        accumulate(jnp.clip(kt, 0, nk - 1), band_buf, j, jnp.where(valid, 0.0, -1e30))

    if nk >= 4:
        scores(far_tile(0), far_buf, 0, None)

        def far_step(f, slot):
            scores(far_tile(f + 1), far_buf, 1 - slot, None)
            accumulate(far_tile(f), far_buf, slot, far_const(f))

        n_steps = n_far - 1

        def far_trip(g, carry):
            for u in range(FAR_UNROLL):
                far_step(FAR_UNROLL * g + u, u % 2)
            return carry

        lax.fori_loop(0, n_steps // FAR_UNROLL, far_trip, 0)
        done = (n_steps // FAR_UNROLL) * FAR_UNROLL
        size = FAR_UNROLL // 2
        while size >= 1:
            taken = (n_steps % (2 * size)) >= size

            @pl.when(taken)
            def _(done=done, size=size):
                for u in range(size):
                    far_step(done + u, u % 2)

            done = done + jnp.where(taken, size, 0)
            size //= 2

        band_scores(0)
        accumulate(far_tile(n_far - 1), far_buf, (n_far - 1) % 2, far_const(n_far - 1))
        band_scores(1)
        band_accumulate(0)
        band_scores(2)
        band_accumulate(1)
        band_accumulate(2)
    else:
        for j in range(3):
            band_scores(j)
            band_accumulate(j)

    lam = lam_ref[...]
    lam_full = (jnp.exp(jnp.sum(lam[0:1] * lam[1:2], keepdims=True))
                - jnp.exp(jnp.sum(lam[2:3] * lam[3:4], keepdims=True)) + lam_init)
    a0 = acc_ref[0]
    a1 = acc_ref[1]
    o = (a0[:DIFF_DV] / a0[DIFF_DV:DIFF_DV + 1]
         - lam_full * (a1[:DIFF_DV] / a1[DIFF_DV:DIFF_DV + 1]))
    o = o * lax.rsqrt(jnp.mean(o * o, axis=0, keepdims=True) + EPS)
    o = o * (nw_ref[...] * (1.0 - lam_init))
    o_ref[...] = o.T


def _attn(qb, kb, vbt, band, far, lam, nw_col, row_off, batch, seqlen, tq, lam_init):
    nq = seqlen // tq
    off = row_off // tq
    off_seq = row_off // seqlen
    assert row_off % seqlen == 0
    return pl.pallas_call(
        functools.partial(_attn_body, lam_init=lam_init, tk=tq, nk=nq),
        grid=(batch, DIFF_HEADS, nq),
        in_specs=[
            pl.BlockSpec(memory_space=pltpu.SMEM),
            pl.BlockSpec((tq, DIFF_DV), lambda b, h, qi: (off + b * nq + qi, h)),
            pl.BlockSpec((seqlen, DIFF_DV), lambda b, h, qi: (off_seq + b, h)),
            pl.BlockSpec((VT_ROWS, seqlen), lambda b, h, qi: (h, off_seq + b)),
            pl.BlockSpec((None, 3, tq, tq), lambda b, h, qi: (h, 0, 0, 0)),
            pl.BlockSpec(lam.shape, lambda b, h, qi: (0, 0)),
            pl.BlockSpec((DIFF_DV, 1), lambda b, h, qi: (0, 0)),
        ],
        out_specs=pl.BlockSpec((tq, DIFF_DV), lambda b, h, qi: (b * nq + qi, h)),
        out_shape=jax.ShapeDtypeStruct((batch * seqlen, DIFF_VW), F32),
        scratch_shapes=[pltpu.VMEM((2, tq, DIFF_DV), BF16),
                        pltpu.VMEM((2, 2, tq, tq), F32),
                        pltpu.VMEM((2, 2, V7X_SUBLANES, tq), F32),
                        pltpu.VMEM((3, 2, tq, tq), F32),
                        pltpu.VMEM((3, 2, V7X_SUBLANES, tq), F32),
                        pltpu.VMEM((2, 1, tq), F32),
                        pltpu.VMEM((2, VT_ROWS, tq), F32)],
        compiler_params=_cparams(("parallel", "parallel", "parallel")),
        name="diffattn",
    )(far, qb, kb, vbt, band, lam, nw_col)


def _memkv_body(m_ref, nw_ref, w_ref, o_ref):
    h = _rms(m_ref[...], nw_ref[...]).astype(BF16)
    o_ref[...] = _dot(h, w_ref[...]).astype(BF16)


def _memkv(mem, nw, wkv):
    nb, nm, d = mem.shape
    return pl.pallas_call(
        _memkv_body,
        grid=(nb,),
        in_specs=[pl.BlockSpec((None, nm, d), lambda b: (b, 0, 0)), _resident((1, d)),
                  _resident(wkv.shape)],
        out_specs=pl.BlockSpec((None, nm, wkv.shape[1]), lambda b: (b, 0, 0)),
        out_shape=jax.ShapeDtypeStruct((nb, nm, wkv.shape[1]), BF16),
        compiler_params=_cparams(("parallel",)),
        name="memkv",
    )(mem, nw, wkv)


def _mergex_body(x_ref, z_ref, sg_ref, kv_ref, gn_ref, wua_ref, wub_ref, wout_ref, xn_ref,
                 wq_ref, wo_ref, *rest, group_tiles):
    o_ref = rest[-1]
    i = pl.program_id(0)
    o = oattn = None
    for g, first in enumerate(group_tiles):
        of_ref, ob_ref, oattn_ref = rest[3 * g:3 * g + 3]
        og = of_ref[...] + ob_ref[...]
        ag = oattn_ref[...]
        o = og if o is None else jnp.where(i >= first, og, o)
        oattn = ag if oattn is None else jnp.where(i >= first, ag, oattn)
    z = z_ref[...]
    gn = gn_ref[...]
    heads = []
    for h in range(GDN_HEADS):
        sl = slice(h * GDN_DV, (h + 1) * GDN_DV)
        zh = z[:, sl]
        heads.append(_rms(o[:, sl], gn) * (zh * jax.nn.sigmoid(zh)))
    oa = jnp.concatenate(heads, axis=1).astype(BF16)
    ya = _dot(oa, wua_ref[...])
    yb = _dot(oattn.astype(BF16), wub_ref[...])
    sg = sg_ref[...]
    merged = sg[:, :D_MODEL] * ya + sg[:, D_MODEL:] * yb
    x = x_ref[...] + _dot(merged.astype(BF16), wout_ref[...])
    hq = _rms(x, xn_ref[...]).astype(BF16)
    q = _dot(hq, wq_ref[...]) * (X_DH ** -0.5)
    kv = kv_ref[...]
    outs = []
    for h in range(X_HEADS):
        sl = slice(h * X_DH, (h + 1) * X_DH)
        kh = kv[:, sl]
        vh = kv[:, X_HEADS * X_DH + h * X_DH:X_HEADS * X_DH + (h + 1) * X_DH]
        s = _dot_nt(q[:, sl].astype(BF16), kh)
        s = s - jnp.max(s, axis=-1, keepdims=True)
        p = jnp.exp(s)
        p = p / jnp.sum(p, axis=-1, keepdims=True)
        outs.append(_dot(p.astype(BF16), vh))
    ox = jnp.concatenate(outs, axis=1).astype(BF16)
    o_ref[...] = x + _dot(ox, wo_ref[...])


def _mergex(x, z, sg, kv, gn, wua, wub, wout, xn, wq, wo, branches, group_rows, tm, tile_batch):
    t, d = x.shape
    rows = lambda width: pl.BlockSpec((tm, width), lambda i: (i, 0))
    in_specs = [rows(d), rows(GDN_V), rows(2 * d),
                pl.BlockSpec((None,) + kv.shape[1:], lambda i: (tile_batch(i), 0, 0)),
                _resident(gn.shape), _resident(wua.shape), _resident(wub.shape),
                _resident(wout.shape), _resident(xn.shape), _resident(wq.shape),
                _resident(wo.shape)]
    args = [x, z, sg, kv, gn, wua, wub, wout, xn, wq, wo]
    for (first_row, n_rows), group in zip(group_rows, branches):
        first, count = first_row // tm, n_rows // tm
        local = lambda i, first=first, count=count: (jnp.clip(i - first, 0, count - 1), 0)
        for arr in group:
            in_specs.append(pl.BlockSpec((tm, arr.shape[1]), local))
            args.append(arr)
    return pl.pallas_call(
        functools.partial(_mergex_body, group_tiles=tuple(r // tm for r, _ in group_rows)),
        grid=(t // tm,),
        in_specs=in_specs,
        out_specs=rows(d),
        out_shape=jax.ShapeDtypeStruct((t, d), F32),
        compiler_params=_cparams(("parallel",)),
        name="mergex",
    )(*args)


def _rel_bucket(rel):
    nb = N_BUCKETS // 2
    ret = jnp.where(rel > 0, nb, 0)
    n = jnp.abs(rel)
    max_exact = nb // 2
    nf = jnp.maximum(n, 1).astype(F32)
    large = max_exact + (jnp.log(nf / max_exact) / math.log(MAX_DISTANCE / max_exact)
                         * (nb - max_exact)).astype(jnp.int32)
    large = jnp.minimum(large, nb - 1)
    return ret + jnp.where(n < max_exact, n, large)


def _toeplitz_body(w_ref, o_ref):
    tq = o_ref.shape[-1]
    x = jnp.broadcast_to(w_ref[...], (tq, 2 * tq))
    o_ref[...] = pltpu.roll(x, 0, 1, stride=1, stride_axis=0)[:, :tq]


def _bias_tables(rel_bias, tq):
    assert tq >= MAX_DISTANCE
    rel = jnp.arange(-(2 * tq - 1), 2 * tq, dtype=jnp.int32)
    by_rel = (rel_bias[_rel_bucket(rel)].astype(F32) * LOG2E).T
    nh = by_rel.shape[0]
    gens = []
    for dd in range(3):
        lo = dd * tq
        gens.append(jnp.concatenate(
            [by_rel[:, lo:lo + tq][:, ::-1], jnp.zeros((nh, 1), F32),
             by_rel[:, lo + tq:lo + 2 * tq - 1][:, ::-1]], axis=1))
    gen = jnp.stack(gens, axis=1)[:, :, None, :]
    band = pl.pallas_call(
        _toeplitz_body,
        grid=(nh, 3),
        in_specs=[pl.BlockSpec((None, None, 1, 2 * tq), lambda h, d: (h, d, 0, 0))],
        out_specs=pl.BlockSpec((None, None, tq, tq), lambda h, d: (h, d, 0, 0)),
        out_shape=jax.ShapeDtypeStruct((nh, 3, tq, tq), F32),
        compiler_params=_cparams(("parallel", "parallel")),
        name="toeplitz",
    )(gen)
    far = jnp.stack([by_rel[:, 0], by_rel[:, -1]], axis=1)
    return band, far


def _pick(limit, n):
    tile = limit
    while n % tile:
        tile //= 2
    return tile


def _encode(xs, mems, p, tile_limits=None):
    lim = dict(ffn=512, proj=256, prep=256, gdn=512, attn=512, mergex=512)
    if tile_limits:
        lim.update(tile_limits)
    depth = p["w_in"].shape[0]
    d = D_MODEL
    groups = []
    row = 0
    bat = 0
    for x in xs:
        b, l, _ = x.shape
        groups.append(dict(b=b, l=l, row=row, bat=bat))
        row += b * l
        bat += b
    t_all = row
    seqlens = [g["l"] for g in groups]
    common = functools.reduce(math.gcd, seqlens)
    tiles = {k: _pick(v, common) for k, v in lim.items()}

    x = jnp.concatenate([xx.reshape(-1, d) for xx in xs], axis=0)
    mem = jnp.concatenate(mems, axis=0)

    def tile_batch(i):
        r0 = i * tiles["mergex"]
        bidx = 0
        for g in groups:
            bidx = jnp.where(r0 >= g["row"], g["bat"] + (r0 - g["row"]) // g["l"], bidx)
        return bidx

    seq_tiles = tuple((g["row"] // tiles["prep"], g["l"] // tiles["prep"]) for g in groups)
    band, far = _bias_tables(p["rel_bias"], tiles["attn"])

    def ffn_weights(prefix, i):
        return (p[prefix + "_norm"][i].reshape(1, d), p[prefix + "_w_gate"][i].astype(BF16),
                p[prefix + "_w_up"][i].astype(BF16), p[prefix + "_w_down"][i].astype(BF16))

    for i in range(depth):
        x = _ffn(x, *ffn_weights("ffn1", i), None, tiles["ffn"])

        w_in = p["w_in"][i]
        o_beta = 3 * GDN_QK
        o_z = o_beta + 2 * N_GATES
        o_qb = o_z + GDN_V
        gate_w = jnp.pad(w_in[:, o_beta:o_z], ((0, 0), (0, V7X_LANES - 2 * N_GATES)))
        w_perm = jnp.concatenate(
            [w_in[:, :o_beta], gate_w, w_in[:, o_z:o_qb],
             w_in[:, o_qb:o_qb + DIFF_QK] * (DIFF_DH ** -0.5 * LOG2E),
             w_in[:, o_qb + DIFF_QK:]],
            axis=1).astype(BF16)
        assert w_perm.shape[1] == _P_COLS
        qkva, gates, z, qb, kb, vb, sg = _proj(x, p["mix_norm"][i].reshape(1, d), w_perm,
                                               tiles["proj"])

        conv_w8 = jnp.pad(p["conv_w"][i], ((0, HALO - GDN_CONV), (0, 0)))
        lane_pad = (N_GATES, V7X_LANES - 2 * N_GATES)
        alog_pad = jnp.pad(p["gdn_a_log"][i].reshape(-1), lane_pad).reshape(1, V7X_LANES)
        dtb_pad = jnp.pad(p["gdn_dt_bias"][i].reshape(-1), lane_pad).reshape(1, V7X_LANES)
        qkvn, bg = _gdnprep(qkva, conv_w8, gates, alog_pad, dtb_pad, tiles["prep"], seq_tiles)
        bg_rows = bg[:, :2 * N_GATES].T

        lam_init = 0.8 - 0.6 * math.exp(-0.3 * i)
        vbt = vb.T.reshape(DIFF_HEADS, DIFF_DV, t_all)
        ones_pad = jnp.zeros((DIFF_HEADS, VT_ROWS - DIFF_DV, t_all), BF16).at[:, 0].set(1.0)
        vbt = jnp.concatenate([vbt, ones_pad], axis=1).reshape(DIFF_HEADS * VT_ROWS, t_all)
        branches = []
        for g in groups:
            o_f, o_b = _gdn(qkvn, bg, bg_rows, g["row"], g["b"], g["l"], tiles["gdn"])
            o_attn = _attn(qb, kb, vbt, band, far, p["diff_lambda"][i],
                           p["diff_norm"][i].reshape(DIFF_DV, 1), g["row"], g["b"], g["l"],
                           tiles["attn"], lam_init)
            branches.append((o_f, o_b, o_attn))

        kv = _memkv(mem, p["mem_norm"][i].reshape(1, d), p["xattn_wkv"][i].astype(BF16))
        x = _mergex(x, z, sg, kv, p["gdn_norm"][i].reshape(1, GDN_DV),
                    p["w_up_a"][i].astype(BF16), p["w_up_b"][i].astype(BF16),
                    p["w_out"][i].astype(BF16), p["xattn_norm"][i].reshape(1, d),
                    p["xattn_wq"][i].astype(BF16), p["xattn_wo"][i].astype(BF16),
                    branches, [(g["row"], g["b"] * g["l"]) for g in groups],
                    tiles["mergex"], tile_batch)

        final_w = p["final_norm"].reshape(1, d) if i == depth - 1 else None
        x = _ffn(x, *ffn_weights("ffn2", i), final_w, tiles["ffn"])

    return tuple(x[g["row"]:g["row"] + g["b"] * g["l"]].reshape(g["b"], g["l"], d)
                 for g in groups)


def kernel(x_prompt, x_sample, mem_prompt, mem_sample, ffn1_norm, ffn1_w_gate, ffn1_w_up, ffn1_w_down, mix_norm, w_in, conv_w, gdn_a_log, gdn_dt_bias, gdn_norm, w_up_a, diff_lambda, diff_norm, w_up_b, w_out, rel_bias, xattn_norm, mem_norm, xattn_wq, xattn_wkv, xattn_wo, ffn2_norm, ffn2_w_gate, ffn2_w_up, ffn2_w_down, final_norm):
    params = dict(
        ffn1_norm=ffn1_norm, ffn1_w_gate=ffn1_w_gate, ffn1_w_up=ffn1_w_up,
        ffn1_w_down=ffn1_w_down, mix_norm=mix_norm, w_in=w_in, conv_w=conv_w,
        gdn_a_log=gdn_a_log, gdn_dt_bias=gdn_dt_bias, gdn_norm=gdn_norm, w_up_a=w_up_a,
        diff_lambda=diff_lambda, diff_norm=diff_norm, w_up_b=w_up_b, w_out=w_out,
        rel_bias=rel_bias, xattn_norm=xattn_norm, mem_norm=mem_norm, xattn_wq=xattn_wq,
        xattn_wkv=xattn_wkv, xattn_wo=xattn_wo, ffn2_norm=ffn2_norm,
        ffn2_w_gate=ffn2_w_gate, ffn2_w_up=ffn2_w_up, ffn2_w_down=ffn2_w_down,
        final_norm=final_norm)
    y_prompt, y_sample = _encode((x_prompt, x_sample), (mem_prompt, mem_sample), params)
    return (y_prompt, y_sample)
```

```python
import functools
import math

import jax
import jax.numpy as jnp
from jax import lax
from jax.experimental import pallas as pl
from jax.experimental.pallas import tpu as pltpu

F32 = jnp.float32
BF16 = jnp.bfloat16
HIGHEST = lax.Precision.HIGHEST

EPS = 1e-6
LOG2E = math.log2(math.e)
D_MODEL = 1024
N_MEM = 256
GDN_HEADS = 4
GDN_DK = 128
GDN_DV = 128
GDN_CONV = 5
GDN_CHUNK = 64
GDN_BASE = 8
DIFF_HEADS = 8
DIFF_DH = 64
DIFF_DV = 2 * DIFF_DH
N_BUCKETS = 32
MAX_DISTANCE = 128
X_HEADS = 4
X_DH = 128
D_FF = 2816
GDN_QK = GDN_HEADS * GDN_DK
GDN_V = GDN_HEADS * GDN_DV
DIFF_QK = DIFF_HEADS * 2 * DIFF_DH
DIFF_VW = DIFF_HEADS * DIFF_DV
N_GATES = 2 * GDN_HEADS
VT_ROWS = DIFF_DV + 16

V7X_LANES = 128
V7X_SUBLANES = 8
V7X_VMEM_LIMIT = 56 * 1024 * 1024

FAR_UNROLL = 8
HALO = V7X_SUBLANES


def _cparams(sem):
    return pltpu.CompilerParams(dimension_semantics=sem, vmem_limit_bytes=V7X_VMEM_LIMIT)


def _dot(a, b, precision=None):
    return jnp.dot(a, b, preferred_element_type=F32, precision=precision)


def _dot_nt(a, b, precision=None):
    return lax.dot_general(a, b, (((1,), (1,)), ((), ())),
                           preferred_element_type=F32, precision=precision)


def _dot_tn(a, b, precision=None):
    return lax.dot_general(a, b, (((0,), (0,)), ((), ())),
                           preferred_element_type=F32, precision=precision)


def _dot_split(a, b):
    a_hi = a.astype(BF16)
    b_hi = b.astype(BF16)
    a_lo = (a - a_hi.astype(F32)).astype(BF16)
    b_lo = (b - b_hi.astype(F32)).astype(BF16)
    return _dot(a_hi, b_hi) + (_dot(a_hi, b_lo) + _dot(a_lo, b_hi))


def _rms(x, w):
    return x * lax.rsqrt(jnp.mean(x * x, axis=-1, keepdims=True) + EPS) * w


def _resident(shape):
    nd = len(shape)
    return pl.BlockSpec(shape, lambda *_: (0,) * nd, pipeline_mode=pl.Buffered(1))


def _ffn_body(x_ref, nw_ref, wg_ref, wu_ref, wd_ref, *rest, final):
    o_ref = rest[-1]
    x = x_ref[...]
    h = _rms(x, nw_ref[...]).astype(BF16)
    g = _dot(h, wg_ref[...])
    u = _dot(h, wu_ref[...])
    a = (g * jax.nn.sigmoid(g) * u).astype(BF16)
    y = x + 0.5 * _dot(a, wd_ref[...])
    if final:
        y = _rms(y, rest[0][...])
    o_ref[...] = y


def _ffn(x, nw, wg, wu, wd, final_w, tm):
    t, d = x.shape
    row = pl.BlockSpec((tm, d), lambda i: (i, 0))
    in_specs = [row, _resident((1, d)), _resident(wg.shape), _resident(wu.shape),
                _resident(wd.shape)]
    args = [x, nw, wg, wu, wd]
    if final_w is not None:
        in_specs.append(_resident((1, d)))
        args.append(final_w)
    return pl.pallas_call(
        functools.partial(_ffn_body, final=final_w is not None),
        grid=(t // tm,),
        in_specs=in_specs,
        out_specs=row,
        out_shape=jax.ShapeDtypeStruct((t, d), F32),
        compiler_params=_cparams(("parallel",)),
        name="ffn",
    )(*args)


_P_QKVA = (0, 3 * GDN_QK)
_P_GATE = (_P_QKVA[1], _P_QKVA[1] + V7X_LANES)
_P_Z = (_P_GATE[1], _P_GATE[1] + GDN_V)
_P_QB = (_P_Z[1], _P_Z[1] + DIFF_QK)
_P_KB = (_P_QB[1], _P_QB[1] + DIFF_QK)
_P_VB = (_P_KB[1], _P_KB[1] + DIFF_VW)
_P_SG = (_P_VB[1], _P_VB[1] + 2 * D_MODEL)
_P_COLS = _P_SG[1]


def _proj_body(x_ref, xprev_ref, xnext_ref, nw_ref, w_ref, cw_ref, alog_ref, dtb_ref,
               qkv_ref, bg_ref, z_ref, qb_ref, kb_ref, vb_ref, sg_ref, *, tm, seq_tiles):
    i = pl.program_id(0)
    first = jnp.bool_(False)
    last = jnp.bool_(False)
    for start, per_seq in seq_tiles:
        rel = i - start
        first = first | ((rel >= 0) & (rel % per_seq == 0))
        last = last | ((rel >= 0) & (rel % per_seq == per_seq - 1))
    x_ext = jnp.concatenate([xprev_ref[...], x_ref[...], xnext_ref[...]], axis=0)
    u_ext = _rms(x_ext, nw_ref[...]).astype(BF16)
    u = u_ext[HALO:HALO + tm]

    def seg(span):
        return _dot(u, w_ref[:, span[0]:span[1]])

    z_ref[...] = seg(_P_Z)
    qb_ref[...] = seg(_P_QB).astype(BF16)
    kb_ref[...] = seg(_P_KB).astype(BF16)
    vb_ref[...] = seg(_P_VB).astype(BF16)
    sg_ref[...] = jax.nn.sigmoid(seg(_P_SG))

    ext = _dot(u_ext, w_ref[:, _P_QKVA[0]:_P_QKVA[1]])
    row = lax.broadcasted_iota(jnp.int32, (tm + 2 * HALO, 1), 0)
    outside = (first & (row < HALO)) | (last & (row >= HALO + tm))
    ext = jnp.where(outside, 0.0, ext)
    pad = (GDN_CONV - 1) // 2
    acc = None
    for k in range(GDN_CONV):
        lo = HALO - pad + k
        term = ext[lo:lo + tm, :] * cw_ref[k:k + 1, :]
        acc = term if acc is None else acc + term
    y = acc * jax.nn.sigmoid(acc)
    for h in range(2 * GDN_HEADS):
        lo = h * GDN_DK
        yh = y[:, lo:lo + GDN_DK]
        nrm = yh * lax.rsqrt(jnp.sum(yh * yh, axis=-1, keepdims=True) + EPS)
        if h < GDN_HEADS:
            nrm = nrm * (GDN_DK ** -0.5)
        qkv_ref[:, lo:lo + GDN_DK] = nrm
    qkv_ref[:, 2 * GDN_QK:] = y[:, 2 * GDN_QK:]
    gl = seg(_P_GATE)
    beta = jax.nn.sigmoid(gl)
    xa = gl + dtb_ref[...]
    softplus = jnp.maximum(xa, 0.0) + jnp.log1p(jnp.exp(-jnp.abs(xa)))
    g = -jnp.exp(alog_ref[...]) * softplus
    lane = lax.broadcasted_iota(jnp.int32, gl.shape, 1)
    bg_ref[...] = jnp.where(lane < N_GATES, beta, g)


def _proj(x, nw, w, conv_w8, alog_pad, dtb_pad, tm, seq_tiles):
    t, d = x.shape
    hb = tm // HALO
    nblk = t // HALO
    widths = [(_P_QKVA, F32), (_P_GATE, F32), (_P_Z, F32), (_P_QB, BF16), (_P_KB, BF16),
              (_P_VB, BF16), (_P_SG, F32)]
    out_shape = [jax.ShapeDtypeStruct((t, s[1] - s[0]), dt) for s, dt in widths]
    out_specs = [pl.BlockSpec((tm, s[1] - s[0]), lambda i: (i, 0)) for s, _ in widths]
    return pl.pallas_call(
        functools.partial(_proj_body, tm=tm, seq_tiles=seq_tiles),
        grid=(t // tm,),
        in_specs=[pl.BlockSpec((tm, d), lambda i: (i, 0)),
                  pl.BlockSpec((HALO, d), lambda i: (jnp.maximum(i * hb - 1, 0), 0)),
                  pl.BlockSpec((HALO, d), lambda i: (jnp.minimum((i + 1) * hb, nblk - 1), 0)),
                  _resident((1, d)), _resident(w.shape), _resident(conv_w8.shape),
                  _resident((1, V7X_LANES)), _resident((1, V7X_LANES))],
        out_specs=out_specs,
        out_shape=out_shape,
        compiler_params=_cparams(("parallel",)),
        name="proj",
    )(x, x, x, nw, w, conv_w8, alog_pad, dtb_pad)


def _gdn_body(qkv_f, bgc_f, bgr_f, qkv_b, bgc_b, bgr_b, of_ref, ob_ref, s_ref, *, n_chunks):
    c64 = GDN_CHUNK

    @pl.when(pl.program_id(1) == 0)
    def _():
        s_ref[...] = jnp.zeros_like(s_ref)

    row = lax.broadcasted_iota(jnp.int32, (c64, c64), 0)
    col = lax.broadcasted_iota(jnp.int32, (c64, c64), 1)
    eye = (row == col).astype(F32)
    incl = (row >= col, row <= col)
    strict = (row > col, row < col)
    same_block = {}
    size = GDN_BASE
    while size <= c64:
        same_block[size] = (row // size) == (col // size)
        size *= 2

    refs = ((qkv_f, bgc_f, bgr_f, of_ref), (qkv_b, bgc_b, bgr_b, ob_ref))
    chains = [(d, h) for d in range(2) for h in range(GDN_HEADS)]

    def row0(c, d):
        return (c if d == 0 else n_chunks - 1 - c) * c64

    def prepare(c):
        gates = []
        for d in range(2):
            _, bgc_ref, bgr_ref, _ = refs[d]
            r0 = row0(c, d)
            gt = bgc_ref[r0:r0 + c64, :]
            gcs = _dot(incl[d].astype(F32), gt, HIGHEST)
            grs = _dot(bgr_ref[:, r0:r0 + c64], incl[1 - d].astype(F32), HIGHEST)
            g_last = gcs[c64 - 1:c64, :] if d == 0 else gcs[0:1, :]
            gates.append((gt, gcs, grs, jnp.exp(gcs), jnp.exp(g_last - gcs), jnp.exp(g_last)))
        q, k, k16, kb, vb, gam, egc_c, ekd_c, egl_c = ([] for _ in range(9))
        for d, h in chains:
            qkv_ref = refs[d][0]
            r0 = row0(c, d)
            gt, gcs, grs, egc, ekd, egl = gates[d]
            idx = d * GDN_HEADS + h
            gi = N_GATES + idx
            qq = qkv_ref[r0:r0 + c64, h * GDN_DK:(h + 1) * GDN_DK]
            kk_ = qkv_ref[r0:r0 + c64, GDN_QK + h * GDN_DK:GDN_QK + (h + 1) * GDN_DK]
            vv = qkv_ref[r0:r0 + c64, 2 * GDN_QK + h * GDN_DV:2 * GDN_QK + (h + 1) * GDN_DV]
            beta = gt[:, idx:idx + 1]
            diff = gcs[:, gi:gi + 1] - grs[gi:gi + 1, :]
            gam.append(jnp.where(incl[d], jnp.exp(jnp.where(incl[d], diff, 0.0)), 0.0))
            q.append(qq)
            k.append(kk_)
            k16.append(kk_.astype(BF16))
            kb.append(kk_ * beta)
            vb.append(vv * beta)
            egc_c.append(egc[:, gi:gi + 1])
            ekd_c.append(ekd[:, gi:gi + 1])
            egl_c.append(egl[:, gi:gi + 1])
        n = len(chains)
        kk = [_dot_nt(kb[i].astype(BF16), k16[i]) for i in range(n)]
        qk = [_dot_nt(q[i].astype(BF16), k16[i]) for i in range(n)]
        nn = [jnp.where(strict[chains[i][0]], kk[i] * gam[i], 0.0) for i in range(n)]
        m = [-jnp.where(same_block[GDN_BASE], nn[i], 0.0) for i in range(n)]
        x = [eye + m[i] for i in range(n)]
        for _ in range(2):
            m = [_dot_split(m[i], m[i]) for i in range(n)]
            x = [x[i] + _dot_split(x[i], m[i]) for i in range(n)]
        size = GDN_BASE
        while size < c64:
            e = [jnp.where(same_block[2 * size] & ~same_block[size], nn[i], 0.0)
                 for i in range(n)]
            ex = [_dot_split(e[i], x[i]) for i in range(n)]
            x = [x[i] - _dot_split(x[i], ex[i]) for i in range(n)]
            size *= 2
        rhs = [jnp.concatenate([vb[i], kb[i] * egc_c[i]], axis=1) for i in range(n)]
        sol = [_dot_split(x[i], rhs[i]) for i in range(n)]
        return dict(
            u=[sol[i][:, :GDN_DV] for i in range(n)],
            w=[sol[i][:, GDN_DV:].astype(BF16) for i in range(n)],
            aqk=[(qk[i] * gam[i]).astype(BF16) for i in range(n)],
            qd=[(q[i] * egc_c[i]).astype(BF16) for i in range(n)],
            kd=[(k[i] * ekd_c[i]).astype(BF16) for i in range(n)],
            egl=egl_c)

    def advance(c, pre):
        n = len(chains)
        s = [s_ref[i] for i in range(n)]
        s16 = [s[i].astype(BF16) for i in range(n)]
        ws = [_dot(pre["w"][i], s16[i]) for i in range(n)]
        qs = [_dot(pre["qd"][i], s16[i]) for i in range(n)]
        vn16 = [(pre["u"][i] - ws[i]).astype(BF16) for i in range(n)]
        av = [_dot(pre["aqk"][i], vn16[i]) for i in range(n)]
        kv = [_dot_tn(pre["kd"][i], vn16[i]) for i in range(n)]
        for i, (d, h) in enumerate(chains):
            r0 = row0(c, d)
            s_ref[i] = s[i] * pre["egl"][i] + kv[i]
            refs[d][3][r0:r0 + c64, h * GDN_DV:(h + 1) * GDN_DV] = qs[i] + av[i]

    pre = prepare(0)
    for c in range(n_chunks):
        nxt = prepare(c + 1) if c + 1 < n_chunks else None
        advance(c, pre)
        pre = nxt


def _gdn(qkvn, bg, bg_rows, row_off, batch, seqlen, ts):
    c = qkvn.shape[1]
    ns = seqlen // ts
    off = row_off // ts
    fwd = lambda b, s: (off + b * ns + s, 0)
    bwd = lambda b, s: (off + b * ns + ns - 1 - s, 0)
    fwd_r = lambda b, s: (0, off + b * ns + s)
    bwd_r = lambda b, s: (0, off + b * ns + ns - 1 - s)
    nr = bg_rows.shape[0]
    out = jax.ShapeDtypeStruct((batch * seqlen, GDN_V), F32)
    return pl.pallas_call(
        functools.partial(_gdn_body, n_chunks=ts // GDN_CHUNK),
        grid=(batch, ns),
        in_specs=[
            pl.BlockSpec((ts, c), fwd), pl.BlockSpec((ts, V7X_LANES), fwd),
            pl.BlockSpec((nr, ts), fwd_r),
            pl.BlockSpec((ts, c), bwd), pl.BlockSpec((ts, V7X_LANES), bwd),
            pl.BlockSpec((nr, ts), bwd_r),
        ],
        out_specs=[pl.BlockSpec((ts, GDN_V), lambda b, s: (b * ns + s, 0)),
                   pl.BlockSpec((ts, GDN_V), lambda b, s: (b * ns + ns - 1 - s, 0))],
        out_shape=[out, out],
        scratch_shapes=[pltpu.VMEM((2 * GDN_HEADS, GDN_DK, GDN_DV), F32)],
        compiler_params=_cparams(("parallel", "arbitrary")),
        name="gdn",
    )(qkvn, bg, bg_rows, qkvn, bg, bg_rows)


def _attn_body(far_ref, q_ref, k_ref, vt_ref, band_ref, lam_ref, nw_ref, o_ref,
               qz_ref, sf_ref, mxf_ref, sb_ref, mxb_ref, m_ref, acc_ref, *, lam_init, tk, nk):
    far_buf = (sf_ref, mxf_ref)
    band_buf = (sb_ref, mxb_ref)
    h = pl.program_id(1)
    qi = pl.program_id(2)
    tq = q_ref.shape[0]
    sub = V7X_SUBLANES

    m_ref[...] = jnp.full_like(m_ref, -1e30)
    acc_ref[...] = jnp.zeros_like(acc_ref)
    q = q_ref[...]
    lane = lax.broadcasted_iota(jnp.int32, q.shape, 1)
    zero = jnp.zeros_like(q)
    qz_ref[0] = jnp.where(lane < DIFF_DH, q, zero)
    qz_ref[1] = jnp.where(lane >= DIFF_DH, q, zero)

    def scores(kt, buf, slot, bias_tile):
        s_ref, mx_ref = buf
        k = k_ref[pl.ds(pl.multiple_of(kt * tk, tk), tk), :]
        for mp in range(2):
            s = _dot_nt(k, qz_ref[mp])
            if bias_tile is not None:
                s = s + bias_tile
            s_ref[slot, mp] = s
            mx_ref[slot, mp] = jnp.max(s.reshape(tk // sub, sub, tq), axis=0)

    def accumulate(kt, buf, slot, c):
        s_ref, mx_ref = buf
        vt = vt_ref[:, pl.ds(pl.multiple_of(kt * tk, tk), tk)]
        for mp in range(2):
            m_cur = jnp.max(mx_ref[slot, mp], axis=0, keepdims=True) + c
            m_prev = m_ref[mp]
            m_new = jnp.maximum(m_prev, m_cur)
            alpha = jnp.exp2(m_prev - m_new)
            m_ref[mp] = m_new
            p = jnp.exp2(s_ref[slot, mp] - (m_new - c)).astype(BF16)
            acc_ref[mp] = alpha * acc_ref[mp] + _dot(vt, p)

    n_left = jnp.maximum(qi - 1, 0)
    right0 = jnp.minimum(qi + 2, nk)
    n_far = n_left + (nk - right0)

    def far_tile(f):
        return jnp.where(f < n_left, f, right0 + (f - n_left))

    def far_const(f):
        return jnp.where(f < n_left, far_ref[h, 0], far_ref[h, 1])

    band = [(qi, 1), (qi - 1, 0), (qi + 1, 2)]

    def band_scores(j):
        kt, dd = band[j]
        scores(jnp.clip(kt, 0, nk - 1), band_buf, j, band_ref[dd])

    def band_accumulate(j):
        kt, _ = band[j]
        valid = (kt >= 0) & (kt < nk)
        accumulate(jnp.clip(kt, 0, nk - 1), band_buf, j, jnp.where(valid, 0.0, -1e30))

    if nk >= 4:
        scores(far_tile(0), far_buf, 0, None)

        def far_step(f, slot):
            scores(far_tile(f + 1), far_buf, 1 - slot, None)
            accumulate(far_tile(f), far_buf, slot, far_const(f))

        n_steps = n_far - 1

        def far_trip(g, carry):
            for u in range(FAR_UNROLL):
                far_step(FAR_UNROLL * g + u, u % 2)
            return carry

        lax.fori_loop(0, n_steps // FAR_UNROLL, far_trip, 0)
        done = (n_steps // FAR_UNROLL) * FAR_UNROLL
        size = FAR_UNROLL // 2
        while size >= 1:
            taken = (n_steps % (2 * size)) >= size

            @pl.when(taken)
            def _(done=done, size=size):
                for u in range(size):
                    far_step(done + u, u % 2)

            done = done + jnp.where(taken, size, 0)
            size //= 2

        band_scores(0)
        accumulate(far_tile(n_far - 1), far_buf, (n_far - 1) % 2, far_const(n_far - 1))
        band_scores(1)
        band_accumulate(0)
        band_scores(2)
        band_accumulate(1)
        band_accumulate(2)
    else:
        for j in range(3):
            band_scores(j)
            band_accumulate(j)

    lam = lam_ref[...]
    lam_full = (jnp.exp(jnp.sum(lam[0:1] * lam[1:2], keepdims=True))
                - jnp.exp(jnp.sum(lam[2:3] * lam[3:4], keepdims=True)) + lam_init)
    a0 = acc_ref[0]
    a1 = acc_ref[1]
    o = (a0[:DIFF_DV] / a0[DIFF_DV:DIFF_DV + 1]
         - lam_full * (a1[:DIFF_DV] / a1[DIFF_DV:DIFF_DV + 1]))
    o = o * lax.rsqrt(jnp.mean(o * o, axis=0, keepdims=True) + EPS)
    o = o * (nw_ref[...] * (1.0 - lam_init))
    o_ref[...] = o.T


def _attn(qb, kb, vbt, band, far, lam, nw_col, row_off, batch, seqlen, tq, lam_init):
    nq = seqlen // tq
    off = row_off // tq
    off_seq = row_off // seqlen
    assert row_off % seqlen == 0
    return pl.pallas_call(
        functools.partial(_attn_body, lam_init=lam_init, tk=tq, nk=nq),
        grid=(batch, DIFF_HEADS, nq),
        in_specs=[
            pl.BlockSpec(memory_space=pltpu.SMEM),
            pl.BlockSpec((tq, DIFF_DV), lambda b, h, qi: (off + b * nq + qi, h)),
            pl.BlockSpec((seqlen, DIFF_DV), lambda b, h, qi: (off_seq + b, h)),
            pl.BlockSpec((VT_ROWS, seqlen), lambda b, h, qi: (h, off_seq + b)),
            pl.BlockSpec((None, 3, tq, tq), lambda b, h, qi: (h, 0, 0, 0)),
            pl.BlockSpec(lam.shape, lambda b, h, qi: (0, 0)),
            pl.BlockSpec((DIFF_DV, 1), lambda b, h, qi: (0, 0)),
        ],
        out_specs=pl.BlockSpec((tq, DIFF_DV), lambda b, h, qi: (b * nq + qi, h)),
        out_shape=jax.ShapeDtypeStruct((batch * seqlen, DIFF_VW), F32),
        scratch_shapes=[pltpu.VMEM((2, tq, DIFF_DV), BF16),
                        pltpu.VMEM((2, 2, tq, tq), F32),
                        pltpu.VMEM((2, 2, V7X_SUBLANES, tq), F32),
                        pltpu.VMEM((3, 2, tq, tq), F32),
                        pltpu.VMEM((3, 2, V7X_SUBLANES, tq), F32),
                        pltpu.VMEM((2, 1, tq), F32),
                        pltpu.VMEM((2, VT_ROWS, tq), F32)],
        compiler_params=_cparams(("parallel", "parallel", "parallel")),
        name="diffattn",
    )(far, qb, kb, vbt, band, lam, nw_col)


def _memkv_body(m_ref, nw_ref, w_ref, o_ref):
    h = _rms(m_ref[...], nw_ref[...]).astype(BF16)
    o_ref[...] = _dot(h, w_ref[...]).astype(BF16)


def _memkv(mem, nw, wkv):
    nb, nm, d = mem.shape
    return pl.pallas_call(
        _memkv_body,
        grid=(nb,),
        in_specs=[pl.BlockSpec((None, nm, d), lambda b: (b, 0, 0)), _resident((1, d)),
                  _resident(wkv.shape)],
        out_specs=pl.BlockSpec((None, nm, wkv.shape[1]), lambda b: (b, 0, 0)),
        out_shape=jax.ShapeDtypeStruct((nb, nm, wkv.shape[1]), BF16),
        compiler_params=_cparams(("parallel",)),
        name="memkv",
    )(mem, nw, wkv)


def _mergex_body(x_ref, z_ref, sg_ref, kv_ref, gn_ref, wua_ref, wub_ref, wout_ref, xn_ref,
                 wq_ref, wo_ref, *rest, group_tiles):
    o_ref = rest[-1]
    i = pl.program_id(0)
    o = oattn = None
    for g, first in enumerate(group_tiles):
        of_ref, ob_ref, oattn_ref = rest[3 * g:3 * g + 3]
        og = of_ref[...] + ob_ref[...]
        ag = oattn_ref[...]
        o = og if o is None else jnp.where(i >= first, og, o)
        oattn = ag if oattn is None else jnp.where(i >= first, ag, oattn)
    z = z_ref[...]
    gn = gn_ref[...]
    heads = []
    for h in range(GDN_HEADS):
        sl = slice(h * GDN_DV, (h + 1) * GDN_DV)
        zh = z[:, sl]
        heads.append(_rms(o[:, sl], gn) * (zh * jax.nn.sigmoid(zh)))
    oa = jnp.concatenate(heads, axis=1).astype(BF16)
    ya = _dot(oa, wua_ref[...])
    yb = _dot(oattn.astype(BF16), wub_ref[...])
    sg = sg_ref[...]
    merged = sg[:, :D_MODEL] * ya + sg[:, D_MODEL:] * yb
    x = x_ref[...] + _dot(merged.astype(BF16), wout_ref[...])
    hq = _rms(x, xn_ref[...]).astype(BF16)
    q = _dot(hq, wq_ref[...]) * (X_DH ** -0.5)
    kv = kv_ref[...]
    outs = []
    for h in range(X_HEADS):
        sl = slice(h * X_DH, (h + 1) * X_DH)
        kh = kv[:, sl]
        vh = kv[:, X_HEADS * X_DH + h * X_DH:X_HEADS * X_DH + (h + 1) * X_DH]
        s = _dot_nt(q[:, sl].astype(BF16), kh)
        s = s - jnp.max(s, axis=-1, keepdims=True)
        p = jnp.exp(s)
        p = p / jnp.sum(p, axis=-1, keepdims=True)
        outs.append(_dot(p.astype(BF16), vh))
    ox = jnp.concatenate(outs, axis=1).astype(BF16)
    o_ref[...] = x + _dot(ox, wo_ref[...])


def _mergex(x, z, sg, kv, gn, wua, wub, wout, xn, wq, wo, branches, group_rows, tm, tile_batch):
    t, d = x.shape
    rows = lambda width: pl.BlockSpec((tm, width), lambda i: (i, 0))
    in_specs = [rows(d), rows(GDN_V), rows(2 * d),
                pl.BlockSpec((None,) + kv.shape[1:], lambda i: (tile_batch(i), 0, 0)),
                _resident(gn.shape), _resident(wua.shape), _resident(wub.shape),
                _resident(wout.shape), _resident(xn.shape), _resident(wq.shape),
                _resident(wo.shape)]
    args = [x, z, sg, kv, gn, wua, wub, wout, xn, wq, wo]
    for (first_row, n_rows), group in zip(group_rows, branches):
        first, count = first_row // tm, n_rows // tm
        local = lambda i, first=first, count=count: (jnp.clip(i - first, 0, count - 1), 0)
        for arr in group:
            in_specs.append(pl.BlockSpec((tm, arr.shape[1]), local))
            args.append(arr)
    return pl.pallas_call(
        functools.partial(_mergex_body, group_tiles=tuple(r // tm for r, _ in group_rows)),
        grid=(t // tm,),
        in_specs=in_specs,
        out_specs=rows(d),
        out_shape=jax.ShapeDtypeStruct((t, d), F32),
        compiler_params=_cparams(("parallel",)),
        name="mergex",
    )(*args)


def _rel_bucket(rel):
    nb = N_BUCKETS // 2
    ret = jnp.where(rel > 0, nb, 0)
    n = jnp.abs(rel)
    max_exact = nb // 2
    nf = jnp.maximum(n, 1).astype(F32)
    large = max_exact + (jnp.log(nf / max_exact) / math.log(MAX_DISTANCE / max_exact)
                         * (nb - max_exact)).astype(jnp.int32)
    large = jnp.minimum(large, nb - 1)
    return ret + jnp.where(n < max_exact, n, large)


def _toeplitz_body(w_ref, o_ref):
    tq = o_ref.shape[-1]
    x = jnp.broadcast_to(w_ref[...], (tq, 2 * tq))
    o_ref[...] = pltpu.roll(x, 0, 1, stride=1, stride_axis=0)[:, :tq]


def _bias_tables(rel_bias, tq):
    assert tq >= MAX_DISTANCE
    rel = jnp.arange(-(2 * tq - 1), 2 * tq, dtype=jnp.int32)
    by_rel = (rel_bias[_rel_bucket(rel)].astype(F32) * LOG2E).T
    nh = by_rel.shape[0]
    gens = []
    for dd in range(3):
        lo = dd * tq
        gens.append(jnp.concatenate(
            [by_rel[:, lo:lo + tq][:, ::-1], jnp.zeros((nh, 1), F32),
             by_rel[:, lo + tq:lo + 2 * tq - 1][:, ::-1]], axis=1))
    gen = jnp.stack(gens, axis=1)[:, :, None, :]
    band = pl.pallas_call(
        _toeplitz_body,
        grid=(nh, 3),
        in_specs=[pl.BlockSpec((None, None, 1, 2 * tq), lambda h, d: (h, d, 0, 0))],
        out_specs=pl.BlockSpec((None, None, tq, tq), lambda h, d: (h, d, 0, 0)),
        out_shape=jax.ShapeDtypeStruct((nh, 3, tq, tq), F32),
        compiler_params=_cparams(("parallel", "parallel")),
        name="toeplitz",
    )(gen)
    far = jnp.stack([by_rel[:, 0], by_rel[:, -1]], axis=1)
    return band, far


def _pick(limit, n):
    tile = limit
    while n % tile:
        tile //= 2
    return tile


def _encode(xs, mems, p, tile_limits=None):
    lim = dict(ffn=512, proj=256, gdn=512, attn=512, mergex=512)
    if tile_limits:
        lim.update(tile_limits)
    depth = p["w_in"].shape[0]
    d = D_MODEL
    groups = []
    row = 0
    bat = 0
    for x in xs:
        b, l, _ = x.shape
        groups.append(dict(b=b, l=l, row=row, bat=bat))
        row += b * l
        bat += b
    t_all = row
    seqlens = [g["l"] for g in groups]
    common = functools.reduce(math.gcd, seqlens)
    tiles = {k: _pick(v, common) for k, v in lim.items()}

    x = jnp.concatenate([xx.reshape(-1, d) for xx in xs], axis=0)
    mem = jnp.concatenate(mems, axis=0)

    def tile_batch(i):
        r0 = i * tiles["mergex"]
        bidx = 0
        for g in groups:
            bidx = jnp.where(r0 >= g["row"], g["bat"] + (r0 - g["row"]) // g["l"], bidx)
        return bidx

    seq_tiles = tuple((g["row"] // tiles["proj"], g["l"] // tiles["proj"]) for g in groups)
    band, far = _bias_tables(p["rel_bias"], tiles["attn"])

    def ffn_weights(prefix, i):
        return (p[prefix + "_norm"][i].reshape(1, d), p[prefix + "_w_gate"][i].astype(BF16),
                p[prefix + "_w_up"][i].astype(BF16), p[prefix + "_w_down"][i].astype(BF16))

    for i in range(depth):
        x = _ffn(x, *ffn_weights("ffn1", i), None, tiles["ffn"])

        w_in = p["w_in"][i]
        o_beta = 3 * GDN_QK
        o_z = o_beta + 2 * N_GATES
        o_qb = o_z + GDN_V
        gate_w = jnp.pad(w_in[:, o_beta:o_z], ((0, 0), (0, V7X_LANES - 2 * N_GATES)))
        w_perm = jnp.concatenate(
            [w_in[:, :o_beta], gate_w, w_in[:, o_z:o_qb],
             w_in[:, o_qb:o_qb + DIFF_QK] * (DIFF_DH ** -0.5 * LOG2E),
             w_in[:, o_qb + DIFF_QK:]],
            axis=1).astype(BF16)
        assert w_perm.shape[1] == _P_COLS
        conv_w8 = jnp.pad(p["conv_w"][i], ((0, HALO - GDN_CONV), (0, 0)))
        lane_pad = (N_GATES, V7X_LANES - 2 * N_GATES)
        alog_pad = jnp.pad(p["gdn_a_log"][i].reshape(-1), lane_pad).reshape(1, V7X_LANES)
        dtb_pad = jnp.pad(p["gdn_dt_bias"][i].reshape(-1), lane_pad).reshape(1, V7X_LANES)
        qkvn, bg, z, qb, kb, vb, sg = _proj(x, p["mix_norm"][i].reshape(1, d), w_perm, conv_w8,
                                            alog_pad, dtb_pad, tiles["proj"], seq_tiles)
        bg_rows = bg[:, :2 * N_GATES].T

        lam_init = 0.8 - 0.6 * math.exp(-0.3 * i)
        vbt = vb.T.reshape(DIFF_HEADS, DIFF_DV, t_all)
        ones_pad = jnp.zeros((DIFF_HEADS, VT_ROWS - DIFF_DV, t_all), BF16).at[:, 0].set(1.0)
        vbt = jnp.concatenate([vbt, ones_pad], axis=1).reshape(DIFF_HEADS * VT_ROWS, t_all)
        branches = []
        for g in groups:
            o_f, o_b = _gdn(qkvn, bg, bg_rows, g["row"], g["b"], g["l"], tiles["gdn"])
            o_attn = _attn(qb, kb, vbt, band, far, p["diff_lambda"][i],
                           p["diff_norm"][i].reshape(DIFF_DV, 1), g["row"], g["b"], g["l"],
                           tiles["attn"], lam_init)
            branches.append((o_f, o_b, o_attn))

        kv = _memkv(mem, p["mem_norm"][i].reshape(1, d), p["xattn_wkv"][i].astype(BF16))
        x = _mergex(x, z, sg, kv, p["gdn_norm"][i].reshape(1, GDN_DV),
                    p["w_up_a"][i].astype(BF16), p["w_up_b"][i].astype(BF16),
                    p["w_out"][i].astype(BF16), p["xattn_norm"][i].reshape(1, d),
                    p["xattn_wq"][i].astype(BF16), p["xattn_wo"][i].astype(BF16),
                    branches, [(g["row"], g["b"] * g["l"]) for g in groups],
                    tiles["mergex"], tile_batch)

        final_w = p["final_norm"].reshape(1, d) if i == depth - 1 else None
        x = _ffn(x, *ffn_weights("ffn2", i), final_w, tiles["ffn"])

    return tuple(x[g["row"]:g["row"] + g["b"] * g["l"]].reshape(g["b"], g["l"], d)
                 for g in groups)


def kernel(x_prompt, x_sample, mem_prompt, mem_sample, ffn1_norm, ffn1_w_gate, ffn1_w_up, ffn1_w_down, mix_norm, w_in, conv_w, gdn_a_log, gdn_dt_bias, gdn_norm, w_up_a, diff_lambda, diff_norm, w_up_b, w_out, rel_bias, xattn_norm, mem_norm, xattn_wq, xattn_wkv, xattn_wo, ffn2_norm, ffn2_w_gate, ffn2_w_up, ffn2_w_down, final_norm):
    params = dict(
        ffn1_norm=ffn1_norm, ffn1_w_gate=ffn1_w_gate, ffn1_w_up=ffn1_w_up,
        ffn1_w_down=ffn1_w_down, mix_norm=mix_norm, w_in=w_in, conv_w=conv_w,
        gdn_a_log=gdn_a_log, gdn_dt_bias=gdn_dt_bias, gdn_norm=gdn_norm, w_up_a=w_up_a,
        diff_lambda=diff_lambda, diff_norm=diff_norm, w_up_b=w_up_b, w_out=w_out,
        rel_bias=rel_bias, xattn_norm=xattn_norm, mem_norm=mem_norm, xattn_wq=xattn_wq,
        xattn_wkv=xattn_wkv, xattn_wo=xattn_wo, ffn2_norm=ffn2_norm,
        ffn2_w_gate=ffn2_w_gate, ffn2_w_up=ffn2_w_up, ffn2_w_down=ffn2_w_down,
        final_norm=final_norm)
    y_prompt, y_sample = _encode((x_prompt, x_sample), (mem_prompt, mem_sample), params)
    return (y_prompt, y_sample)
```

```python
import functools
import math

import jax
import jax.numpy as jnp
from jax import lax
from jax.experimental import pallas as pl
from jax.experimental.pallas import tpu as pltpu

F32 = jnp.float32
BF16 = jnp.bfloat16
HIGHEST = lax.Precision.HIGHEST

EPS = 1e-6
LOG2E = math.log2(math.e)
D_MODEL = 1024
N_MEM = 256
GDN_HEADS = 4
GDN_DK = 128
GDN_DV = 128
GDN_CONV = 5
GDN_CHUNK = 64
GDN_BASE = 8
GDN_GROUP = 2
DIFF_HEADS = 8
DIFF_DH = 64
DIFF_DV = 2 * DIFF_DH
N_BUCKETS = 32
MAX_DISTANCE = 128
X_HEADS = 4
X_DH = 128
D_FF = 2816
GDN_QK = GDN_HEADS * GDN_DK
GDN_V = GDN_HEADS * GDN_DV
DIFF_QK = DIFF_HEADS * 2 * DIFF_DH
DIFF_VW = DIFF_HEADS * DIFF_DV
N_GATES = 2 * GDN_HEADS
VT_ROWS = DIFF_DV + 16

V7X_LANES = 128
V7X_SUBLANES = 8
V7X_VMEM_LIMIT = 56 * 1024 * 1024

FAR_UNROLL = 8
HALO = V7X_SUBLANES


def _cparams(sem):
    return pltpu.CompilerParams(dimension_semantics=sem, vmem_limit_bytes=V7X_VMEM_LIMIT)


def _dot(a, b, precision=None):
    return jnp.dot(a, b, preferred_element_type=F32, precision=precision)


def _dot_nt(a, b, precision=None):
    return lax.dot_general(a, b, (((1,), (1,)), ((), ())),
                           preferred_element_type=F32, precision=precision)


def _dot_tn(a, b, precision=None):
    return lax.dot_general(a, b, (((0,), (0,)), ((), ())),
                           preferred_element_type=F32, precision=precision)


def _dot_split(a, b):
    a_hi = a.astype(BF16)
    b_hi = b.astype(BF16)
    a_lo = (a - a_hi.astype(F32)).astype(BF16)
    b_lo = (b - b_hi.astype(F32)).astype(BF16)
    m = a.shape[0]
    both = _dot(jnp.concatenate([a_hi, a_lo], axis=0), b_hi)
    return both[:m] + (both[m:] + _dot(a_hi, b_lo))


def _rms(x, w):
    return x * lax.rsqrt(jnp.mean(x * x, axis=-1, keepdims=True) + EPS) * w


def _resident(shape):
    nd = len(shape)
    return pl.BlockSpec(shape, lambda *_: (0,) * nd, pipeline_mode=pl.Buffered(1))


def _ffn_body(x_ref, nw_ref, wg_ref, wu_ref, wd_ref, *rest, final):
    o_ref = rest[-1]
    x = x_ref[...]
    h = _rms(x, nw_ref[...]).astype(BF16)
    g = _dot(h, wg_ref[...])
    u = _dot(h, wu_ref[...])
    a = (g * jax.nn.sigmoid(g) * u).astype(BF16)
    y = x + 0.5 * _dot(a, wd_ref[...])
    if final:
        y = _rms(y, rest[0][...])
    o_ref[...] = y


def _ffn(x, nw, wg, wu, wd, final_w, tm):
    t, d = x.shape
    row = pl.BlockSpec((tm, d), lambda i: (i, 0))
    in_specs = [row, _resident((1, d)), _resident(wg.shape), _resident(wu.shape),
                _resident(wd.shape)]
    args = [x, nw, wg, wu, wd]
    if final_w is not None:
        in_specs.append(_resident((1, d)))
        args.append(final_w)
    return pl.pallas_call(
        functools.partial(_ffn_body, final=final_w is not None),
        grid=(t // tm,),
        in_specs=in_specs,
        out_specs=row,
        out_shape=jax.ShapeDtypeStruct((t, d), F32),
        compiler_params=_cparams(("parallel",)),
        name="ffn",
    )(*args)


_P_QKVA = (0, 3 * GDN_QK)
_P_GATE = (_P_QKVA[1], _P_QKVA[1] + V7X_LANES)
_P_Z = (_P_GATE[1], _P_GATE[1] + GDN_V)
_P_QB = (_P_Z[1], _P_Z[1] + DIFF_QK)
_P_KB = (_P_QB[1], _P_QB[1] + DIFF_QK)
_P_VB = (_P_KB[1], _P_KB[1] + DIFF_VW)
_P_SG = (_P_VB[1], _P_VB[1] + 2 * D_MODEL)
_P_COLS = _P_SG[1]


def _proj_body(x_ref, xprev_ref, xnext_ref, nw_ref, w_ref, cw_ref, alog_ref, dtb_ref,
               qkv_ref, bg_ref, z_ref, qb_ref, kb_ref, vb_ref, sg_ref, *, tm, seq_tiles):
    i = pl.program_id(0)
    first = jnp.bool_(False)
    last = jnp.bool_(False)
    for start, per_seq in seq_tiles:
        rel = i - start
        first = first | ((rel >= 0) & (rel % per_seq == 0))
        last = last | ((rel >= 0) & (rel % per_seq == per_seq - 1))
    x_ext = jnp.concatenate([xprev_ref[...], x_ref[...], xnext_ref[...]], axis=0)
    u_ext = _rms(x_ext, nw_ref[...]).astype(BF16)
    u = u_ext[HALO:HALO + tm]

    def seg(span):
        return _dot(u, w_ref[:, span[0]:span[1]])

    z_ref[...] = seg(_P_Z)
    qb_ref[...] = seg(_P_QB).astype(BF16)
    kb_ref[...] = seg(_P_KB).astype(BF16)
    vb_ref[...] = seg(_P_VB).astype(BF16)
    sg_ref[...] = jax.nn.sigmoid(seg(_P_SG))

    ext = _dot(u_ext, w_ref[:, _P_QKVA[0]:_P_QKVA[1]])
    row = lax.broadcasted_iota(jnp.int32, (tm + 2 * HALO, 1), 0)
    outside = (first & (row < HALO)) | (last & (row >= HALO + tm))
    ext = jnp.where(outside, 0.0, ext)
    pad = (GDN_CONV - 1) // 2
    acc = None
    for k in range(GDN_CONV):
        lo = HALO - pad + k
        term = ext[lo:lo + tm, :] * cw_ref[k:k + 1, :]
        acc = term if acc is None else acc + term
    y = acc * jax.nn.sigmoid(acc)
    for h in range(2 * GDN_HEADS):
        lo = h * GDN_DK
        yh = y[:, lo:lo + GDN_DK]
        nrm = yh * lax.rsqrt(jnp.sum(yh * yh, axis=-1, keepdims=True) + EPS)
        if h < GDN_HEADS:
            nrm = nrm * (GDN_DK ** -0.5)
        qkv_ref[:, lo:lo + GDN_DK] = nrm
    qkv_ref[:, 2 * GDN_QK:] = y[:, 2 * GDN_QK:]
    gl = seg(_P_GATE)
    beta = jax.nn.sigmoid(gl)
    xa = gl + dtb_ref[...]
    softplus = jnp.maximum(xa, 0.0) + jnp.log1p(jnp.exp(-jnp.abs(xa)))
    g = -jnp.exp(alog_ref[...]) * softplus
    lane = lax.broadcasted_iota(jnp.int32, gl.shape, 1)
    bg_ref[...] = jnp.where(lane < N_GATES, beta, g)


def _proj(x, nw, w, conv_w8, alog_pad, dtb_pad, tm, seq_tiles):
    t, d = x.shape
    hb = tm // HALO
    nblk = t // HALO
    widths = [(_P_QKVA, F32), (_P_GATE, F32), (_P_Z, F32), (_P_QB, BF16), (_P_KB, BF16),
              (_P_VB, BF16), (_P_SG, F32)]
    out_shape = [jax.ShapeDtypeStruct((t, s[1] - s[0]), dt) for s, dt in widths]
    out_specs = [pl.BlockSpec((tm, s[1] - s[0]), lambda i: (i, 0)) for s, _ in widths]
    return pl.pallas_call(
        functools.partial(_proj_body, tm=tm, seq_tiles=seq_tiles),
        grid=(t // tm,),
        in_specs=[pl.BlockSpec((tm, d), lambda i: (i, 0)),
                  pl.BlockSpec((HALO, d), lambda i: (jnp.maximum(i * hb - 1, 0), 0)),
                  pl.BlockSpec((HALO, d), lambda i: (jnp.minimum((i + 1) * hb, nblk - 1), 0)),
                  _resident((1, d)), _resident(w.shape), _resident(conv_w8.shape),
                  _resident((1, V7X_LANES)), _resident((1, V7X_LANES))],
        out_specs=out_specs,
        out_shape=out_shape,
        compiler_params=_cparams(("parallel",)),
        name="proj",
    )(x, x, x, nw, w, conv_w8, alog_pad, dtb_pad)


def _gdn_body(qkv_f, bgc_f, bgr_f, qkv_b, bgc_b, bgr_b, of_ref, ob_ref, s_ref, *, n_chunks):
    c64 = GDN_CHUNK

    @pl.when(pl.program_id(1) == 0)
    def _():
        s_ref[...] = jnp.zeros_like(s_ref)

    row = lax.broadcasted_iota(jnp.int32, (c64, c64), 0)
    col = lax.broadcasted_iota(jnp.int32, (c64, c64), 1)
    eye = (row == col).astype(F32)
    incl = (row >= col, row <= col)
    strict = (row > col, row < col)
    same_block = {}
    size = GDN_BASE
    while size <= c64:
        same_block[size] = (row // size) == (col // size)
        size *= 2

    refs = ((qkv_f, bgc_f, bgr_f, of_ref), (qkv_b, bgc_b, bgr_b, ob_ref))
    chains = [(d, h) for d in range(2) for h in range(GDN_HEADS)]

    def row0(c, d):
        return (c if d == 0 else n_chunks - 1 - c) * c64

    def prepare(cs):
        gates = {}
        for c in cs:
            for d in range(2):
                _, bgc_ref, bgr_ref, _ = refs[d]
                r0 = row0(c, d)
                gt = bgc_ref[r0:r0 + c64, :]
                gcs = _dot(incl[d].astype(F32), gt, HIGHEST)
                grs = _dot(bgr_ref[:, r0:r0 + c64], incl[1 - d].astype(F32), HIGHEST)
                g_last = gcs[c64 - 1:c64, :] if d == 0 else gcs[0:1, :]
                gates[c, d] = (gt, gcs, grs, jnp.exp(gcs), jnp.exp(g_last - gcs),
                               jnp.exp(g_last))
        q, k, k16, kb, vb, gam, egc_c, ekd_c, egl_c = ([] for _ in range(9))
        jobs = [(c, d, h) for c in cs for d, h in chains]
        for c, d, h in jobs:
            qkv_ref = refs[d][0]
            r0 = row0(c, d)
            gt, gcs, grs, egc, ekd, egl = gates[c, d]
            idx = d * GDN_HEADS + h
            gi = N_GATES + idx
            qq = qkv_ref[r0:r0 + c64, h * GDN_DK:(h + 1) * GDN_DK]
            kk_ = qkv_ref[r0:r0 + c64, GDN_QK + h * GDN_DK:GDN_QK + (h + 1) * GDN_DK]
            vv = qkv_ref[r0:r0 + c64, 2 * GDN_QK + h * GDN_DV:2 * GDN_QK + (h + 1) * GDN_DV]
            beta = gt[:, idx:idx + 1]
            diff = gcs[:, gi:gi + 1] - grs[gi:gi + 1, :]
            gam.append(jnp.where(incl[d], jnp.exp(jnp.where(incl[d], diff, 0.0)), 0.0))
            q.append(qq)
            k.append(kk_)
            k16.append(kk_.astype(BF16))
            kb.append(kk_ * beta)
            vb.append(vv * beta)
            egc_c.append(egc[:, gi:gi + 1])
            ekd_c.append(ekd[:, gi:gi + 1])
            egl_c.append(egl[:, gi:gi + 1])
        n = len(jobs)
        kk = [_dot_nt(kb[i].astype(BF16), k16[i]) for i in range(n)]
        qk = [_dot_nt(q[i].astype(BF16), k16[i]) for i in range(n)]
        nn = [jnp.where(strict[jobs[i][1]], kk[i] * gam[i], 0.0) for i in range(n)]
        m = [-jnp.where(same_block[GDN_BASE], nn[i], 0.0) for i in range(n)]
        x = [eye + m[i] for i in range(n)]
        for _ in range(2):
            m = [_dot_split(m[i], m[i]) for i in range(n)]
            x = [x[i] + _dot_split(x[i], m[i]) for i in range(n)]
        size = GDN_BASE
        while size < c64:
            e = [jnp.where(same_block[2 * size] & ~same_block[size], nn[i], 0.0)
                 for i in range(n)]
            ex = [_dot_split(e[i], x[i]) for i in range(n)]
            x = [x[i] - _dot_split(x[i], ex[i]) for i in range(n)]
            size *= 2
        rhs = [jnp.concatenate([vb[i], kb[i] * egc_c[i]], axis=1) for i in range(n)]
        sol = [_dot_split(x[i], rhs[i]) for i in range(n)]
        per_chunk = len(chains)
        pick = lambda vals, j: vals[j * per_chunk:(j + 1) * per_chunk]
        return [dict(
            u=[sol[i][:, :GDN_DV] for i in pick(range(n), j)],
            w=[sol[i][:, GDN_DV:].astype(BF16) for i in pick(range(n), j)],
            aqk=[(qk[i] * gam[i]).astype(BF16) for i in pick(range(n), j)],
            qd=[(q[i] * egc_c[i]).astype(BF16) for i in pick(range(n), j)],
            kd=[(k[i] * ekd_c[i]).astype(BF16) for i in pick(range(n), j)],
            egl=pick(egl_c, j)) for j in range(len(cs))]

    def advance(c, pre):
        n = len(chains)
        s = [s_ref[i] for i in range(n)]
        s16 = [s[i].astype(BF16) for i in range(n)]
        ws = [_dot(pre["w"][i], s16[i]) for i in range(n)]
        qs = [_dot(pre["qd"][i], s16[i]) for i in range(n)]
        vn16 = [(pre["u"][i] - ws[i]).astype(BF16) for i in range(n)]
        av = [_dot(pre["aqk"][i], vn16[i]) for i in range(n)]
        kv = [_dot_tn(pre["kd"][i], vn16[i]) for i in range(n)]
        for i, (d, h) in enumerate(chains):
            r0 = row0(c, d)
            s_ref[i] = s[i] * pre["egl"][i] + kv[i]
            refs[d][3][r0:r0 + c64, h * GDN_DV:(h + 1) * GDN_DV] = qs[i] + av[i]

    groups = [list(range(c, min(c + GDN_GROUP, n_chunks))) for c in range(0, n_chunks, GDN_GROUP)]
    pre = prepare(groups[0])
    for gi, cs in enumerate(groups):
        nxt = prepare(groups[gi + 1]) if gi + 1 < len(groups) else None
        for c, one in zip(cs, pre):
            advance(c, one)
        pre = nxt


def _gdn(qkvn, bg, bg_rows, row_off, batch, seqlen, ts):
    c = qkvn.shape[1]
    ns = seqlen // ts
    off = row_off // ts
    fwd = lambda b, s: (off + b * ns + s, 0)
    bwd = lambda b, s: (off + b * ns + ns - 1 - s, 0)
    fwd_r = lambda b, s: (0, off + b * ns + s)
    bwd_r = lambda b, s: (0, off + b * ns + ns - 1 - s)
    nr = bg_rows.shape[0]
    out = jax.ShapeDtypeStruct((batch * seqlen, GDN_V), F32)
    return pl.pallas_call(
        functools.partial(_gdn_body, n_chunks=ts // GDN_CHUNK),
        grid=(batch, ns),
        in_specs=[
            pl.BlockSpec((ts, c), fwd), pl.BlockSpec((ts, V7X_LANES), fwd),
            pl.BlockSpec((nr, ts), fwd_r),
            pl.BlockSpec((ts, c), bwd), pl.BlockSpec((ts, V7X_LANES), bwd),
            pl.BlockSpec((nr, ts), bwd_r),
        ],
        out_specs=[pl.BlockSpec((ts, GDN_V), lambda b, s: (b * ns + s, 0)),
                   pl.BlockSpec((ts, GDN_V), lambda b, s: (b * ns + ns - 1 - s, 0))],
        out_shape=[out, out],
        scratch_shapes=[pltpu.VMEM((2 * GDN_HEADS, GDN_DK, GDN_DV), F32)],
        compiler_params=_cparams(("parallel", "arbitrary")),
        name="gdn",
    )(qkvn, bg, bg_rows, qkvn, bg, bg_rows)


def _attn_body(far_ref, q_ref, k_ref, vt_ref, band_ref, lam_ref, nw_ref, o_ref,
               qz_ref, sf_ref, mxf_ref, sb_ref, mxb_ref, m_ref, acc_ref, *, lam_init, tk, nk):
    far_buf = (sf_ref, mxf_ref)
    band_buf = (sb_ref, mxb_ref)
    h = pl.program_id(1)
    qi = pl.program_id(2)
    tq = q_ref.shape[0]
    sub = V7X_SUBLANES

    m_ref[...] = jnp.full_like(m_ref, -1e30)
    acc_ref[...] = jnp.zeros_like(acc_ref)
    q = q_ref[...]
    lane = lax.broadcasted_iota(jnp.int32, q.shape, 1)
    zero = jnp.zeros_like(q)
    qz_ref[0] = jnp.where(lane < DIFF_DH, q, zero)
    qz_ref[1] = jnp.where(lane >= DIFF_DH, q, zero)

    def scores(kt, buf, slot, bias_tile):
        s_ref, mx_ref = buf
        k = k_ref[pl.ds(pl.multiple_of(kt * tk, tk), tk), :]
        for mp in range(2):
            s = _dot_nt(k, qz_ref[mp])
            if bias_tile is not None:
                s = s + bias_tile
            s_ref[slot, mp] = s
            mx_ref[slot, mp] = jnp.max(s.reshape(tk // sub, sub, tq), axis=0)

    def accumulate(kt, buf, slot, c):
        s_ref, mx_ref = buf
        vt = vt_ref[:, pl.ds(pl.multiple_of(kt * tk, tk), tk)]
        for mp in range(2):
            m_cur = jnp.max(mx_ref[slot, mp], axis=0, keepdims=True) + c
            m_prev = m_ref[mp]
            m_new = jnp.maximum(m_prev, m_cur)
            alpha = jnp.exp2(m_prev - m_new)
            m_ref[mp] = m_new
            p = jnp.exp2(s_ref[slot, mp] - (m_new - c)).astype(BF16)
            acc_ref[mp] = alpha * acc_ref[mp] + _dot(vt, p)

    n_left = jnp.maximum(qi - 1, 0)
    right0 = jnp.minimum(qi + 2, nk)
    n_far = n_left + (nk - right0)

    def far_tile(f):
        return jnp.where(f < n_left, f, right0 + (f - n_left))

    def far_const(f):
        return jnp.where(f < n_left, far_ref[h, 0], far_ref[h, 1])

    band = [(qi, 1), (qi - 1, 0), (qi + 1, 2)]

    def band_scores(j):
        kt, dd = band[j]
        scores(jnp.clip(kt, 0, nk - 1), band_buf, j, band_ref[dd])

    def band_accumulate(j):
        kt, _ = band[j]
        valid = (kt >= 0) & (kt < nk)
        accumulate(jnp.clip(kt, 0, nk - 1), band_buf, j, jnp.where(valid, 0.0, -1e30))

    if nk >= 4:
        scores(far_tile(0), far_buf, 0, None)

        def far_step(f, slot):
            scores(far_tile(f + 1), far_buf, 1 - slot, None)
            accumulate(far_tile(f), far_buf, slot, far_const(f))

        n_steps = n_far - 1

        def far_trip(g, carry):
            for u in range(FAR_UNROLL):
                far_step(FAR_UNROLL * g + u, u % 2)
            return carry

        lax.fori_loop(0, n_steps // FAR_UNROLL, far_trip, 0)
        done = (n_steps // FAR_UNROLL) * FAR_UNROLL
        size = FAR_UNROLL // 2
        while size >= 1:
            taken = (n_steps % (2 * size)) >= size

            @pl.when(taken)
            def _(done=done, size=size):
                for u in range(size):
                    far_step(done + u, u % 2)

            done = done + jnp.where(taken, size, 0)
            size //= 2

        band_scores(0)
        accumulate(far_tile(n_far - 1), far_buf, (n_far - 1) % 2, far_const(n_far - 1))
        band_scores(1)
        band_accumulate(0)
        band_scores(2)
        band_accumulate(1)
        band_accumulate(2)
    else:
        for j in range(3):
            band_scores(j)
            band_accumulate(j)

    lam = lam_ref[...]
    lam_full = (jnp.exp(jnp.sum(lam[0:1] * lam[1:2], keepdims=True))
                - jnp.exp(jnp.sum(lam[2:3] * lam[3:4], keepdims=True)) + lam_init)
    a0 = acc_ref[0]
    a1 = acc_ref[1]
    o = (a0[:DIFF_DV] / a0[DIFF_DV:DIFF_DV + 1]
         - lam_full * (a1[:DIFF_DV] / a1[DIFF_DV:DIFF_DV + 1]))
    o = o * lax.rsqrt(jnp.mean(o * o, axis=0, keepdims=True) + EPS)
    o = o * (nw_ref[...] * (1.0 - lam_init))
    o_ref[...] = o.T


def _attn(qb, kb, vbt, band, far, lam, nw_col, row_off, batch, seqlen, tq, lam_init):
    nq = seqlen // tq
    off = row_off // tq
    off_seq = row_off // seqlen
    assert row_off % seqlen == 0
    return pl.pallas_call(
        functools.partial(_attn_body, lam_init=lam_init, tk=tq, nk=nq),
        grid=(batch, DIFF_HEADS, nq),
        in_specs=[
            pl.BlockSpec(memory_space=pltpu.SMEM),
            pl.BlockSpec((tq, DIFF_DV), lambda b, h, qi: (off + b * nq + qi, h)),
            pl.BlockSpec((seqlen, DIFF_DV), lambda b, h, qi: (off_seq + b, h)),
            pl.BlockSpec((VT_ROWS, seqlen), lambda b, h, qi: (h, off_seq + b)),
            pl.BlockSpec((None, 3, tq, tq), lambda b, h, qi: (h, 0, 0, 0)),
            pl.BlockSpec(lam.shape, lambda b, h, qi: (0, 0)),
            pl.BlockSpec((DIFF_DV, 1), lambda b, h, qi: (0, 0)),
        ],
        out_specs=pl.BlockSpec((tq, DIFF_DV), lambda b, h, qi: (b * nq + qi, h)),
        out_shape=jax.ShapeDtypeStruct((batch * seqlen, DIFF_VW), F32),
        scratch_shapes=[pltpu.VMEM((2, tq, DIFF_DV), BF16),
                        pltpu.VMEM((2, 2, tq, tq), F32),
                        pltpu.VMEM((2, 2, V7X_SUBLANES, tq), F32),
                        pltpu.VMEM((3, 2, tq, tq), F32),
                        pltpu.VMEM((3, 2, V7X_SUBLANES, tq), F32),
                        pltpu.VMEM((2, 1, tq), F32),
                        pltpu.VMEM((2, VT_ROWS, tq), F32)],
        compiler_params=_cparams(("parallel", "parallel", "parallel")),
        name="diffattn",
    )(far, qb, kb, vbt, band, lam, nw_col)


def _memkv_body(m_ref, nw_ref, w_ref, o_ref):
    h = _rms(m_ref[...], nw_ref[...]).astype(BF16)
    o_ref[...] = _dot(h, w_ref[...]).astype(BF16)


def _memkv(mem, nw, wkv):
    nb, nm, d = mem.shape
    return pl.pallas_call(
        _memkv_body,
        grid=(nb,),
        in_specs=[pl.BlockSpec((None, nm, d), lambda b: (b, 0, 0)), _resident((1, d)),
                  _resident(wkv.shape)],
        out_specs=pl.BlockSpec((None, nm, wkv.shape[1]), lambda b: (b, 0, 0)),
        out_shape=jax.ShapeDtypeStruct((nb, nm, wkv.shape[1]), BF16),
        compiler_params=_cparams(("parallel",)),
        name="memkv",
    )(mem, nw, wkv)


def _mergex_body(x_ref, z_ref, sg_ref, kv_ref, gn_ref, wua_ref, wub_ref, wout_ref, xn_ref,
                 wq_ref, wo_ref, *rest, group_tiles):
    o_ref = rest[-1]
    i = pl.program_id(0)
    o = oattn = None
    for g, first in enumerate(group_tiles):
        of_ref, ob_ref, oattn_ref = rest[3 * g:3 * g + 3]
        og = of_ref[...] + ob_ref[...]
        ag = oattn_ref[...]
        o = og if o is None else jnp.where(i >= first, og, o)
        oattn = ag if oattn is None else jnp.where(i >= first, ag, oattn)
    z = z_ref[...]
    gn = gn_ref[...]
    heads = []
    for h in range(GDN_HEADS):
        sl = slice(h * GDN_DV, (h + 1) * GDN_DV)
        zh = z[:, sl]
        heads.append(_rms(o[:, sl], gn) * (zh * jax.nn.sigmoid(zh)))
    oa = jnp.concatenate(heads, axis=1).astype(BF16)
    ya = _dot(oa, wua_ref[...])
    yb = _dot(oattn.astype(BF16), wub_ref[...])
    sg = sg_ref[...]
    merged = sg[:, :D_MODEL] * ya + sg[:, D_MODEL:] * yb
    x = x_ref[...] + _dot(merged.astype(BF16), wout_ref[...])
    hq = _rms(x, xn_ref[...]).astype(BF16)
    q = _dot(hq, wq_ref[...]) * (X_DH ** -0.5)
    kv = kv_ref[...]
    outs = []
    for h in range(X_HEADS):
        sl = slice(h * X_DH, (h + 1) * X_DH)
        kh = kv[:, sl]
        vh = kv[:, X_HEADS * X_DH + h * X_DH:X_HEADS * X_DH + (h + 1) * X_DH]
        s = _dot_nt(q[:, sl].astype(BF16), kh)
        s = s - jnp.max(s, axis=-1, keepdims=True)
        p = jnp.exp(s)
        p = p / jnp.sum(p, axis=-1, keepdims=True)
        outs.append(_dot(p.astype(BF16), vh))
    ox = jnp.concatenate(outs, axis=1).astype(BF16)
    o_ref[...] = x + _dot(ox, wo_ref[...])


def _mergex(x, z, sg, kv, gn, wua, wub, wout, xn, wq, wo, branches, group_rows, tm, tile_batch):
    t, d = x.shape
    rows = lambda width: pl.BlockSpec((tm, width), lambda i: (i, 0))
    in_specs = [rows(d), rows(GDN_V), rows(2 * d),
                pl.BlockSpec((None,) + kv.shape[1:], lambda i: (tile_batch(i), 0, 0)),
                _resident(gn.shape), _resident(wua.shape), _resident(wub.shape),
                _resident(wout.shape), _resident(xn.shape), _resident(wq.shape),
                _resident(wo.shape)]
    args = [x, z, sg, kv, gn, wua, wub, wout, xn, wq, wo]
    for (first_row, n_rows), group in zip(group_rows, branches):
        first, count = first_row // tm, n_rows // tm
        local = lambda i, first=first, count=count: (jnp.clip(i - first, 0, count - 1), 0)
        for arr in group:
            in_specs.append(pl.BlockSpec((tm, arr.shape[1]), local))
            args.append(arr)
    return pl.pallas_call(
        functools.partial(_mergex_body, group_tiles=tuple(r // tm for r, _ in group_rows)),
        grid=(t // tm,),
        in_specs=in_specs,
        out_specs=rows(d),
        out_shape=jax.ShapeDtypeStruct((t, d), F32),
        compiler_params=_cparams(("parallel",)),
        name="mergex",
    )(*args)


def _rel_bucket(rel):
    nb = N_BUCKETS // 2
    ret = jnp.where(rel > 0, nb, 0)
    n = jnp.abs(rel)
    max_exact = nb // 2
    nf = jnp.maximum(n, 1).astype(F32)
    large = max_exact + (jnp.log(nf / max_exact) / math.log(MAX_DISTANCE / max_exact)
                         * (nb - max_exact)).astype(jnp.int32)
    large = jnp.minimum(large, nb - 1)
    return ret + jnp.where(n < max_exact, n, large)


def _toeplitz_body(w_ref, o_ref):
    tq = o_ref.shape[-1]
    x = jnp.broadcast_to(w_ref[...], (tq, 2 * tq))
    o_ref[...] = pltpu.roll(x, 0, 1, stride=1, stride_axis=0)[:, :tq]


def _bias_tables(rel_bias, tq):
    assert tq >= MAX_DISTANCE
    rel = jnp.arange(-(2 * tq - 1), 2 * tq, dtype=jnp.int32)
    by_rel = (rel_bias[_rel_bucket(rel)].astype(F32) * LOG2E).T
    nh = by_rel.shape[0]
    gens = []
    for dd in range(3):
        lo = dd * tq
        gens.append(jnp.concatenate(
            [by_rel[:, lo:lo + tq][:, ::-1], jnp.zeros((nh, 1), F32),
             by_rel[:, lo + tq:lo + 2 * tq - 1][:, ::-1]], axis=1))
    gen = jnp.stack(gens, axis=1)[:, :, None, :]
    band = pl.pallas_call(
        _toeplitz_body,
        grid=(nh, 3),
        in_specs=[pl.BlockSpec((None, None, 1, 2 * tq), lambda h, d: (h, d, 0, 0))],
        out_specs=pl.BlockSpec((None, None, tq, tq), lambda h, d: (h, d, 0, 0)),
        out_shape=jax.ShapeDtypeStruct((nh, 3, tq, tq), F32),
        compiler_params=_cparams(("parallel", "parallel")),
        name="toeplitz",
    )(gen)
    far = jnp.stack([by_rel[:, 0], by_rel[:, -1]], axis=1)
    return band, far


def _pick(limit, n):
    tile = limit
    while n % tile:
        tile //= 2
    return tile


def _encode(xs, mems, p, tile_limits=None):
    lim = dict(ffn=512, proj=256, gdn=512, attn=512, mergex=512)
    if tile_limits:
        lim.update(tile_limits)
    depth = p["w_in"].shape[0]
    d = D_MODEL
    groups = []
    row = 0
    bat = 0
    for x in xs:
        b, l, _ = x.shape
        groups.append(dict(b=b, l=l, row=row, bat=bat))
        row += b * l
        bat += b
    t_all = row
    seqlens = [g["l"] for g in groups]
    common = functools.reduce(math.gcd, seqlens)
    tiles = {k: _pick(v, common) for k, v in lim.items()}

    x = jnp.concatenate([xx.reshape(-1, d) for xx in xs], axis=0)
    mem = jnp.concatenate(mems, axis=0)

    def tile_batch(i):
        r0 = i * tiles["mergex"]
        bidx = 0
        for g in groups:
            bidx = jnp.where(r0 >= g["row"], g["bat"] + (r0 - g["row"]) // g["l"], bidx)
        return bidx

    seq_tiles = tuple((g["row"] // tiles["proj"], g["l"] // tiles["proj"]) for g in groups)
    band, far = _bias_tables(p["rel_bias"], tiles["attn"])

    def ffn_weights(prefix, i):
        return (p[prefix + "_norm"][i].reshape(1, d), p[prefix + "_w_gate"][i].astype(BF16),
                p[prefix + "_w_up"][i].astype(BF16), p[prefix + "_w_down"][i].astype(BF16))

    for i in range(depth):
        x = _ffn(x, *ffn_weights("ffn1", i), None, tiles["ffn"])

        w_in = p["w_in"][i]
        o_beta = 3 * GDN_QK
        o_z = o_beta + 2 * N_GATES
        o_qb = o_z + GDN_V
        gate_w = jnp.pad(w_in[:, o_beta:o_z], ((0, 0), (0, V7X_LANES - 2 * N_GATES)))
        w_perm = jnp.concatenate(
            [w_in[:, :o_beta], gate_w, w_in[:, o_z:o_qb],
             w_in[:, o_qb:o_qb + DIFF_QK] * (DIFF_DH ** -0.5 * LOG2E),
             w_in[:, o_qb + DIFF_QK:]],
            axis=1).astype(BF16)
        assert w_perm.shape[1] == _P_COLS
        conv_w8 = jnp.pad(p["conv_w"][i], ((0, HALO - GDN_CONV), (0, 0)))
        lane_pad = (N_GATES, V7X_LANES - 2 * N_GATES)
        alog_pad = jnp.pad(p["gdn_a_log"][i].reshape(-1), lane_pad).reshape(1, V7X_LANES)
        dtb_pad = jnp.pad(p["gdn_dt_bias"][i].reshape(-1), lane_pad).reshape(1, V7X_LANES)
        qkvn, bg, z, qb, kb, vb, sg = _proj(x, p["mix_norm"][i].reshape(1, d), w_perm, conv_w8,
                                            alog_pad, dtb_pad, tiles["proj"], seq_tiles)
        bg_rows = bg[:, :2 * N_GATES].T

        lam_init = 0.8 - 0.6 * math.exp(-0.3 * i)
        vbt = vb.T.reshape(DIFF_HEADS, DIFF_DV, t_all)
        ones_pad = jnp.zeros((DIFF_HEADS, VT_ROWS - DIFF_DV, t_all), BF16).at[:, 0].set(1.0)
        vbt = jnp.concatenate([vbt, ones_pad], axis=1).reshape(DIFF_HEADS * VT_ROWS, t_all)
        branches = []
        for g in groups:
            o_f, o_b = _gdn(qkvn, bg, bg_rows, g["row"], g["b"], g["l"], tiles["gdn"])
            o_attn = _attn(qb, kb, vbt, band, far, p["diff_lambda"][i],
                           p["diff_norm"][i].reshape(DIFF_DV, 1), g["row"], g["b"], g["l"],
                           tiles["attn"], lam_init)
            branches.append((o_f, o_b, o_attn))

        kv = _memkv(mem, p["mem_norm"][i].reshape(1, d), p["xattn_wkv"][i].astype(BF16))
        x = _mergex(x, z, sg, kv, p["gdn_norm"][i].reshape(1, GDN_DV),
                    p["w_up_a"][i].astype(BF16), p["w_up_b"][i].astype(BF16),
                    p["w_out"][i].astype(BF16), p["xattn_norm"][i].reshape(1, d),
                    p["xattn_wq"][i].astype(BF16), p["xattn_wo"][i].astype(BF16),
                    branches, [(g["row"], g["b"] * g["l"]) for g in groups],
                    tiles["mergex"], tile_batch)

        final_w = p["final_norm"].reshape(1, d) if i == depth - 1 else None
        x = _ffn(x, *ffn_weights("ffn2", i), final_w, tiles["ffn"])

    return tuple(x[g["row"]:g["row"] + g["b"] * g["l"]].reshape(g["b"], g["l"], d)
                 for g in groups)


def kernel(x_prompt, x_sample, mem_prompt, mem_sample, ffn1_norm, ffn1_w_gate, ffn1_w_up, ffn1_w_down, mix_norm, w_in, conv_w, gdn_a_log, gdn_dt_bias, gdn_norm, w_up_a, diff_lambda, diff_norm, w_up_b, w_out, rel_bias, xattn_norm, mem_norm, xattn_wq, xattn_wkv, xattn_wo, ffn2_norm, ffn2_w_gate, ffn2_w_up, ffn2_w_down, final_norm):
    params = dict(
        ffn1_norm=ffn1_norm, ffn1_w_gate=ffn1_w_gate, ffn1_w_up=ffn1_w_up,
        ffn1_w_down=ffn1_w_down, mix_norm=mix_norm, w_in=w_in, conv_w=conv_w,
        gdn_a_log=gdn_a_log, gdn_dt_bias=gdn_dt_bias, gdn_norm=gdn_norm, w_up_a=w_up_a,
        diff_lambda=diff_lambda, diff_norm=diff_norm, w_up_b=w_up_b, w_out=w_out,
        rel_bias=rel_bias, xattn_norm=xattn_norm, mem_norm=mem_norm, xattn_wq=xattn_wq,
        xattn_wkv=xattn_wkv, xattn_wo=xattn_wo, ffn2_norm=ffn2_norm,
        ffn2_w_gate=ffn2_w_gate, ffn2_w_up=ffn2_w_up, ffn2_w_down=ffn2_w_down,
        final_norm=final_norm)
    y_prompt, y_sample = _encode((x_prompt, x_sample), (mem_prompt, mem_sample), params)
    return (y_prompt, y_sample)
```

```python
import functools
import math

import jax
import jax.numpy as jnp
from jax import lax
from jax.experimental import pallas as pl
from jax.experimental.pallas import tpu as pltpu

F32 = jnp.float32
BF16 = jnp.bfloat16
HIGHEST = lax.Precision.HIGHEST

EPS = 1e-6
LOG2E = math.log2(math.e)
D_MODEL = 1024
N_MEM = 256
GDN_HEADS = 4
GDN_DK = 128
GDN_DV = 128
GDN_CONV = 5
GDN_CHUNK = 64
GDN_BASE = 8
GDN_GROUP = 2
DIFF_HEADS = 8
DIFF_DH = 64
DIFF_DV = 2 * DIFF_DH
N_BUCKETS = 32
MAX_DISTANCE = 128
X_HEADS = 4
X_DH = 128
D_FF = 2816
GDN_QK = GDN_HEADS * GDN_DK
GDN_V = GDN_HEADS * GDN_DV
DIFF_QK = DIFF_HEADS * 2 * DIFF_DH
DIFF_VW = DIFF_HEADS * DIFF_DV
N_GATES = 2 * GDN_HEADS
VT_ROWS = DIFF_DV + 16

V7X_LANES = 128
V7X_SUBLANES = 8
V7X_VMEM_LIMIT = 56 * 1024 * 1024

FAR_UNROLL = 8
HALO = V7X_SUBLANES


def _cparams(sem):
    return pltpu.CompilerParams(dimension_semantics=sem, vmem_limit_bytes=V7X_VMEM_LIMIT)


def _dot(a, b, precision=None):
    return jnp.dot(a, b, preferred_element_type=F32, precision=precision)


def _dot_nt(a, b, precision=None):
    return lax.dot_general(a, b, (((1,), (1,)), ((), ())),
                           preferred_element_type=F32, precision=precision)


def _dot_tn(a, b, precision=None):
    return lax.dot_general(a, b, (((0,), (0,)), ((), ())),
                           preferred_element_type=F32, precision=precision)


def _dot_split(a, b):
    a_hi = a.astype(BF16)
    b_hi = b.astype(BF16)
    a_lo = (a - a_hi.astype(F32)).astype(BF16)
    b_lo = (b - b_hi.astype(F32)).astype(BF16)
    m = a.shape[0]
    both = _dot(jnp.concatenate([a_hi, a_lo], axis=0), b_hi)
    return both[:m] + (both[m:] + _dot(a_hi, b_lo))


def _rms(x, w):
    return x * lax.rsqrt(jnp.mean(x * x, axis=-1, keepdims=True) + EPS) * w


def _resident(shape):
    nd = len(shape)
    return pl.BlockSpec(shape, lambda *_: (0,) * nd, pipeline_mode=pl.Buffered(1))


def _ffn_body(x_ref, nw_ref, wg_ref, wu_ref, wd_ref, *rest, final):
    o_ref = rest[-1]
    x = x_ref[...]
    h = _rms(x, nw_ref[...]).astype(BF16)
    g = _dot(h, wg_ref[...])
    u = _dot(h, wu_ref[...])
    a = (g * jax.nn.sigmoid(g) * u).astype(BF16)
    y = x + 0.5 * _dot(a, wd_ref[...])
    if final:
        y = _rms(y, rest[0][...])
    o_ref[...] = y


def _ffn(x, nw, wg, wu, wd, final_w, tm):
    t, d = x.shape
    row = pl.BlockSpec((tm, d), lambda i: (i, 0))
    in_specs = [row, _resident((1, d)), _resident(wg.shape), _resident(wu.shape),
                _resident(wd.shape)]
    args = [x, nw, wg, wu, wd]
    if final_w is not None:
        in_specs.append(_resident((1, d)))
        args.append(final_w)
    return pl.pallas_call(
        functools.partial(_ffn_body, final=final_w is not None),
        grid=(t // tm,),
        in_specs=in_specs,
        out_specs=row,
        out_shape=jax.ShapeDtypeStruct((t, d), F32),
        compiler_params=_cparams(("parallel",)),
        name="ffn",
    )(*args)


_P_QKVA = (0, 3 * GDN_QK)
_P_GATE = (_P_QKVA[1], _P_QKVA[1] + V7X_LANES)
_P_Z = (_P_GATE[1], _P_GATE[1] + GDN_V)
_P_QB = (_P_Z[1], _P_Z[1] + DIFF_QK)
_P_KB = (_P_QB[1], _P_QB[1] + DIFF_QK)
_P_VB = (_P_KB[1], _P_KB[1] + DIFF_VW)
_P_SG = (_P_VB[1], _P_VB[1] + 2 * D_MODEL)
_P_COLS = _P_SG[1]


def _proj_body(x_ref, xprev_ref, xnext_ref, nw_ref, w_ref, cw_ref, alog_ref, dtb_ref,
               qkv_ref, bg_ref, z_ref, qb_ref, kb_ref, vb_ref, sg_ref, *, tm, seq_tiles):
    i = pl.program_id(0)
    first = jnp.bool_(False)
    last = jnp.bool_(False)
    for start, per_seq in seq_tiles:
        rel = i - start
        first = first | ((rel >= 0) & (rel % per_seq == 0))
        last = last | ((rel >= 0) & (rel % per_seq == per_seq - 1))
    x_ext = jnp.concatenate([xprev_ref[...], x_ref[...], xnext_ref[...]], axis=0)
    u_ext = _rms(x_ext, nw_ref[...]).astype(BF16)
    u = u_ext[HALO:HALO + tm]

    def seg(span):
        return _dot(u, w_ref[:, span[0]:span[1]])

    z_ref[...] = seg(_P_Z)
    qb_ref[...] = seg(_P_QB).astype(BF16)
    kb_ref[...] = seg(_P_KB).astype(BF16)
    vb_ref[...] = seg(_P_VB).astype(BF16)
    sg_ref[...] = jax.nn.sigmoid(seg(_P_SG))

    ext = _dot(u_ext, w_ref[:, _P_QKVA[0]:_P_QKVA[1]])
    row = lax.broadcasted_iota(jnp.int32, (tm + 2 * HALO, 1), 0)
    outside = (first & (row < HALO)) | (last & (row >= HALO + tm))
    ext = jnp.where(outside, 0.0, ext)
    pad = (GDN_CONV - 1) // 2
    acc = None
    for k in range(GDN_CONV):
        lo = HALO - pad + k
        term = ext[lo:lo + tm, :] * cw_ref[k:k + 1, :]
        acc = term if acc is None else acc + term
    y = acc * jax.nn.sigmoid(acc)
    for h in range(2 * GDN_HEADS):
        lo = h * GDN_DK
        yh = y[:, lo:lo + GDN_DK]
        nrm = yh * lax.rsqrt(jnp.sum(yh * yh, axis=-1, keepdims=True) + EPS)
        if h < GDN_HEADS:
            nrm = nrm * (GDN_DK ** -0.5)
        qkv_ref[:, lo:lo + GDN_DK] = nrm
    qkv_ref[:, 2 * GDN_QK:] = y[:, 2 * GDN_QK:]
    gl = seg(_P_GATE)
    beta = jax.nn.sigmoid(gl)
    xa = gl + dtb_ref[...]
    softplus = jnp.maximum(xa, 0.0) + jnp.log1p(jnp.exp(-jnp.abs(xa)))
    g = -jnp.exp(alog_ref[...]) * softplus
    lane = lax.broadcasted_iota(jnp.int32, gl.shape, 1)
    bg_ref[...] = jnp.where(lane < N_GATES, beta, g)


def _proj(x, nw, w, conv_w8, alog_pad, dtb_pad, tm, seq_tiles):
    t, d = x.shape
    hb = tm // HALO
    nblk = t // HALO
    widths = [(_P_QKVA, F32), (_P_GATE, F32), (_P_Z, F32), (_P_QB, BF16), (_P_KB, BF16),
              (_P_VB, BF16), (_P_SG, F32)]
    out_shape = [jax.ShapeDtypeStruct((t, s[1] - s[0]), dt) for s, dt in widths]
    out_specs = [pl.BlockSpec((tm, s[1] - s[0]), lambda i: (i, 0)) for s, _ in widths]
    return pl.pallas_call(
        functools.partial(_proj_body, tm=tm, seq_tiles=seq_tiles),
        grid=(t // tm,),
        in_specs=[pl.BlockSpec((tm, d), lambda i: (i, 0)),
                  pl.BlockSpec((HALO, d), lambda i: (jnp.maximum(i * hb - 1, 0), 0)),
                  pl.BlockSpec((HALO, d), lambda i: (jnp.minimum((i + 1) * hb, nblk - 1), 0)),
                  _resident((1, d)), _resident(w.shape), _resident(conv_w8.shape),
                  _resident((1, V7X_LANES)), _resident((1, V7X_LANES))],
        out_specs=out_specs,
        out_shape=out_shape,
        compiler_params=_cparams(("parallel",)),
        name="proj",
    )(x, x, x, nw, w, conv_w8, alog_pad, dtb_pad)


def _gdn_body(qkv_f, bgc_f, bgr_f, qkv_b, bgc_b, bgr_b, of_ref, ob_ref, s_ref, *, n_chunks):
    c64 = GDN_CHUNK

    @pl.when(pl.program_id(1) == 0)
    def _():
        s_ref[...] = jnp.zeros_like(s_ref)

    row = lax.broadcasted_iota(jnp.int32, (c64, c64), 0)
    col = lax.broadcasted_iota(jnp.int32, (c64, c64), 1)
    eye = (row == col).astype(F32)
    incl = (row >= col, row <= col)
    strict = (row > col, row < col)
    same_block = {}
    size = GDN_BASE
    while size <= c64:
        same_block[size] = (row // size) == (col // size)
        size *= 2

    refs = ((qkv_f, bgc_f, bgr_f, of_ref), (qkv_b, bgc_b, bgr_b, ob_ref))
    chains = [(d, h) for d in range(2) for h in range(GDN_HEADS)]

    def row0(c, d):
        return (c if d == 0 else n_chunks - 1 - c) * c64

    def prepare(cs):
        gates = {}
        for c in cs:
            for d in range(2):
                _, bgc_ref, bgr_ref, _ = refs[d]
                r0 = row0(c, d)
                gt = bgc_ref[r0:r0 + c64, :]
                gcs = _dot(incl[d].astype(F32), gt, HIGHEST)
                grs = _dot(bgr_ref[:, r0:r0 + c64], incl[1 - d].astype(F32), HIGHEST)
                g_last = gcs[c64 - 1:c64, :] if d == 0 else gcs[0:1, :]
                gates[c, d] = (gt, gcs, grs, jnp.exp(gcs), jnp.exp(g_last - gcs),
                               jnp.exp(g_last))
        q, k, k16, kb, vb, gam, egc_c, ekd_c, egl_c = ([] for _ in range(9))
        jobs = [(c, d, h) for c in cs for d, h in chains]
        for c, d, h in jobs:
            qkv_ref = refs[d][0]
            r0 = row0(c, d)
            gt, gcs, grs, egc, ekd, egl = gates[c, d]
            idx = d * GDN_HEADS + h
            gi = N_GATES + idx
            qq = qkv_ref[r0:r0 + c64, h * GDN_DK:(h + 1) * GDN_DK]
            kk_ = qkv_ref[r0:r0 + c64, GDN_QK + h * GDN_DK:GDN_QK + (h + 1) * GDN_DK]
            vv = qkv_ref[r0:r0 + c64, 2 * GDN_QK + h * GDN_DV:2 * GDN_QK + (h + 1) * GDN_DV]
            beta = gt[:, idx:idx + 1]
            diff = gcs[:, gi:gi + 1] - grs[gi:gi + 1, :]
            gam.append(jnp.where(incl[d], jnp.exp(jnp.where(incl[d], diff, 0.0)), 0.0))
            q.append(qq)
            k.append(kk_)
            k16.append(kk_.astype(BF16))
            kb.append(kk_ * beta)
            vb.append(vv * beta)
            egc_c.append(egc[:, gi:gi + 1])
            ekd_c.append(ekd[:, gi:gi + 1])
            egl_c.append(egl[:, gi:gi + 1])
        n = len(jobs)
        both = [_dot_nt(jnp.concatenate([kb[i].astype(BF16), q[i].astype(BF16)], axis=0), k16[i])
                for i in range(n)]
        kk = [both[i][:c64] for i in range(n)]
        qk = [both[i][c64:] for i in range(n)]
        nn = [jnp.where(strict[jobs[i][1]], kk[i] * gam[i], 0.0) for i in range(n)]
        m = [-jnp.where(same_block[GDN_BASE], nn[i], 0.0) for i in range(n)]
        x = [eye + m[i] for i in range(n)]
        for _ in range(2):
            m = [_dot_split(m[i], m[i]) for i in range(n)]
            x = [x[i] + _dot_split(x[i], m[i]) for i in range(n)]
        size = GDN_BASE
        while size < c64:
            e = [jnp.where(same_block[2 * size] & ~same_block[size], nn[i], 0.0)
                 for i in range(n)]
            ex = [_dot_split(e[i], x[i]) for i in range(n)]
            x = [x[i] - _dot_split(x[i], ex[i]) for i in range(n)]
            size *= 2
        rhs = [jnp.concatenate([vb[i], kb[i] * egc_c[i]], axis=1) for i in range(n)]
        sol = [_dot_split(x[i], rhs[i]) for i in range(n)]
        per_chunk = len(chains)
        pick = lambda vals, j: vals[j * per_chunk:(j + 1) * per_chunk]
        return [dict(
            u=[sol[i][:, :GDN_DV] for i in pick(range(n), j)],
            w=[sol[i][:, GDN_DV:].astype(BF16) for i in pick(range(n), j)],
            aqk=[(qk[i] * gam[i]).astype(BF16) for i in pick(range(n), j)],
            qd=[(q[i] * egc_c[i]).astype(BF16) for i in pick(range(n), j)],
            kd=[(k[i] * ekd_c[i]).astype(BF16) for i in pick(range(n), j)],
            egl=pick(egl_c, j)) for j in range(len(cs))]

    def advance(c, pre):
        n = len(chains)
        s = [s_ref[i] for i in range(n)]
        s16 = [s[i].astype(BF16) for i in range(n)]
        wq = [_dot(jnp.concatenate([pre["w"][i], pre["qd"][i]], axis=0), s16[i]) for i in range(n)]
        ws = [wq[i][:c64] for i in range(n)]
        qs = [wq[i][c64:] for i in range(n)]
        vn16 = [(pre["u"][i] - ws[i]).astype(BF16) for i in range(n)]
        av = [_dot(pre["aqk"][i], vn16[i]) for i in range(n)]
        kv = [_dot_tn(pre["kd"][i], vn16[i]) for i in range(n)]
        for i, (d, h) in enumerate(chains):
            r0 = row0(c, d)
            s_ref[i] = s[i] * pre["egl"][i] + kv[i]
            refs[d][3][r0:r0 + c64, h * GDN_DV:(h + 1) * GDN_DV] = qs[i] + av[i]

    groups = [list(range(c, min(c + GDN_GROUP, n_chunks))) for c in range(0, n_chunks, GDN_GROUP)]
    pre = prepare(groups[0])
    for gi, cs in enumerate(groups):
        nxt = prepare(groups[gi + 1]) if gi + 1 < len(groups) else None
        for c, one in zip(cs, pre):
            advance(c, one)
        pre = nxt


def _gdn(qkvn, bg, bg_rows, row_off, batch, seqlen, ts):
    c = qkvn.shape[1]
    ns = seqlen // ts
    off = row_off // ts
    fwd = lambda b, s: (off + b * ns + s, 0)
    bwd = lambda b, s: (off + b * ns + ns - 1 - s, 0)
    fwd_r = lambda b, s: (0, off + b * ns + s)
    bwd_r = lambda b, s: (0, off + b * ns + ns - 1 - s)
    nr = bg_rows.shape[0]
    out = jax.ShapeDtypeStruct((batch * seqlen, GDN_V), F32)
    return pl.pallas_call(
        functools.partial(_gdn_body, n_chunks=ts // GDN_CHUNK),
        grid=(batch, ns),
        in_specs=[
            pl.BlockSpec((ts, c), fwd), pl.BlockSpec((ts, V7X_LANES), fwd),
            pl.BlockSpec((nr, ts), fwd_r),
            pl.BlockSpec((ts, c), bwd), pl.BlockSpec((ts, V7X_LANES), bwd),
            pl.BlockSpec((nr, ts), bwd_r),
        ],
        out_specs=[pl.BlockSpec((ts, GDN_V), lambda b, s: (b * ns + s, 0)),
                   pl.BlockSpec((ts, GDN_V), lambda b, s: (b * ns + ns - 1 - s, 0))],
        out_shape=[out, out],
        scratch_shapes=[pltpu.VMEM((2 * GDN_HEADS, GDN_DK, GDN_DV), F32)],
        compiler_params=_cparams(("parallel", "arbitrary")),
        name="gdn",
    )(qkvn, bg, bg_rows, qkvn, bg, bg_rows)


def _attn_body(far_ref, q_ref, k_ref, vt_ref, band_ref, lam_ref, nw_ref, o_ref,
               qz_ref, sf_ref, mxf_ref, sb_ref, mxb_ref, m_ref, acc_ref, *, lam_init, tk, nk):
    far_buf = (sf_ref, mxf_ref)
    band_buf = (sb_ref, mxb_ref)
    h = pl.program_id(1)
    qi = pl.program_id(2)
    tq = q_ref.shape[0]
    sub = V7X_SUBLANES

    m_ref[...] = jnp.full_like(m_ref, -1e30)
    acc_ref[...] = jnp.zeros_like(acc_ref)
    q = q_ref[...]
    lane = lax.broadcasted_iota(jnp.int32, q.shape, 1)
    zero = jnp.zeros_like(q)
    qz_ref[0] = jnp.where(lane < DIFF_DH, q, zero)
    qz_ref[1] = jnp.where(lane >= DIFF_DH, q, zero)

    def scores(kt, buf, slot, bias_tile):
        s_ref, mx_ref = buf
        k = k_ref[pl.ds(pl.multiple_of(kt * tk, tk), tk), :]
        for mp in range(2):
            s = _dot_nt(k, qz_ref[mp])
            if bias_tile is not None:
                s = s + bias_tile
            s_ref[slot, mp] = s
            mx_ref[slot, mp] = jnp.max(s.reshape(tk // sub, sub, tq), axis=0)

    def accumulate(kt, buf, slot, c):
        s_ref, mx_ref = buf
        vt = vt_ref[:, pl.ds(pl.multiple_of(kt * tk, tk), tk)]
        for mp in range(2):
            m_cur = jnp.max(mx_ref[slot, mp], axis=0, keepdims=True) + c
            m_prev = m_ref[mp]
            m_new = jnp.maximum(m_prev, m_cur)
            alpha = jnp.exp2(m_prev - m_new)
            m_ref[mp] = m_new
            p = jnp.exp2(s_ref[slot, mp] - (m_new - c)).astype(BF16)
            acc_ref[mp] = alpha * acc_ref[mp] + _dot(vt, p)

    n_left = jnp.maximum(qi - 1, 0)
    right0 = jnp.minimum(qi + 2, nk)
    n_far = n_left + (nk - right0)

    def far_tile(f):
        return jnp.where(f < n_left, f, right0 + (f - n_left))

    def far_const(f):
        return jnp.where(f < n_left, far_ref[h, 0], far_ref[h, 1])

    band = [(qi, 1), (qi - 1, 0), (qi + 1, 2)]

    def band_scores(j):
        kt, dd = band[j]
        scores(jnp.clip(kt, 0, nk - 1), band_buf, j, band_ref[dd])

    def band_accumulate(j):
        kt, _ = band[j]
        valid = (kt >= 0) & (kt < nk)
        accumulate(jnp.clip(kt, 0, nk - 1), band_buf, j, jnp.where(valid, 0.0, -1e30))

    if nk >= 4:
        scores(far_tile(0), far_buf, 0, None)

        def far_step(f, slot):
            scores(far_tile(f + 1), far_buf, 1 - slot, None)
            accumulate(far_tile(f), far_buf, slot, far_const(f))

        n_steps = n_far - 1

        def far_trip(g, carry):
            for u in range(FAR_UNROLL):
                far_step(FAR_UNROLL * g + u, u % 2)
            return carry

        lax.fori_loop(0, n_steps // FAR_UNROLL, far_trip, 0)
        done = (n_steps // FAR_UNROLL) * FAR_UNROLL
        size = FAR_UNROLL // 2
        while size >= 1:
            taken = (n_steps % (2 * size)) >= size

            @pl.when(taken)
            def _(done=done, size=size):
                for u in range(size):
                    far_step(done + u, u % 2)

            done = done + jnp.where(taken, size, 0)
            size //= 2

        band_scores(0)
        accumulate(far_tile(n_far - 1), far_buf, (n_far - 1) % 2, far_const(n_far - 1))
        band_scores(1)
        band_accumulate(0)
        band_scores(2)
        band_accumulate(1)
        band_accumulate(2)
    else:
        for j in range(3):
            band_scores(j)
            band_accumulate(j)

    lam = lam_ref[...]
    lam_full = (jnp.exp(jnp.sum(lam[0:1] * lam[1:2], keepdims=True))
                - jnp.exp(jnp.sum(lam[2:3] * lam[3:4], keepdims=True)) + lam_init)
    a0 = acc_ref[0]
    a1 = acc_ref[1]
    o = (a0[:DIFF_DV] / a0[DIFF_DV:DIFF_DV + 1]
         - lam_full * (a1[:DIFF_DV] / a1[DIFF_DV:DIFF_DV + 1]))
    o = o * lax.rsqrt(jnp.mean(o * o, axis=0, keepdims=True) + EPS)
    o = o * (nw_ref[...] * (1.0 - lam_init))
    o_ref[...] = o.T


def _attn(qb, kb, vbt, band, far, lam, nw_col, row_off, batch, seqlen, tq, lam_init):
    nq = seqlen // tq
    off = row_off // tq
    off_seq = row_off // seqlen
    assert row_off % seqlen == 0
    return pl.pallas_call(
        functools.partial(_attn_body, lam_init=lam_init, tk=tq, nk=nq),
        grid=(batch, DIFF_HEADS, nq),
        in_specs=[
            pl.BlockSpec(memory_space=pltpu.SMEM),
            pl.BlockSpec((tq, DIFF_DV), lambda b, h, qi: (off + b * nq + qi, h)),
            pl.BlockSpec((seqlen, DIFF_DV), lambda b, h, qi: (off_seq + b, h)),
            pl.BlockSpec((VT_ROWS, seqlen), lambda b, h, qi: (h, off_seq + b)),
            pl.BlockSpec((None, 3, tq, tq), lambda b, h, qi: (h, 0, 0, 0)),
            pl.BlockSpec(lam.shape, lambda b, h, qi: (0, 0)),
            pl.BlockSpec((DIFF_DV, 1), lambda b, h, qi: (0, 0)),
        ],
        out_specs=pl.BlockSpec((tq, DIFF_DV), lambda b, h, qi: (b * nq + qi, h)),
        out_shape=jax.ShapeDtypeStruct((batch * seqlen, DIFF_VW), F32),
        scratch_shapes=[pltpu.VMEM((2, tq, DIFF_DV), BF16),
                        pltpu.VMEM((2, 2, tq, tq), F32),
                        pltpu.VMEM((2, 2, V7X_SUBLANES, tq), F32),
                        pltpu.VMEM((3, 2, tq, tq), F32),
                        pltpu.VMEM((3, 2, V7X_SUBLANES, tq), F32),
                        pltpu.VMEM((2, 1, tq), F32),
                        pltpu.VMEM((2, VT_ROWS, tq), F32)],
        compiler_params=_cparams(("parallel", "parallel", "parallel")),
        name="diffattn",
    )(far, qb, kb, vbt, band, lam, nw_col)


def _memkv_body(m_ref, nw_ref, w_ref, o_ref):
    h = _rms(m_ref[...], nw_ref[...]).astype(BF16)
    o_ref[...] = _dot(h, w_ref[...]).astype(BF16)


def _memkv(mem, nw, wkv):
    nb, nm, d = mem.shape
    return pl.pallas_call(
        _memkv_body,
        grid=(nb,),
        in_specs=[pl.BlockSpec((None, nm, d), lambda b: (b, 0, 0)), _resident((1, d)),
                  _resident(wkv.shape)],
        out_specs=pl.BlockSpec((None, nm, wkv.shape[1]), lambda b: (b, 0, 0)),
        out_shape=jax.ShapeDtypeStruct((nb, nm, wkv.shape[1]), BF16),
        compiler_params=_cparams(("parallel",)),
        name="memkv",
    )(mem, nw, wkv)


def _mergex_body(x_ref, z_ref, sg_ref, kv_ref, gn_ref, wua_ref, wub_ref, wout_ref, xn_ref,
                 wq_ref, wo_ref, *rest, group_tiles):
    o_ref = rest[-1]
    i = pl.program_id(0)
    o = oattn = None
    for g, first in enumerate(group_tiles):
        of_ref, ob_ref, oattn_ref = rest[3 * g:3 * g + 3]
        og = of_ref[...] + ob_ref[...]
        ag = oattn_ref[...]
        o = og if o is None else jnp.where(i >= first, og, o)
        oattn = ag if oattn is None else jnp.where(i >= first, ag, oattn)
    z = z_ref[...]
    gn = gn_ref[...]
    heads = []
    for h in range(GDN_HEADS):
        sl = slice(h * GDN_DV, (h + 1) * GDN_DV)
        zh = z[:, sl]
        heads.append(_rms(o[:, sl], gn) * (zh * jax.nn.sigmoid(zh)))
    oa = jnp.concatenate(heads, axis=1).astype(BF16)
    ya = _dot(oa, wua_ref[...])
    yb = _dot(oattn.astype(BF16), wub_ref[...])
    sg = sg_ref[...]
    merged = sg[:, :D_MODEL] * ya + sg[:, D_MODEL:] * yb
    x = x_ref[...] + _dot(merged.astype(BF16), wout_ref[...])
    hq = _rms(x, xn_ref[...]).astype(BF16)
    q = _dot(hq, wq_ref[...]) * (X_DH ** -0.5)
    kv = kv_ref[...]
    outs = []
    for h in range(X_HEADS):
        sl = slice(h * X_DH, (h + 1) * X_DH)
        kh = kv[:, sl]
        vh = kv[:, X_HEADS * X_DH + h * X_DH:X_HEADS * X_DH + (h + 1) * X_DH]
        s = _dot_nt(q[:, sl].astype(BF16), kh)
        s = s - jnp.max(s, axis=-1, keepdims=True)
        p = jnp.exp(s)
        p = p / jnp.sum(p, axis=-1, keepdims=True)
        outs.append(_dot(p.astype(BF16), vh))
    ox = jnp.concatenate(outs, axis=1).astype(BF16)
    o_ref[...] = x + _dot(ox, wo_ref[...])


def _mergex(x, z, sg, kv, gn, wua, wub, wout, xn, wq, wo, branches, group_rows, tm, tile_batch):
    t, d = x.shape
    rows = lambda width: pl.BlockSpec((tm, width), lambda i: (i, 0))
    in_specs = [rows(d), rows(GDN_V), rows(2 * d),
                pl.BlockSpec((None,) + kv.shape[1:], lambda i: (tile_batch(i), 0, 0)),
                _resident(gn.shape), _resident(wua.shape), _resident(wub.shape),
                _resident(wout.shape), _resident(xn.shape), _resident(wq.shape),
                _resident(wo.shape)]
    args = [x, z, sg, kv, gn, wua, wub, wout, xn, wq, wo]
    for (first_row, n_rows), group in zip(group_rows, branches):
        first, count = first_row // tm, n_rows // tm
        local = lambda i, first=first, count=count: (jnp.clip(i - first, 0, count - 1), 0)
        for arr in group:
            in_specs.append(pl.BlockSpec((tm, arr.shape[1]), local))
            args.append(arr)
    return pl.pallas_call(
        functools.partial(_mergex_body, group_tiles=tuple(r // tm for r, _ in group_rows)),
        grid=(t // tm,),
        in_specs=in_specs,
        out_specs=rows(d),
        out_shape=jax.ShapeDtypeStruct((t, d), F32),
        compiler_params=_cparams(("parallel",)),
        name="mergex",
    )(*args)


def _rel_bucket(rel):
    nb = N_BUCKETS // 2
    ret = jnp.where(rel > 0, nb, 0)
    n = jnp.abs(rel)
    max_exact = nb // 2
    nf = jnp.maximum(n, 1).astype(F32)
    large = max_exact + (jnp.log(nf / max_exact) / math.log(MAX_DISTANCE / max_exact)
                         * (nb - max_exact)).astype(jnp.int32)
    large = jnp.minimum(large, nb - 1)
    return ret + jnp.where(n < max_exact, n, large)


def _toeplitz_body(w_ref, o_ref):
    tq = o_ref.shape[-1]
    x = jnp.broadcast_to(w_ref[...], (tq, 2 * tq))
    o_ref[...] = pltpu.roll(x, 0, 1, stride=1, stride_axis=0)[:, :tq]


def _bias_tables(rel_bias, tq):
    assert tq >= MAX_DISTANCE
    rel = jnp.arange(-(2 * tq - 1), 2 * tq, dtype=jnp.int32)
    by_rel = (rel_bias[_rel_bucket(rel)].astype(F32) * LOG2E).T
    nh = by_rel.shape[0]
    gens = []
    for dd in range(3):
        lo = dd * tq
        gens.append(jnp.concatenate(
            [by_rel[:, lo:lo + tq][:, ::-1], jnp.zeros((nh, 1), F32),
             by_rel[:, lo + tq:lo + 2 * tq - 1][:, ::-1]], axis=1))
    gen = jnp.stack(gens, axis=1)[:, :, None, :]
    band = pl.pallas_call(
        _toeplitz_body,
        grid=(nh, 3),
        in_specs=[pl.BlockSpec((None, None, 1, 2 * tq), lambda h, d: (h, d, 0, 0))],
        out_specs=pl.BlockSpec((None, None, tq, tq), lambda h, d: (h, d, 0, 0)),
        out_shape=jax.ShapeDtypeStruct((nh, 3, tq, tq), F32),
        compiler_params=_cparams(("parallel", "parallel")),
        name="toeplitz",
    )(gen)
    far = jnp.stack([by_rel[:, 0], by_rel[:, -1]], axis=1)
    return band, far


def _pick(limit, n):
    tile = limit
    while n % tile:
        tile //= 2
    return tile


def _encode(xs, mems, p, tile_limits=None):
    lim = dict(ffn=512, proj=256, gdn=512, attn=512, mergex=512)
    if tile_limits:
        lim.update(tile_limits)
    depth = p["w_in"].shape[0]
    d = D_MODEL
    groups = []
    row = 0
    bat = 0
    for x in xs:
        b, l, _ = x.shape
        groups.append(dict(b=b, l=l, row=row, bat=bat))
        row += b * l
        bat += b
    t_all = row
    seqlens = [g["l"] for g in groups]
    common = functools.reduce(math.gcd, seqlens)
    tiles = {k: _pick(v, common) for k, v in lim.items()}

    x = jnp.concatenate([xx.reshape(-1, d) for xx in xs], axis=0)
    mem = jnp.concatenate(mems, axis=0)

    def tile_batch(i):
        r0 = i * tiles["mergex"]
        bidx = 0
        for g in groups:
            bidx = jnp.where(r0 >= g["row"], g["bat"] + (r0 - g["row"]) // g["l"], bidx)
        return bidx

    seq_tiles = tuple((g["row"] // tiles["proj"], g["l"] // tiles["proj"]) for g in groups)
    band, far = _bias_tables(p["rel_bias"], tiles["attn"])

    def ffn_weights(prefix, i):
        return (p[prefix + "_norm"][i].reshape(1, d), p[prefix + "_w_gate"][i].astype(BF16),
                p[prefix + "_w_up"][i].astype(BF16), p[prefix + "_w_down"][i].astype(BF16))

    for i in range(depth):
        x = _ffn(x, *ffn_weights("ffn1", i), None, tiles["ffn"])

        w_in = p["w_in"][i]
        o_beta = 3 * GDN_QK
        o_z = o_beta + 2 * N_GATES
        o_qb = o_z + GDN_V
        gate_w = jnp.pad(w_in[:, o_beta:o_z], ((0, 0), (0, V7X_LANES - 2 * N_GATES)))
        w_perm = jnp.concatenate(
            [w_in[:, :o_beta], gate_w, w_in[:, o_z:o_qb],
             w_in[:, o_qb:o_qb + DIFF_QK] * (DIFF_DH ** -0.5 * LOG2E),
             w_in[:, o_qb + DIFF_QK:]],
            axis=1).astype(BF16)
        assert w_perm.shape[1] == _P_COLS
        conv_w8 = jnp.pad(p["conv_w"][i], ((0, HALO - GDN_CONV), (0, 0)))
        lane_pad = (N_GATES, V7X_LANES - 2 * N_GATES)
        alog_pad = jnp.pad(p["gdn_a_log"][i].reshape(-1), lane_pad).reshape(1, V7X_LANES)
        dtb_pad = jnp.pad(p["gdn_dt_bias"][i].reshape(-1), lane_pad).reshape(1, V7X_LANES)
        qkvn, bg, z, qb, kb, vb, sg = _proj(x, p["mix_norm"][i].reshape(1, d), w_perm, conv_w8,
                                            alog_pad, dtb_pad, tiles["proj"], seq_tiles)
        bg_rows = bg[:, :2 * N_GATES].T

        lam_init = 0.8 - 0.6 * math.exp(-0.3 * i)
        vbt = vb.T.reshape(DIFF_HEADS, DIFF_DV, t_all)
        ones_pad = jnp.zeros((DIFF_HEADS, VT_ROWS - DIFF_DV, t_all), BF16).at[:, 0].set(1.0)
        vbt = jnp.concatenate([vbt, ones_pad], axis=1).reshape(DIFF_HEADS * VT_ROWS, t_all)
        branches = []
        for g in groups:
            o_f, o_b = _gdn(qkvn, bg, bg_rows, g["row"], g["b"], g["l"], tiles["gdn"])
            o_attn = _attn(qb, kb, vbt, band, far, p["diff_lambda"][i],
                           p["diff_norm"][i].reshape(DIFF_DV, 1), g["row"], g["b"], g["l"],
                           tiles["attn"], lam_init)
            branches.append((o_f, o_b, o_attn))

        kv = _memkv(mem, p["mem_norm"][i].reshape(1, d), p["xattn_wkv"][i].astype(BF16))
        x = _mergex(x, z, sg, kv, p["gdn_norm"][i].reshape(1, GDN_DV),
                    p["w_up_a"][i].astype(BF16), p["w_up_b"][i].astype(BF16),
                    p["w_out"][i].astype(BF16), p["xattn_norm"][i].reshape(1, d),
                    p["xattn_wq"][i].astype(BF16), p["xattn_wo"][i].astype(BF16),
                    branches, [(g["row"], g["b"] * g["l"]) for g in groups],
                    tiles["mergex"], tile_batch)

        final_w = p["final_norm"].reshape(1, d) if i == depth - 1 else None
        x = _ffn(x, *ffn_weights("ffn2", i), final_w, tiles["ffn"])

    return tuple(x[g["row"]:g["row"] + g["b"] * g["l"]].reshape(g["b"], g["l"], d)
                 for g in groups)


def kernel(x_prompt, x_sample, mem_prompt, mem_sample, ffn1_norm, ffn1_w_gate, ffn1_w_up, ffn1_w_down, mix_norm, w_in, conv_w, gdn_a_log, gdn_dt_bias, gdn_norm, w_up_a, diff_lambda, diff_norm, w_up_b, w_out, rel_bias, xattn_norm, mem_norm, xattn_wq, xattn_wkv, xattn_wo, ffn2_norm, ffn2_w_gate, ffn2_w_up, ffn2_w_down, final_norm):
    params = dict(
        ffn1_norm=ffn1_norm, ffn1_w_gate=ffn1_w_gate, ffn1_w_up=ffn1_w_up,
        ffn1_w_down=ffn1_w_down, mix_norm=mix_norm, w_in=w_in, conv_w=conv_w,
        gdn_a_log=gdn_a_log, gdn_dt_bias=gdn_dt_bias, gdn_norm=gdn_norm, w_up_a=w_up_a,
        diff_lambda=diff_lambda, diff_norm=diff_norm, w_up_b=w_up_b, w_out=w_out,
        rel_bias=rel_bias, xattn_norm=xattn_norm, mem_norm=mem_norm, xattn_wq=xattn_wq,
        xattn_wkv=xattn_wkv, xattn_wo=xattn_wo, ffn2_norm=ffn2_norm,
        ffn2_w_gate=ffn2_w_gate, ffn2_w_up=ffn2_w_up, ffn2_w_down=ffn2_w_down,
        final_norm=final_norm)
    y_prompt, y_sample = _encode((x_prompt, x_sample), (mem_prompt, mem_sample), params)
    return (y_prompt, y_sample)
```

```python
import functools
import math

import jax
import jax.numpy as jnp
from jax import lax
from jax.experimental import pallas as pl
from jax.experimental.pallas import tpu as pltpu

F32 = jnp.float32
BF16 = jnp.bfloat16
HIGHEST = lax.Precision.HIGHEST

EPS = 1e-6
LOG2E = math.log2(math.e)
D_MODEL = 1024
N_MEM = 256
GDN_HEADS = 4
GDN_DK = 128
GDN_DV = 128
GDN_CONV = 5
GDN_CHUNK = 64
GDN_BASE = 8
GDN_GROUP = 2
DIFF_HEADS = 8
DIFF_DH = 64
DIFF_DV = 2 * DIFF_DH
N_BUCKETS = 32
MAX_DISTANCE = 128
X_HEADS = 4
X_DH = 128
D_FF = 2816
GDN_QK = GDN_HEADS * GDN_DK
GDN_V = GDN_HEADS * GDN_DV
DIFF_QK = DIFF_HEADS * 2 * DIFF_DH
DIFF_VW = DIFF_HEADS * DIFF_DV
N_GATES = 2 * GDN_HEADS
VT_ROWS = DIFF_DV + 16

V7X_LANES = 128
V7X_SUBLANES = 8
V7X_VMEM_LIMIT = 56 * 1024 * 1024

FAR_UNROLL = 8
HALO = V7X_SUBLANES


def _cparams(sem):
    return pltpu.CompilerParams(dimension_semantics=sem, vmem_limit_bytes=V7X_VMEM_LIMIT)


def _dot(a, b, precision=None):
    return jnp.dot(a, b, preferred_element_type=F32, precision=precision)


def _dot_nt(a, b, precision=None):
    return lax.dot_general(a, b, (((1,), (1,)), ((), ())),
                           preferred_element_type=F32, precision=precision)


def _dot_tn(a, b, precision=None):
    return lax.dot_general(a, b, (((0,), (0,)), ((), ())),
                           preferred_element_type=F32, precision=precision)


def _dot_split(a, b):
    a_hi = a.astype(BF16)
    b_hi = b.astype(BF16)
    a_lo = (a - a_hi.astype(F32)).astype(BF16)
    b_lo = (b - b_hi.astype(F32)).astype(BF16)
    m = a.shape[0]
    both = _dot(jnp.concatenate([a_hi, a_lo], axis=0), b_hi)
    return both[:m] + (both[m:] + _dot(a_hi, b_lo))


def _rms(x, w):
    return x * lax.rsqrt(jnp.mean(x * x, axis=-1, keepdims=True) + EPS) * w


def _resident(shape):
    nd = len(shape)
    return pl.BlockSpec(shape, lambda *_: (0,) * nd, pipeline_mode=pl.Buffered(1))


def _ffn_body(x_ref, nw_ref, wg_ref, wu_ref, wd_ref, *rest, final):
    o_ref = rest[-1]
    x = x_ref[...]
    h = _rms(x, nw_ref[...]).astype(BF16)
    g = _dot(h, wg_ref[...])
    u = _dot(h, wu_ref[...])
    a = (g * jax.nn.sigmoid(g) * u).astype(BF16)
    y = x + 0.5 * _dot(a, wd_ref[...])
    if final:
        y = _rms(y, rest[0][...])
    o_ref[...] = y


def _ffn(x, nw, wg, wu, wd, final_w, tm):
    t, d = x.shape
    row = pl.BlockSpec((tm, d), lambda i: (i, 0))
    in_specs = [row, _resident((1, d)), _resident(wg.shape), _resident(wu.shape),
                _resident(wd.shape)]
    args = [x, nw, wg, wu, wd]
    if final_w is not None:
        in_specs.append(_resident((1, d)))
        args.append(final_w)
    return pl.pallas_call(
        functools.partial(_ffn_body, final=final_w is not None),
        grid=(t // tm,),
        in_specs=in_specs,
        out_specs=row,
        out_shape=jax.ShapeDtypeStruct((t, d), F32),
        compiler_params=_cparams(("parallel",)),
        name="ffn",
    )(*args)


_P_QKVA = (0, 3 * GDN_QK)
_P_GATE = (_P_QKVA[1], _P_QKVA[1] + V7X_LANES)
_P_Z = (_P_GATE[1], _P_GATE[1] + GDN_V)
_P_QB = (_P_Z[1], _P_Z[1] + DIFF_QK)
_P_KB = (_P_QB[1], _P_QB[1] + DIFF_QK)
_P_VB = (_P_KB[1], _P_KB[1] + DIFF_VW)
_P_SG = (_P_VB[1], _P_VB[1] + 2 * D_MODEL)
_P_COLS = _P_SG[1]


def _proj_body(x_ref, xprev_ref, xnext_ref, nw_ref, w_ref, cw_ref, alog_ref, dtb_ref,
               qkv_ref, bg_ref, z_ref, qb_ref, kb_ref, vb_ref, sg_ref, *, tm, seq_tiles):
    i = pl.program_id(0)
    first = jnp.bool_(False)
    last = jnp.bool_(False)
    for start, per_seq in seq_tiles:
        rel = i - start
        first = first | ((rel >= 0) & (rel % per_seq == 0))
        last = last | ((rel >= 0) & (rel % per_seq == per_seq - 1))
    x_ext = jnp.concatenate([xprev_ref[...], x_ref[...], xnext_ref[...]], axis=0)
    u_ext = _rms(x_ext, nw_ref[...]).astype(BF16)
    u = u_ext[HALO:HALO + tm]

    def seg(span):
        return _dot(u, w_ref[:, span[0]:span[1]])

    row = lax.broadcasted_iota(jnp.int32, (tm + 2 * HALO, 1), 0)
    outside = (first & (row < HALO)) | (last & (row >= HALO + tm))
    pad = (GDN_CONV - 1) // 2

    def conv_block(blk):
        width = 2 * GDN_DK
        lo = blk * width
        ext = _dot(u_ext, w_ref[:, _P_QKVA[0] + lo:_P_QKVA[0] + lo + width])
        ext = jnp.where(outside, 0.0, ext)
        acc = None
        for k in range(GDN_CONV):
            r0 = HALO - pad + k
            term = ext[r0:r0 + tm, :] * cw_ref[k:k + 1, lo:lo + width]
            acc = term if acc is None else acc + term
        y = acc * jax.nn.sigmoid(acc)
        for half in range(2):
            head = 2 * blk + half
            yh = y[:, half * GDN_DK:(half + 1) * GDN_DK]
            if head < 2 * GDN_HEADS:
                yh = yh * lax.rsqrt(jnp.sum(yh * yh, axis=-1, keepdims=True) + EPS)
                if head < GDN_HEADS:
                    yh = yh * (GDN_DK ** -0.5)
            qkv_ref[:, head * GDN_DK:(head + 1) * GDN_DK] = yh

    n_blocks = 3 * GDN_HEADS // 2
    plain = [(z_ref, _P_Z, None), (qb_ref, _P_QB, BF16), (kb_ref, _P_KB, BF16),
             (vb_ref, _P_VB, BF16)]
    for j, (ref, span, dtype) in enumerate(plain):
        for blk in range(j * n_blocks // len(plain), (j + 1) * n_blocks // len(plain)):
            conv_block(blk)
        val = seg(span)
        ref[...] = val if dtype is None else val.astype(dtype)
    sg_ref[...] = jax.nn.sigmoid(seg(_P_SG))
    gl = seg(_P_GATE)
    beta = jax.nn.sigmoid(gl)
    xa = gl + dtb_ref[...]
    softplus = jnp.maximum(xa, 0.0) + jnp.log1p(jnp.exp(-jnp.abs(xa)))
    g = -jnp.exp(alog_ref[...]) * softplus
    lane = lax.broadcasted_iota(jnp.int32, gl.shape, 1)
    bg_ref[...] = jnp.where(lane < N_GATES, beta, g)


def _proj(x, nw, w, conv_w8, alog_pad, dtb_pad, tm, seq_tiles):
    t, d = x.shape
    hb = tm // HALO
    nblk = t // HALO
    widths = [(_P_QKVA, F32), (_P_GATE, F32), (_P_Z, F32), (_P_QB, BF16), (_P_KB, BF16),
              (_P_VB, BF16), (_P_SG, F32)]
    out_shape = [jax.ShapeDtypeStruct((t, s[1] - s[0]), dt) for s, dt in widths]
    out_specs = [pl.BlockSpec((tm, s[1] - s[0]), lambda i: (i, 0)) for s, _ in widths]
    return pl.pallas_call(
        functools.partial(_proj_body, tm=tm, seq_tiles=seq_tiles),
        grid=(t // tm,),
        in_specs=[pl.BlockSpec((tm, d), lambda i: (i, 0)),
                  pl.BlockSpec((HALO, d), lambda i: (jnp.maximum(i * hb - 1, 0), 0)),
                  pl.BlockSpec((HALO, d), lambda i: (jnp.minimum((i + 1) * hb, nblk - 1), 0)),
                  _resident((1, d)), _resident(w.shape), _resident(conv_w8.shape),
                  _resident((1, V7X_LANES)), _resident((1, V7X_LANES))],
        out_specs=out_specs,
        out_shape=out_shape,
        compiler_params=_cparams(("parallel",)),
        name="proj",
    )(x, x, x, nw, w, conv_w8, alog_pad, dtb_pad)


def _gdn_body(qkv_f, bgc_f, bgr_f, qkv_b, bgc_b, bgr_b, of_ref, ob_ref, s_ref, *, n_chunks):
    c64 = GDN_CHUNK

    @pl.when(pl.program_id(1) == 0)
    def _():
        s_ref[...] = jnp.zeros_like(s_ref)

    row = lax.broadcasted_iota(jnp.int32, (c64, c64), 0)
    col = lax.broadcasted_iota(jnp.int32, (c64, c64), 1)
    eye = (row == col).astype(F32)
    incl = (row >= col, row <= col)
    strict = (row > col, row < col)
    same_block = {}
    size = GDN_BASE
    while size <= c64:
        same_block[size] = (row // size) == (col // size)
        size *= 2

    refs = ((qkv_f, bgc_f, bgr_f, of_ref), (qkv_b, bgc_b, bgr_b, ob_ref))
    chains = [(d, h) for d in range(2) for h in range(GDN_HEADS)]

    def row0(c, d):
        return (c if d == 0 else n_chunks - 1 - c) * c64

    def prepare(cs):
        gates = {}
        for c in cs:
            for d in range(2):
                _, bgc_ref, bgr_ref, _ = refs[d]
                r0 = row0(c, d)
                gt = bgc_ref[r0:r0 + c64, :]
                gcs = _dot(incl[d].astype(F32), gt, HIGHEST)
                grs = _dot(bgr_ref[:, r0:r0 + c64], incl[1 - d].astype(F32), HIGHEST)
                g_last = gcs[c64 - 1:c64, :] if d == 0 else gcs[0:1, :]
                gates[c, d] = (gt, gcs, grs, jnp.exp(gcs), jnp.exp(g_last - gcs),
                               jnp.exp(g_last))
        q, k, k16, kb, vb, gam, egc_c, ekd_c, egl_c = ([] for _ in range(9))
        jobs = [(c, d, h) for c in cs for d, h in chains]
        for c, d, h in jobs:
            qkv_ref = refs[d][0]
            r0 = row0(c, d)
            gt, gcs, grs, egc, ekd, egl = gates[c, d]
            idx = d * GDN_HEADS + h
            gi = N_GATES + idx
            qq = qkv_ref[r0:r0 + c64, h * GDN_DK:(h + 1) * GDN_DK]
            kk_ = qkv_ref[r0:r0 + c64, GDN_QK + h * GDN_DK:GDN_QK + (h + 1) * GDN_DK]
            vv = qkv_ref[r0:r0 + c64, 2 * GDN_QK + h * GDN_DV:2 * GDN_QK + (h + 1) * GDN_DV]
            beta = gt[:, idx:idx + 1]
            diff = gcs[:, gi:gi + 1] - grs[gi:gi + 1, :]
            gam.append(jnp.where(incl[d], jnp.exp(jnp.where(incl[d], diff, 0.0)), 0.0))
            q.append(qq)
            k.append(kk_)
            k16.append(kk_.astype(BF16))
            kb.append(kk_ * beta)
            vb.append(vv * beta)
            egc_c.append(egc[:, gi:gi + 1])
            ekd_c.append(ekd[:, gi:gi + 1])
            egl_c.append(egl[:, gi:gi + 1])
        n = len(jobs)
        kk = [_dot_nt(kb[i].astype(BF16), k16[i]) for i in range(n)]
        qk = [_dot_nt(q[i].astype(BF16), k16[i]) for i in range(n)]
        nn = [jnp.where(strict[jobs[i][1]], kk[i] * gam[i], 0.0) for i in range(n)]
        m = [-jnp.where(same_block[GDN_BASE], nn[i], 0.0) for i in range(n)]
        x = [eye + m[i] for i in range(n)]
        for _ in range(2):
            m = [_dot_split(m[i], m[i]) for i in range(n)]
            x = [x[i] + _dot_split(x[i], m[i]) for i in range(n)]
        size = GDN_BASE
        while size < c64:
            e = [jnp.where(same_block[2 * size] & ~same_block[size], nn[i], 0.0)
                 for i in range(n)]
            ex = [_dot_split(e[i], x[i]) for i in range(n)]
            x = [x[i] - _dot_split(x[i], ex[i]) for i in range(n)]
            size *= 2
        rhs = [jnp.concatenate([vb[i], kb[i] * egc_c[i]], axis=1) for i in range(n)]
        sol = [_dot_split(x[i], rhs[i]) for i in range(n)]
        per_chunk = len(chains)
        pick = lambda vals, j: vals[j * per_chunk:(j + 1) * per_chunk]
        return [dict(
            u=[sol[i][:, :GDN_DV] for i in pick(range(n), j)],
            w=[sol[i][:, GDN_DV:].astype(BF16) for i in pick(range(n), j)],
            aqk=[(qk[i] * gam[i]).astype(BF16) for i in pick(range(n), j)],
            qd=[(q[i] * egc_c[i]).astype(BF16) for i in pick(range(n), j)],
            kd=[(k[i] * ekd_c[i]).astype(BF16) for i in pick(range(n), j)],
            egl=pick(egl_c, j)) for j in range(len(cs))]

    def advance(c, pre):
        n = len(chains)
        s = [s_ref[i] for i in range(n)]
        s16 = [s[i].astype(BF16) for i in range(n)]
        ws = [_dot(pre["w"][i], s16[i]) for i in range(n)]
        qs = [_dot(pre["qd"][i], s16[i]) for i in range(n)]
        vn16 = [(pre["u"][i] - ws[i]).astype(BF16) for i in range(n)]
        av = [_dot(pre["aqk"][i], vn16[i]) for i in range(n)]
        kv = [_dot_tn(pre["kd"][i], vn16[i]) for i in range(n)]
        for i, (d, h) in enumerate(chains):
            r0 = row0(c, d)
            s_ref[i] = s[i] * pre["egl"][i] + kv[i]
            refs[d][3][r0:r0 + c64, h * GDN_DV:(h + 1) * GDN_DV] = qs[i] + av[i]

    groups = [list(range(c, min(c + GDN_GROUP, n_chunks))) for c in range(0, n_chunks, GDN_GROUP)]
    pre = prepare(groups[0])
    for gi, cs in enumerate(groups):
        nxt = prepare(groups[gi + 1]) if gi + 1 < len(groups) else None
        for c, one in zip(cs, pre):
            advance(c, one)
        pre = nxt


def _gdn(qkvn, bg, bg_rows, row_off, batch, seqlen, ts):
    c = qkvn.shape[1]
    ns = seqlen // ts
    off = row_off // ts
    fwd = lambda b, s: (off + b * ns + s, 0)
    bwd = lambda b, s: (off + b * ns + ns - 1 - s, 0)
    fwd_r = lambda b, s: (0, off + b * ns + s)
    bwd_r = lambda b, s: (0, off + b * ns + ns - 1 - s)
    nr = bg_rows.shape[0]
    out = jax.ShapeDtypeStruct((batch * seqlen, GDN_V), F32)
    return pl.pallas_call(
        functools.partial(_gdn_body, n_chunks=ts // GDN_CHUNK),
        grid=(batch, ns),
        in_specs=[
            pl.BlockSpec((ts, c), fwd), pl.BlockSpec((ts, V7X_LANES), fwd),
            pl.BlockSpec((nr, ts), fwd_r),
            pl.BlockSpec((ts, c), bwd), pl.BlockSpec((ts, V7X_LANES), bwd),
            pl.BlockSpec((nr, ts), bwd_r),
        ],
        out_specs=[pl.BlockSpec((ts, GDN_V), lambda b, s: (b * ns + s, 0)),
                   pl.BlockSpec((ts, GDN_V), lambda b, s: (b * ns + ns - 1 - s, 0))],
        out_shape=[out, out],
        scratch_shapes=[pltpu.VMEM((2 * GDN_HEADS, GDN_DK, GDN_DV), F32)],
        compiler_params=_cparams(("parallel", "arbitrary")),
        name="gdn",
    )(qkvn, bg, bg_rows, qkvn, bg, bg_rows)


def _attn_body(far_ref, q_ref, k_ref, vt_ref, band_ref, lam_ref, nw_ref, o_ref,
               qz_ref, sf_ref, mxf_ref, sb_ref, mxb_ref, m_ref, acc_ref, *, lam_init, tk, nk):
    far_buf = (sf_ref, mxf_ref)
    band_buf = (sb_ref, mxb_ref)
    h = pl.program_id(1)
    qi = pl.program_id(2)
    tq = q_ref.shape[0]
    sub = V7X_SUBLANES

    m_ref[...] = jnp.full_like(m_ref, -1e30)
    acc_ref[...] = jnp.zeros_like(acc_ref)
    q = q_ref[...]
    lane = lax.broadcasted_iota(jnp.int32, q.shape, 1)
    zero = jnp.zeros_like(q)
    qz_ref[0] = jnp.where(lane < DIFF_DH, q, zero)
    qz_ref[1] = jnp.where(lane >= DIFF_DH, q, zero)

    def scores(kt, buf, slot, bias_tile):
        s_ref, mx_ref = buf
        k = k_ref[pl.ds(pl.multiple_of(kt * tk, tk), tk), :]
        for mp in range(2):
            s = _dot_nt(k, qz_ref[mp])
            if bias_tile is not None:
                s = s + bias_tile
            s_ref[slot, mp] = s
            mx_ref[slot, mp] = jnp.max(s.reshape(tk // sub, sub, tq), axis=0)

    def accumulate(kt, buf, slot, c):
        s_ref, mx_ref = buf
        vt = vt_ref[:, pl.ds(pl.multiple_of(kt * tk, tk), tk)]
        for mp in range(2):
            m_cur = jnp.max(mx_ref[slot, mp], axis=0, keepdims=True) + c
            m_prev = m_ref[mp]
            m_new = jnp.maximum(m_prev, m_cur)
            alpha = jnp.exp2(m_prev - m_new)
            m_ref[mp] = m_new
            p = jnp.exp2(s_ref[slot, mp] - (m_new - c)).astype(BF16)
            acc_ref[mp] = alpha * acc_ref[mp] + _dot(vt, p)

    n_left = jnp.maximum(qi - 1, 0)
    right0 = jnp.minimum(qi + 2, nk)
    n_far = n_left + (nk - right0)

    def far_tile(f):
        return jnp.where(f < n_left, f, right0 + (f - n_left))

    def far_const(f):
        return jnp.where(f < n_left, far_ref[h, 0], far_ref[h, 1])

    band = [(qi, 1), (qi - 1, 0), (qi + 1, 2)]

    def band_scores(j):
        kt, dd = band[j]
        scores(jnp.clip(kt, 0, nk - 1), band_buf, j, band_ref[dd])

    def band_accumulate(j):
        kt, _ = band[j]
        valid = (kt >= 0) & (kt < nk)
        accumulate(jnp.clip(kt, 0, nk - 1), band_buf, j, jnp.where(valid, 0.0, -1e30))

    if nk >= 4:
        scores(far_tile(0), far_buf, 0, None)

        def far_step(f, slot):
            scores(far_tile(f + 1), far_buf, 1 - slot, None)
            accumulate(far_tile(f), far_buf, slot, far_const(f))

        n_steps = n_far - 1

        def far_trip(g, carry):
            for u in range(FAR_UNROLL):
                far_step(FAR_UNROLL * g + u, u % 2)
            return carry

        lax.fori_loop(0, n_steps // FAR_UNROLL, far_trip, 0)
        done = (n_steps // FAR_UNROLL) * FAR_UNROLL
        size = FAR_UNROLL // 2
        while size >= 1:
            taken = (n_steps % (2 * size)) >= size

            @pl.when(taken)
            def _(done=done, size=size):
                for u in range(size):
                    far_step(done + u, u % 2)

            done = done + jnp.where(taken, size, 0)
            size //= 2

        band_scores(0)
        accumulate(far_tile(n_far - 1), far_buf, (n_far - 1) % 2, far_const(n_far - 1))
        band_scores(1)
        band_accumulate(0)
        band_scores(2)
        band_accumulate(1)
        band_accumulate(2)
    else:
        for j in range(3):
            band_scores(j)
            band_accumulate(j)

    lam = lam_ref[...]
    lam_full = (jnp.exp(jnp.sum(lam[0:1] * lam[1:2], keepdims=True))
                - jnp.exp(jnp.sum(lam[2:3] * lam[3:4], keepdims=True)) + lam_init)
    a0 = acc_ref[0]
    a1 = acc_ref[1]
    o = (a0[:DIFF_DV] / a0[DIFF_DV:DIFF_DV + 1]
         - lam_full * (a1[:DIFF_DV] / a1[DIFF_DV:DIFF_DV + 1]))
    o = o * lax.rsqrt(jnp.mean(o * o, axis=0, keepdims=True) + EPS)
    o = o * (nw_ref[...] * (1.0 - lam_init))
    o_ref[...] = o.T


def _attn(qb, kb, vbt, band, far, lam, nw_col, row_off, batch, seqlen, tq, lam_init):
    nq = seqlen // tq
    off = row_off // tq
    off_seq = row_off // seqlen
    assert row_off % seqlen == 0
    return pl.pallas_call(
        functools.partial(_attn_body, lam_init=lam_init, tk=tq, nk=nq),
        grid=(batch, DIFF_HEADS, nq),
        in_specs=[
            pl.BlockSpec(memory_space=pltpu.SMEM),
            pl.BlockSpec((tq, DIFF_DV), lambda b, h, qi: (off + b * nq + qi, h)),
            pl.BlockSpec((seqlen, DIFF_DV), lambda b, h, qi: (off_seq + b, h)),
            pl.BlockSpec((VT_ROWS, seqlen), lambda b, h, qi: (h, off_seq + b)),
            pl.BlockSpec((None, 3, tq, tq), lambda b, h, qi: (h, 0, 0, 0)),
            pl.BlockSpec(lam.shape, lambda b, h, qi: (0, 0)),
            pl.BlockSpec((DIFF_DV, 1), lambda b, h, qi: (0, 0)),
        ],
        out_specs=pl.BlockSpec((tq, DIFF_DV), lambda b, h, qi: (b * nq + qi, h)),
        out_shape=jax.ShapeDtypeStruct((batch * seqlen, DIFF_VW), F32),
        scratch_shapes=[pltpu.VMEM((2, tq, DIFF_DV), BF16),
                        pltpu.VMEM((2, 2, tq, tq), F32),
                        pltpu.VMEM((2, 2, V7X_SUBLANES, tq), F32),
                        pltpu.VMEM((3, 2, tq, tq), F32),
                        pltpu.VMEM((3, 2, V7X_SUBLANES, tq), F32),
                        pltpu.VMEM((2, 1, tq), F32),
                        pltpu.VMEM((2, VT_ROWS, tq), F32)],
        compiler_params=_cparams(("parallel", "parallel", "parallel")),
        name="diffattn",
    )(far, qb, kb, vbt, band, lam, nw_col)


def _memkv_body(m_ref, nw_ref, w_ref, o_ref):
    h = _rms(m_ref[...], nw_ref[...]).astype(BF16)
    o_ref[...] = _dot(h, w_ref[...]).astype(BF16)


def _memkv(mem, nw, wkv):
    nb, nm, d = mem.shape
    return pl.pallas_call(
        _memkv_body,
        grid=(nb,),
        in_specs=[pl.BlockSpec((None, nm, d), lambda b: (b, 0, 0)), _resident((1, d)),
                  _resident(wkv.shape)],
        out_specs=pl.BlockSpec((None, nm, wkv.shape[1]), lambda b: (b, 0, 0)),
        out_shape=jax.ShapeDtypeStruct((nb, nm, wkv.shape[1]), BF16),
        compiler_params=_cparams(("parallel",)),
        name="memkv",
    )(mem, nw, wkv)


def _mergex_body(x_ref, z_ref, sg_ref, kv_ref, gn_ref, wua_ref, wub_ref, wout_ref, xn_ref,
                 wq_ref, wo_ref, *rest, group_tiles):
    o_ref = rest[-1]
    i = pl.program_id(0)
    o = oattn = None
    for g, first in enumerate(group_tiles):
        of_ref, ob_ref, oattn_ref = rest[3 * g:3 * g + 3]
        og = of_ref[...] + ob_ref[...]
        ag = oattn_ref[...]
        o = og if o is None else jnp.where(i >= first, og, o)
        oattn = ag if oattn is None else jnp.where(i >= first, ag, oattn)
    z = z_ref[...]
    gn = gn_ref[...]
    heads = []
    for h in range(GDN_HEADS):
        sl = slice(h * GDN_DV, (h + 1) * GDN_DV)
        zh = z[:, sl]
        heads.append(_rms(o[:, sl], gn) * (zh * jax.nn.sigmoid(zh)))
    oa = jnp.concatenate(heads, axis=1).astype(BF16)
    ya = _dot(oa, wua_ref[...])
    yb = _dot(oattn.astype(BF16), wub_ref[...])
    sg = sg_ref[...]
    merged = sg[:, :D_MODEL] * ya + sg[:, D_MODEL:] * yb
    x = x_ref[...] + _dot(merged.astype(BF16), wout_ref[...])
    hq = _rms(x, xn_ref[...]).astype(BF16)
    q = _dot(hq, wq_ref[...]) * (X_DH ** -0.5)
    kv = kv_ref[...]
    outs = []
    for h in range(X_HEADS):
        sl = slice(h * X_DH, (h + 1) * X_DH)
        kh = kv[:, sl]
        vh = kv[:, X_HEADS * X_DH + h * X_DH:X_HEADS * X_DH + (h + 1) * X_DH]
        s = _dot_nt(q[:, sl].astype(BF16), kh)
        s = s - jnp.max(s, axis=-1, keepdims=True)
        p = jnp.exp(s)
        p = p / jnp.sum(p, axis=-1, keepdims=True)
        outs.append(_dot(p.astype(BF16), vh))
    ox = jnp.concatenate(outs, axis=1).astype(BF16)
    o_ref[...] = x + _dot(ox, wo_ref[...])


def _mergex(x, z, sg, kv, gn, wua, wub, wout, xn, wq, wo, branches, group_rows, tm, tile_batch):
    t, d = x.shape
    rows = lambda width: pl.BlockSpec((tm, width), lambda i: (i, 0))
    in_specs = [rows(d), rows(GDN_V), rows(2 * d),
                pl.BlockSpec((None,) + kv.shape[1:], lambda i: (tile_batch(i), 0, 0)),
                _resident(gn.shape), _resident(wua.shape), _resident(wub.shape),
                _resident(wout.shape), _resident(xn.shape), _resident(wq.shape),
                _resident(wo.shape)]
    args = [x, z, sg, kv, gn, wua, wub, wout, xn, wq, wo]
    for (first_row, n_rows), group in zip(group_rows, branches):
        first, count = first_row // tm, n_rows // tm
        local = lambda i, first=first, count=count: (jnp.clip(i - first, 0, count - 1), 0)
        for arr in group:
            in_specs.append(pl.BlockSpec((tm, arr.shape[1]), local))
            args.append(arr)
    return pl.pallas_call(
        functools.partial(_mergex_body, group_tiles=tuple(r // tm for r, _ in group_rows)),
        grid=(t // tm,),
        in_specs=in_specs,
        out_specs=rows(d),
        out_shape=jax.ShapeDtypeStruct((t, d), F32),
        compiler_params=_cparams(("parallel",)),
        name="mergex",
    )(*args)


def _rel_bucket(rel):
    nb = N_BUCKETS // 2
    ret = jnp.where(rel > 0, nb, 0)
    n = jnp.abs(rel)
    max_exact = nb // 2
    nf = jnp.maximum(n, 1).astype(F32)
    large = max_exact + (jnp.log(nf / max_exact) / math.log(MAX_DISTANCE / max_exact)
                         * (nb - max_exact)).astype(jnp.int32)
    large = jnp.minimum(large, nb - 1)
    return ret + jnp.where(n < max_exact, n, large)


def _toeplitz_body(w_ref, o_ref):
    tq = o_ref.shape[-1]
    x = jnp.broadcast_to(w_ref[...], (tq, 2 * tq))
    o_ref[...] = pltpu.roll(x, 0, 1, stride=1, stride_axis=0)[:, :tq]


def _bias_tables(rel_bias, tq):
    assert tq >= MAX_DISTANCE
    rel = jnp.arange(-(2 * tq - 1), 2 * tq, dtype=jnp.int32)
    by_rel = (rel_bias[_rel_bucket(rel)].astype(F32) * LOG2E).T
    nh = by_rel.shape[0]
    gens = []
    for dd in range(3):
        lo = dd * tq
        gens.append(jnp.concatenate(
            [by_rel[:, lo:lo + tq][:, ::-1], jnp.zeros((nh, 1), F32),
             by_rel[:, lo + tq:lo + 2 * tq - 1][:, ::-1]], axis=1))
    gen = jnp.stack(gens, axis=1)[:, :, None, :]
    band = pl.pallas_call(
        _toeplitz_body,
        grid=(nh, 3),
        in_specs=[pl.BlockSpec((None, None, 1, 2 * tq), lambda h, d: (h, d, 0, 0))],
        out_specs=pl.BlockSpec((None, None, tq, tq), lambda h, d: (h, d, 0, 0)),
        out_shape=jax.ShapeDtypeStruct((nh, 3, tq, tq), F32),
        compiler_params=_cparams(("parallel", "parallel")),
        name="toeplitz",
    )(gen)
    far = jnp.stack([by_rel[:, 0], by_rel[:, -1]], axis=1)
    return band, far


def _pick(limit, n):
    tile = limit
    while n % tile:
        tile //= 2
    return tile


def _encode(xs, mems, p, tile_limits=None):
    lim = dict(ffn=512, proj=256, gdn=512, attn=512, mergex=512)
    if tile_limits:
        lim.update(tile_limits)
    depth = p["w_in"].shape[0]
    d = D_MODEL
    groups = []
    row = 0
    bat = 0
    for x in xs:
        b, l, _ = x.shape
        groups.append(dict(b=b, l=l, row=row, bat=bat))
        row += b * l
        bat += b
    t_all = row
    seqlens = [g["l"] for g in groups]
    common = functools.reduce(math.gcd, seqlens)
    tiles = {k: _pick(v, common) for k, v in lim.items()}

    x = jnp.concatenate([xx.reshape(-1, d) for xx in xs], axis=0)
    mem = jnp.concatenate(mems, axis=0)

    def tile_batch(i):
        r0 = i * tiles["mergex"]
        bidx = 0
        for g in groups:
            bidx = jnp.where(r0 >= g["row"], g["bat"] + (r0 - g["row"]) // g["l"], bidx)
        return bidx

    seq_tiles = tuple((g["row"] // tiles["proj"], g["l"] // tiles["proj"]) for g in groups)
    band, far = _bias_tables(p["rel_bias"], tiles["attn"])

    def ffn_weights(prefix, i):
        return (p[prefix + "_norm"][i].reshape(1, d), p[prefix + "_w_gate"][i].astype(BF16),
                p[prefix + "_w_up"][i].astype(BF16), p[prefix + "_w_down"][i].astype(BF16))

    for i in range(depth):
        x = _ffn(x, *ffn_weights("ffn1", i), None, tiles["ffn"])

        w_in = p["w_in"][i]
        o_beta = 3 * GDN_QK
        o_z = o_beta + 2 * N_GATES
        o_qb = o_z + GDN_V
        gate_w = jnp.pad(w_in[:, o_beta:o_z], ((0, 0), (0, V7X_LANES - 2 * N_GATES)))
        w_perm = jnp.concatenate(
            [w_in[:, :o_beta], gate_w, w_in[:, o_z:o_qb],
             w_in[:, o_qb:o_qb + DIFF_QK] * (DIFF_DH ** -0.5 * LOG2E),
             w_in[:, o_qb + DIFF_QK:]],
            axis=1).astype(BF16)
        assert w_perm.shape[1] == _P_COLS
        conv_w8 = jnp.pad(p["conv_w"][i], ((0, HALO - GDN_CONV), (0, 0)))
        lane_pad = (N_GATES, V7X_LANES - 2 * N_GATES)
        alog_pad = jnp.pad(p["gdn_a_log"][i].reshape(-1), lane_pad).reshape(1, V7X_LANES)
        dtb_pad = jnp.pad(p["gdn_dt_bias"][i].reshape(-1), lane_pad).reshape(1, V7X_LANES)
        qkvn, bg, z, qb, kb, vb, sg = _proj(x, p["mix_norm"][i].reshape(1, d), w_perm, conv_w8,
                                            alog_pad, dtb_pad, tiles["proj"], seq_tiles)
        bg_rows = bg[:, :2 * N_GATES].T

        lam_init = 0.8 - 0.6 * math.exp(-0.3 * i)
        vbt = vb.T.reshape(DIFF_HEADS, DIFF_DV, t_all)
        ones_pad = jnp.zeros((DIFF_HEADS, VT_ROWS - DIFF_DV, t_all), BF16).at[:, 0].set(1.0)
        vbt = jnp.concatenate([vbt, ones_pad], axis=1).reshape(DIFF_HEADS * VT_ROWS, t_all)
        branches = []
        for g in groups:
            o_f, o_b = _gdn(qkvn, bg, bg_rows, g["row"], g["b"], g["l"], tiles["gdn"])
            o_attn = _attn(qb, kb, vbt, band, far, p["diff_lambda"][i],
                           p["diff_norm"][i].reshape(DIFF_DV, 1), g["row"], g["b"], g["l"],
                           tiles["attn"], lam_init)
            branches.append((o_f, o_b, o_attn))

        kv = _memkv(mem, p["mem_norm"][i].reshape(1, d), p["xattn_wkv"][i].astype(BF16))
        x = _mergex(x, z, sg, kv, p["gdn_norm"][i].reshape(1, GDN_DV),
                    p["w_up_a"][i].astype(BF16), p["w_up_b"][i].astype(BF16),
                    p["w_out"][i].astype(BF16), p["xattn_norm"][i].reshape(1, d),
                    p["xattn_wq"][i].astype(BF16), p["xattn_wo"][i].astype(BF16),
                    branches, [(g["row"], g["b"] * g["l"]) for g in groups],
                    tiles["mergex"], tile_batch)

        final_w = p["final_norm"].reshape(1, d) if i == depth - 1 else None
        x = _ffn(x, *ffn_weights("ffn2", i), final_w, tiles["ffn"])

    return tuple(x[g["row"]:g["row"] + g["b"] * g["l"]].reshape(g["b"], g["l"], d)
                 for g in groups)


def kernel(x_prompt, x_sample, mem_prompt, mem_sample, ffn1_norm, ffn1_w_gate, ffn1_w_up, ffn1_w_down, mix_norm, w_in, conv_w, gdn_a_log, gdn_dt_bias, gdn_norm, w_up_a, diff_lambda, diff_norm, w_up_b, w_out, rel_bias, xattn_norm, mem_norm, xattn_wq, xattn_wkv, xattn_wo, ffn2_norm, ffn2_w_gate, ffn2_w_up, ffn2_w_down, final_norm):
    params = dict(
        ffn1_norm=ffn1_norm, ffn1_w_gate=ffn1_w_gate, ffn1_w_up=ffn1_w_up,
        ffn1_w_down=ffn1_w_down, mix_norm=mix_norm, w_in=w_in, conv_w=conv_w,
        gdn_a_log=gdn_a_log, gdn_dt_bias=gdn_dt_bias, gdn_norm=gdn_norm, w_up_a=w_up_a,
        diff_lambda=diff_lambda, diff_norm=diff_norm, w_up_b=w_up_b, w_out=w_out,
        rel_bias=rel_bias, xattn_norm=xattn_norm, mem_norm=mem_norm, xattn_wq=xattn_wq,
        xattn_wkv=xattn_wkv, xattn_wo=xattn_wo, ffn2_norm=ffn2_norm,
        ffn2_w_gate=ffn2_w_gate, ffn2_w_up=ffn2_w_up, ffn2_w_down=ffn2_w_down,
        final_norm=final_norm)
    y_prompt, y_sample = _encode((x_prompt, x_sample), (mem_prompt, mem_sample), params)
    return (y_prompt, y_sample)
```

```python
import functools
import math

import jax
import jax.numpy as jnp
from jax import lax
from jax.experimental import pallas as pl
from jax.experimental.pallas import tpu as pltpu

F32 = jnp.float32
BF16 = jnp.bfloat16
HIGHEST = lax.Precision.HIGHEST

EPS = 1e-6
LOG2E = math.log2(math.e)
D_MODEL = 1024
N_MEM = 256
GDN_HEADS = 4
GDN_DK = 128
GDN_DV = 128
GDN_CONV = 5
GDN_CHUNK = 64
GDN_BASE = 8
GDN_GROUP = 2
DIFF_HEADS = 8
DIFF_DH = 64
DIFF_DV = 2 * DIFF_DH
N_BUCKETS = 32
MAX_DISTANCE = 128
X_HEADS = 4
X_DH = 128
D_FF = 2816
GDN_QK = GDN_HEADS * GDN_DK
GDN_V = GDN_HEADS * GDN_DV
DIFF_QK = DIFF_HEADS * 2 * DIFF_DH
DIFF_VW = DIFF_HEADS * DIFF_DV
N_GATES = 2 * GDN_HEADS
VT_ROWS = DIFF_DV + 16

V7X_LANES = 128
V7X_SUBLANES = 8
V7X_VMEM_LIMIT = 56 * 1024 * 1024

FAR_UNROLL = 8
HALO = V7X_SUBLANES


def _cparams(sem):
    return pltpu.CompilerParams(dimension_semantics=sem, vmem_limit_bytes=V7X_VMEM_LIMIT)


def _dot(a, b, precision=None):
    return jnp.dot(a, b, preferred_element_type=F32, precision=precision)


def _dot_nt(a, b, precision=None):
    return lax.dot_general(a, b, (((1,), (1,)), ((), ())),
                           preferred_element_type=F32, precision=precision)


def _dot_tn(a, b, precision=None):
    return lax.dot_general(a, b, (((0,), (0,)), ((), ())),
                           preferred_element_type=F32, precision=precision)


def _dot_split(a, b):
    a_hi = a.astype(BF16)
    b_hi = b.astype(BF16)
    a_lo = (a - a_hi.astype(F32)).astype(BF16)
    b_lo = (b - b_hi.astype(F32)).astype(BF16)
    m = a.shape[0]
    both = _dot(jnp.concatenate([a_hi, a_lo], axis=0), b_hi)
    return both[:m] + (both[m:] + _dot(a_hi, b_lo))


def _rms(x, w):
    return x * lax.rsqrt(jnp.mean(x * x, axis=-1, keepdims=True) + EPS) * w


def _resident(shape):
    nd = len(shape)
    return pl.BlockSpec(shape, lambda *_: (0,) * nd, pipeline_mode=pl.Buffered(1))


def _ffn_body(x_ref, nw_ref, wg_ref, wu_ref, wd_ref, *rest, final):
    o_ref = rest[-1]
    x = x_ref[...]
    h = _rms(x, nw_ref[...]).astype(BF16)
    g = _dot(h, wg_ref[...])
    u = _dot(h, wu_ref[...])
    a = (g * jax.nn.sigmoid(g) * u).astype(BF16)
    y = x + 0.5 * _dot(a, wd_ref[...])
    if final:
        y = _rms(y, rest[0][...])
    o_ref[...] = y


def _ffn(x, nw, wg, wu, wd, final_w, tm):
    t, d = x.shape
    row = pl.BlockSpec((tm, d), lambda i: (i, 0))
    in_specs = [row, _resident((1, d)), _resident(wg.shape), _resident(wu.shape),
                _resident(wd.shape)]
    args = [x, nw, wg, wu, wd]
    if final_w is not None:
        in_specs.append(_resident((1, d)))
        args.append(final_w)
    return pl.pallas_call(
        functools.partial(_ffn_body, final=final_w is not None),
        grid=(t // tm,),
        in_specs=in_specs,
        out_specs=row,
        out_shape=jax.ShapeDtypeStruct((t, d), F32),
        compiler_params=_cparams(("parallel",)),
        name="ffn",
    )(*args)


_P_QKVA = (0, 3 * GDN_QK)
_P_GATE = (_P_QKVA[1], _P_QKVA[1] + V7X_LANES)
_P_Z = (_P_GATE[1], _P_GATE[1] + GDN_V)
_P_QB = (_P_Z[1], _P_Z[1] + DIFF_QK)
_P_KB = (_P_QB[1], _P_QB[1] + DIFF_QK)
_P_VB = (_P_KB[1], _P_KB[1] + DIFF_VW)
_P_SG = (_P_VB[1], _P_VB[1] + 2 * D_MODEL)
_P_COLS = _P_SG[1]


def _proj_body(x_ref, xprev_ref, xnext_ref, nw_ref, w_ref, cw_ref, alog_ref, dtb_ref,
               qkv_ref, bg_ref, z_ref, qb_ref, kb_ref, vbt_ref, sg_ref, *, tm, seq_tiles):
    i = pl.program_id(0)
    first = jnp.bool_(False)
    last = jnp.bool_(False)
    for start, per_seq in seq_tiles:
        rel = i - start
        first = first | ((rel >= 0) & (rel % per_seq == 0))
        last = last | ((rel >= 0) & (rel % per_seq == per_seq - 1))
    x_ext = jnp.concatenate([xprev_ref[...], x_ref[...], xnext_ref[...]], axis=0)
    u_ext = _rms(x_ext, nw_ref[...]).astype(BF16)
    u = u_ext[HALO:HALO + tm]

    def seg(span):
        return _dot(u, w_ref[:, span[0]:span[1]])

    row = lax.broadcasted_iota(jnp.int32, (tm + 2 * HALO, 1), 0)
    outside = (first & (row < HALO)) | (last & (row >= HALO + tm))
    pad = (GDN_CONV - 1) // 2

    def conv_block(blk):
        width = 2 * GDN_DK
        lo = blk * width
        ext = _dot(u_ext, w_ref[:, _P_QKVA[0] + lo:_P_QKVA[0] + lo + width])
        ext = jnp.where(outside, 0.0, ext)
        acc = None
        for k in range(GDN_CONV):
            r0 = HALO - pad + k
            term = ext[r0:r0 + tm, :] * cw_ref[k:k + 1, lo:lo + width]
            acc = term if acc is None else acc + term
        y = acc * jax.nn.sigmoid(acc)
        for half in range(2):
            head = 2 * blk + half
            yh = y[:, half * GDN_DK:(half + 1) * GDN_DK]
            if head < 2 * GDN_HEADS:
                yh = yh * lax.rsqrt(jnp.sum(yh * yh, axis=-1, keepdims=True) + EPS)
                if head < GDN_HEADS:
                    yh = yh * (GDN_DK ** -0.5)
            qkv_ref[:, head * GDN_DK:(head + 1) * GDN_DK] = yh

    n_blocks = 3 * GDN_HEADS // 2
    plain = [(z_ref, _P_Z, None), (qb_ref, _P_QB, BF16), (kb_ref, _P_KB, BF16),
             (None, _P_VB, BF16)]
    for j, (ref, span, dtype) in enumerate(plain):
        for blk in range(j * n_blocks // len(plain), (j + 1) * n_blocks // len(plain)):
            conv_block(blk)
        val = seg(span)
        val = val if dtype is None else val.astype(dtype)
        if ref is not None:
            ref[...] = val
    pad_row = lax.broadcasted_iota(jnp.int32, (VT_ROWS - DIFF_DV, tm), 0)
    ones_pad = jnp.where(pad_row == 0, 1.0, 0.0).astype(BF16)
    for h in range(DIFF_HEADS):
        vbt_ref[h * VT_ROWS:h * VT_ROWS + DIFF_DV, :] = val[:, h * DIFF_DV:(h + 1) * DIFF_DV].T
        vbt_ref[h * VT_ROWS + DIFF_DV:(h + 1) * VT_ROWS, :] = ones_pad
    sg_ref[...] = jax.nn.sigmoid(seg(_P_SG))
    gl = seg(_P_GATE)
    beta = jax.nn.sigmoid(gl)
    xa = gl + dtb_ref[...]
    softplus = jnp.maximum(xa, 0.0) + jnp.log1p(jnp.exp(-jnp.abs(xa)))
    g = -jnp.exp(alog_ref[...]) * softplus
    lane = lax.broadcasted_iota(jnp.int32, gl.shape, 1)
    bg_ref[...] = jnp.where(lane < N_GATES, beta, g)


def _proj(x, nw, w, conv_w8, alog_pad, dtb_pad, tm, seq_tiles):
    t, d = x.shape
    hb = tm // HALO
    nblk = t // HALO
    widths = [(_P_QKVA, F32), (_P_GATE, F32), (_P_Z, F32), (_P_QB, BF16), (_P_KB, BF16),
              (_P_VB, BF16), (_P_SG, F32)]
    out_shape = [jax.ShapeDtypeStruct((t, s[1] - s[0]), dt) for s, dt in widths]
    out_specs = [pl.BlockSpec((tm, s[1] - s[0]), lambda i: (i, 0)) for s, _ in widths]
    vb_at = [s for s, _ in widths].index(_P_VB)
    out_shape[vb_at] = jax.ShapeDtypeStruct((DIFF_HEADS * VT_ROWS, t), BF16)
    out_specs[vb_at] = pl.BlockSpec((DIFF_HEADS * VT_ROWS, tm), lambda i: (0, i))
    return pl.pallas_call(
        functools.partial(_proj_body, tm=tm, seq_tiles=seq_tiles),
        grid=(t // tm,),
        in_specs=[pl.BlockSpec((tm, d), lambda i: (i, 0)),
                  pl.BlockSpec((HALO, d), lambda i: (jnp.maximum(i * hb - 1, 0), 0)),
                  pl.BlockSpec((HALO, d), lambda i: (jnp.minimum((i + 1) * hb, nblk - 1), 0)),
                  _resident((1, d)), _resident(w.shape), _resident(conv_w8.shape),
                  _resident((1, V7X_LANES)), _resident((1, V7X_LANES))],
        out_specs=out_specs,
        out_shape=out_shape,
        compiler_params=_cparams(("parallel",)),
        name="proj",
    )(x, x, x, nw, w, conv_w8, alog_pad, dtb_pad)


def _gdn_body(qkv_f, bgc_f, bgr_f, qkv_b, bgc_b, bgr_b, of_ref, ob_ref, s_ref, *, n_chunks):
    c64 = GDN_CHUNK

    @pl.when(pl.program_id(1) == 0)
    def _():
        s_ref[...] = jnp.zeros_like(s_ref)

    row = lax.broadcasted_iota(jnp.int32, (c64, c64), 0)
    col = lax.broadcasted_iota(jnp.int32, (c64, c64), 1)
    eye = (row == col).astype(F32)
    incl = (row >= col, row <= col)
    strict = (row > col, row < col)
    same_block = {}
    size = GDN_BASE
    while size <= c64:
        same_block[size] = (row // size) == (col // size)
        size *= 2

    refs = ((qkv_f, bgc_f, bgr_f, of_ref), (qkv_b, bgc_b, bgr_b, ob_ref))
    chains = [(d, h) for d in range(2) for h in range(GDN_HEADS)]

    def row0(c, d):
        return (c if d == 0 else n_chunks - 1 - c) * c64

    def prepare(cs):
        gates = {}
        for c in cs:
            for d in range(2):
                _, bgc_ref, bgr_ref, _ = refs[d]
                r0 = row0(c, d)
                gt = bgc_ref[r0:r0 + c64, :]
                gcs = _dot(incl[d].astype(F32), gt, HIGHEST)
                grs = _dot(bgr_ref[:, r0:r0 + c64], incl[1 - d].astype(F32), HIGHEST)
                g_last = gcs[c64 - 1:c64, :] if d == 0 else gcs[0:1, :]
                gates[c, d] = (gt, gcs, grs, jnp.exp(gcs), jnp.exp(g_last - gcs),
                               jnp.exp(g_last))
        q, k, k16, kb, vb, gam, egc_c, ekd_c, egl_c = ([] for _ in range(9))
        jobs = [(c, d, h) for c in cs for d, h in chains]
        for c, d, h in jobs:
            qkv_ref = refs[d][0]
            r0 = row0(c, d)
            gt, gcs, grs, egc, ekd, egl = gates[c, d]
            idx = d * GDN_HEADS + h
            gi = N_GATES + idx
            qq = qkv_ref[r0:r0 + c64, h * GDN_DK:(h + 1) * GDN_DK]
            kk_ = qkv_ref[r0:r0 + c64, GDN_QK + h * GDN_DK:GDN_QK + (h + 1) * GDN_DK]
            vv = qkv_ref[r0:r0 + c64, 2 * GDN_QK + h * GDN_DV:2 * GDN_QK + (h + 1) * GDN_DV]
            beta = gt[:, idx:idx + 1]
            diff = gcs[:, gi:gi + 1] - grs[gi:gi + 1, :]
            gam.append(jnp.where(incl[d], jnp.exp(jnp.where(incl[d], diff, 0.0)), 0.0))
            q.append(qq)
            k.append(kk_)
            k16.append(kk_.astype(BF16))
            kb.append(kk_ * beta)
            vb.append(vv * beta)
            egc_c.append(egc[:, gi:gi + 1])
            ekd_c.append(ekd[:, gi:gi + 1])
            egl_c.append(egl[:, gi:gi + 1])
        n = len(jobs)
        kk = [_dot_nt(kb[i].astype(BF16), k16[i]) for i in range(n)]
        qk = [_dot_nt(q[i].astype(BF16), k16[i]) for i in range(n)]
        nn = [jnp.where(strict[jobs[i][1]], kk[i] * gam[i], 0.0) for i in range(n)]
        m = [-jnp.where(same_block[GDN_BASE], nn[i], 0.0) for i in range(n)]
        x = [eye + m[i] for i in range(n)]
        for _ in range(2):
            m = [_dot_split(m[i], m[i]) for i in range(n)]
            x = [x[i] + _dot_split(x[i], m[i]) for i in range(n)]
        size = GDN_BASE
        while size < c64:
            e = [jnp.where(same_block[2 * size] & ~same_block[size], nn[i], 0.0)
                 for i in range(n)]
            ex = [_dot_split(e[i], x[i]) for i in range(n)]
            x = [x[i] - _dot_split(x[i], ex[i]) for i in range(n)]
            size *= 2
        rhs = [jnp.concatenate([vb[i], kb[i] * egc_c[i]], axis=1) for i in range(n)]
        sol = [_dot_split(x[i], rhs[i]) for i in range(n)]
        per_chunk = len(chains)
        pick = lambda vals, j: vals[j * per_chunk:(j + 1) * per_chunk]
        return [dict(
            u=[sol[i][:, :GDN_DV] for i in pick(range(n), j)],
            w=[sol[i][:, GDN_DV:].astype(BF16) for i in pick(range(n), j)],
            aqk=[(qk[i] * gam[i]).astype(BF16) for i in pick(range(n), j)],
            qd=[(q[i] * egc_c[i]).astype(BF16) for i in pick(range(n), j)],
            kd=[(k[i] * ekd_c[i]).astype(BF16) for i in pick(range(n), j)],
            egl=pick(egl_c, j)) for j in range(len(cs))]

    def advance(c, pre):
        n = len(chains)
        s = [s_ref[i] for i in range(n)]
        s16 = [s[i].astype(BF16) for i in range(n)]
        ws = [_dot(pre["w"][i], s16[i]) for i in range(n)]
        qs = [_dot(pre["qd"][i], s16[i]) for i in range(n)]
        vn16 = [(pre["u"][i] - ws[i]).astype(BF16) for i in range(n)]
        av = [_dot(pre["aqk"][i], vn16[i]) for i in range(n)]
        kv = [_dot_tn(pre["kd"][i], vn16[i]) for i in range(n)]
        for i, (d, h) in enumerate(chains):
            r0 = row0(c, d)
            s_ref[i] = s[i] * pre["egl"][i] + kv[i]
            refs[d][3][r0:r0 + c64, h * GDN_DV:(h + 1) * GDN_DV] = qs[i] + av[i]

    groups = [list(range(c, min(c + GDN_GROUP, n_chunks))) for c in range(0, n_chunks, GDN_GROUP)]
    pre = prepare(groups[0])
    for gi, cs in enumerate(groups):
        nxt = prepare(groups[gi + 1]) if gi + 1 < len(groups) else None
        for c, one in zip(cs, pre):
            advance(c, one)
        pre = nxt


def _gdn(qkvn, bg, bg_rows, row_off, batch, seqlen, ts):
    c = qkvn.shape[1]
    ns = seqlen // ts
    off = row_off // ts
    fwd = lambda b, s: (off + b * ns + s, 0)
    bwd = lambda b, s: (off + b * ns + ns - 1 - s, 0)
    fwd_r = lambda b, s: (0, off + b * ns + s)
    bwd_r = lambda b, s: (0, off + b * ns + ns - 1 - s)
    nr = bg_rows.shape[0]
    out = jax.ShapeDtypeStruct((batch * seqlen, GDN_V), F32)
    return pl.pallas_call(
        functools.partial(_gdn_body, n_chunks=ts // GDN_CHUNK),
        grid=(batch, ns),
        in_specs=[
            pl.BlockSpec((ts, c), fwd), pl.BlockSpec((ts, V7X_LANES), fwd),
            pl.BlockSpec((nr, ts), fwd_r),
            pl.BlockSpec((ts, c), bwd), pl.BlockSpec((ts, V7X_LANES), bwd),
            pl.BlockSpec((nr, ts), bwd_r),
        ],
        out_specs=[pl.BlockSpec((ts, GDN_V), lambda b, s: (b * ns + s, 0)),
                   pl.BlockSpec((ts, GDN_V), lambda b, s: (b * ns + ns - 1 - s, 0))],
        out_shape=[out, out],
        scratch_shapes=[pltpu.VMEM((2 * GDN_HEADS, GDN_DK, GDN_DV), F32)],
        compiler_params=_cparams(("parallel", "arbitrary")),
        name="gdn",
    )(qkvn, bg, bg_rows, qkvn, bg, bg_rows)


def _attn_body(far_ref, q_ref, k_ref, vt_ref, band_ref, lam_ref, nw_ref, o_ref,
               qz_ref, sf_ref, mxf_ref, sb_ref, mxb_ref, m_ref, acc_ref, *, lam_init, tk, nk):
    far_buf = (sf_ref, mxf_ref)
    band_buf = (sb_ref, mxb_ref)
    h = pl.program_id(1)
    qi = pl.program_id(2)
    tq = q_ref.shape[0]
    sub = V7X_SUBLANES

    m_ref[...] = jnp.full_like(m_ref, -1e30)
    acc_ref[...] = jnp.zeros_like(acc_ref)
    q = q_ref[...]
    lane = lax.broadcasted_iota(jnp.int32, q.shape, 1)
    zero = jnp.zeros_like(q)
    qz_ref[0] = jnp.where(lane < DIFF_DH, q, zero)
    qz_ref[1] = jnp.where(lane >= DIFF_DH, q, zero)

    def scores(kt, buf, slot, bias_tile):
        s_ref, mx_ref = buf
        k = k_ref[pl.ds(pl.multiple_of(kt * tk, tk), tk), :]
        for mp in range(2):
            s = _dot_nt(k, qz_ref[mp])
            if bias_tile is not None:
                s = s + bias_tile
            s_ref[slot, mp] = s
            mx_ref[slot, mp] = jnp.max(s.reshape(tk // sub, sub, tq), axis=0)

    def accumulate(kt, buf, slot, c):
        s_ref, mx_ref = buf
        vt = vt_ref[:, pl.ds(pl.multiple_of(kt * tk, tk), tk)]
        for mp in range(2):
            m_cur = jnp.max(mx_ref[slot, mp], axis=0, keepdims=True) + c
            m_prev = m_ref[mp]
            m_new = jnp.maximum(m_prev, m_cur)
            alpha = jnp.exp2(m_prev - m_new)
            m_ref[mp] = m_new
            p = jnp.exp2(s_ref[slot, mp] - (m_new - c)).astype(BF16)
            acc_ref[mp] = alpha * acc_ref[mp] + _dot(vt, p)

    n_left = jnp.maximum(qi - 1, 0)
    right0 = jnp.minimum(qi + 2, nk)
    n_far = n_left + (nk - right0)

    def far_tile(f):
        return jnp.where(f < n_left, f, right0 + (f - n_left))

    def far_const(f):
        return jnp.where(f < n_left, far_ref[h, 0], far_ref[h, 1])

    band = [(qi, 1), (qi - 1, 0), (qi + 1, 2)]

    def band_scores(j):
        kt, dd = band[j]
        scores(jnp.clip(kt, 0, nk - 1), band_buf, j, band_ref[dd])

    def band_accumulate(j):
        kt, _ = band[j]
        valid = (kt >= 0) & (kt < nk)
        accumulate(jnp.clip(kt, 0, nk - 1), band_buf, j, jnp.where(valid, 0.0, -1e30))

    if nk >= 4:
        scores(far_tile(0), far_buf, 0, None)

        def far_step(f, slot):
            scores(far_tile(f + 1), far_buf, 1 - slot, None)
            accumulate(far_tile(f), far_buf, slot, far_const(f))

        n_steps = n_far - 1

        def far_trip(g, carry):
            for u in range(FAR_UNROLL):
                far_step(FAR_UNROLL * g + u, u % 2)
            return carry

        lax.fori_loop(0, n_steps // FAR_UNROLL, far_trip, 0)
        done = (n_steps // FAR_UNROLL) * FAR_UNROLL
        size = FAR_UNROLL // 2
        while size >= 1:
            taken = (n_steps % (2 * size)) >= size

            @pl.when(taken)
            def _(done=done, size=size):
                for u in range(size):
                    far_step(done + u, u % 2)

            done = done + jnp.where(taken, size, 0)
            size //= 2

        band_scores(0)
        accumulate(far_tile(n_far - 1), far_buf, (n_far - 1) % 2, far_const(n_far - 1))
        band_scores(1)
        band_accumulate(0)
        band_scores(2)
        band_accumulate(1)
        band_accumulate(2)
    else:
        for j in range(3):
            band_scores(j)
            band_accumulate(j)

    lam = lam_ref[...]
    lam_full = (jnp.exp(jnp.sum(lam[0:1] * lam[1:2], keepdims=True))
                - jnp.exp(jnp.sum(lam[2:3] * lam[3:4], keepdims=True)) + lam_init)
    a0 = acc_ref[0]
    a1 = acc_ref[1]
    o = (a0[:DIFF_DV] / a0[DIFF_DV:DIFF_DV + 1]
         - lam_full * (a1[:DIFF_DV] / a1[DIFF_DV:DIFF_DV + 1]))
    o = o * lax.rsqrt(jnp.mean(o * o, axis=0, keepdims=True) + EPS)
    o = o * (nw_ref[...] * (1.0 - lam_init))
    o_ref[...] = o.T


def _attn(qb, kb, vbt, band, far, lam, nw_col, row_off, batch, seqlen, tq, lam_init):
    nq = seqlen // tq
    off = row_off // tq
    off_seq = row_off // seqlen
    assert row_off % seqlen == 0
    return pl.pallas_call(
        functools.partial(_attn_body, lam_init=lam_init, tk=tq, nk=nq),
        grid=(batch, DIFF_HEADS, nq),
        in_specs=[
            pl.BlockSpec(memory_space=pltpu.SMEM),
            pl.BlockSpec((tq, DIFF_DV), lambda b, h, qi: (off + b * nq + qi, h)),
            pl.BlockSpec((seqlen, DIFF_DV), lambda b, h, qi: (off_seq + b, h)),
            pl.BlockSpec((VT_ROWS, seqlen), lambda b, h, qi: (h, off_seq + b)),
            pl.BlockSpec((None, 3, tq, tq), lambda b, h, qi: (h, 0, 0, 0)),
            pl.BlockSpec(lam.shape, lambda b, h, qi: (0, 0)),
            pl.BlockSpec((DIFF_DV, 1), lambda b, h, qi: (0, 0)),
        ],
        out_specs=pl.BlockSpec((tq, DIFF_DV), lambda b, h, qi: (b * nq + qi, h)),
        out_shape=jax.ShapeDtypeStruct((batch * seqlen, DIFF_VW), F32),
        scratch_shapes=[pltpu.VMEM((2, tq, DIFF_DV), BF16),
                        pltpu.VMEM((2, 2, tq, tq), F32),
                        pltpu.VMEM((2, 2, V7X_SUBLANES, tq), F32),
                        pltpu.VMEM((3, 2, tq, tq), F32),
                        pltpu.VMEM((3, 2, V7X_SUBLANES, tq), F32),
                        pltpu.VMEM((2, 1, tq), F32),
                        pltpu.VMEM((2, VT_ROWS, tq), F32)],
        compiler_params=_cparams(("parallel", "parallel", "parallel")),
        name="diffattn",
    )(far, qb, kb, vbt, band, lam, nw_col)


def _memkv_body(m_ref, nw_ref, w_ref, o_ref):
    h = _rms(m_ref[...], nw_ref[...]).astype(BF16)
    o_ref[...] = _dot(h, w_ref[...]).astype(BF16)


def _memkv(mem, nw, wkv):
    nb, nm, d = mem.shape
    return pl.pallas_call(
        _memkv_body,
        grid=(nb,),
        in_specs=[pl.BlockSpec((None, nm, d), lambda b: (b, 0, 0)), _resident((1, d)),
                  _resident(wkv.shape)],
        out_specs=pl.BlockSpec((None, nm, wkv.shape[1]), lambda b: (b, 0, 0)),
        out_shape=jax.ShapeDtypeStruct((nb, nm, wkv.shape[1]), BF16),
        compiler_params=_cparams(("parallel",)),
        name="memkv",
    )(mem, nw, wkv)


def _mergex_body(x_ref, z_ref, sg_ref, kv_ref, gn_ref, wua_ref, wub_ref, wout_ref, xn_ref,
                 wq_ref, wo_ref, *rest, group_tiles):
    o_ref = rest[-1]
    i = pl.program_id(0)
    o = oattn = None
    for g, first in enumerate(group_tiles):
        of_ref, ob_ref, oattn_ref = rest[3 * g:3 * g + 3]
        og = of_ref[...] + ob_ref[...]
        ag = oattn_ref[...]
        o = og if o is None else jnp.where(i >= first, og, o)
        oattn = ag if oattn is None else jnp.where(i >= first, ag, oattn)
    z = z_ref[...]
    gn = gn_ref[...]
    heads = []
    for h in range(GDN_HEADS):
        sl = slice(h * GDN_DV, (h + 1) * GDN_DV)
        zh = z[:, sl]
        heads.append(_rms(o[:, sl], gn) * (zh * jax.nn.sigmoid(zh)))
    oa = jnp.concatenate(heads, axis=1).astype(BF16)
    ya = _dot(oa, wua_ref[...])
    yb = _dot(oattn.astype(BF16), wub_ref[...])
    sg = sg_ref[...]
    merged = sg[:, :D_MODEL] * ya + sg[:, D_MODEL:] * yb
    x = x_ref[...] + _dot(merged.astype(BF16), wout_ref[...])
    hq = _rms(x, xn_ref[...]).astype(BF16)
    q = _dot(hq, wq_ref[...]) * (X_DH ** -0.5)
    kv = kv_ref[...]
    outs = []
    for h in range(X_HEADS):
        sl = slice(h * X_DH, (h + 1) * X_DH)
        kh = kv[:, sl]
        vh = kv[:, X_HEADS * X_DH + h * X_DH:X_HEADS * X_DH + (h + 1) * X_DH]
        s = _dot_nt(q[:, sl].astype(BF16), kh)
        s = s - jnp.max(s, axis=-1, keepdims=True)
        p = jnp.exp(s)
        p = p / jnp.sum(p, axis=-1, keepdims=True)
        outs.append(_dot(p.astype(BF16), vh))
    ox = jnp.concatenate(outs, axis=1).astype(BF16)
    o_ref[...] = x + _dot(ox, wo_ref[...])


def _mergex(x, z, sg, kv, gn, wua, wub, wout, xn, wq, wo, branches, group_rows, tm, tile_batch):
    t, d = x.shape
    rows = lambda width: pl.BlockSpec((tm, width), lambda i: (i, 0))
    in_specs = [rows(d), rows(GDN_V), rows(2 * d),
                pl.BlockSpec((None,) + kv.shape[1:], lambda i: (tile_batch(i), 0, 0)),
                _resident(gn.shape), _resident(wua.shape), _resident(wub.shape),
                _resident(wout.shape), _resident(xn.shape), _resident(wq.shape),
                _resident(wo.shape)]
    args = [x, z, sg, kv, gn, wua, wub, wout, xn, wq, wo]
    for (first_row, n_rows), group in zip(group_rows, branches):
        first, count = first_row // tm, n_rows // tm
        local = lambda i, first=first, count=count: (jnp.clip(i - first, 0, count - 1), 0)
        for arr in group:
            in_specs.append(pl.BlockSpec((tm, arr.shape[1]), local))
            args.append(arr)
    return pl.pallas_call(
        functools.partial(_mergex_body, group_tiles=tuple(r // tm for r, _ in group_rows)),
        grid=(t // tm,),
        in_specs=in_specs,
        out_specs=rows(d),
        out_shape=jax.ShapeDtypeStruct((t, d), F32),
        compiler_params=_cparams(("parallel",)),
        name="mergex",
    )(*args)


def _rel_bucket(rel):
    nb = N_BUCKETS // 2
    ret = jnp.where(rel > 0, nb, 0)
    n = jnp.abs(rel)
    max_exact = nb // 2
    nf = jnp.maximum(n, 1).astype(F32)
    large = max_exact + (jnp.log(nf / max_exact) / math.log(MAX_DISTANCE / max_exact)
                         * (nb - max_exact)).astype(jnp.int32)
    large = jnp.minimum(large, nb - 1)
    return ret + jnp.where(n < max_exact, n, large)


def _toeplitz_body(w_ref, o_ref):
    tq = o_ref.shape[-1]
    x = jnp.broadcast_to(w_ref[...], (tq, 2 * tq))
    o_ref[...] = pltpu.roll(x, 0, 1, stride=1, stride_axis=0)[:, :tq]


def _bias_tables(rel_bias, tq):
    assert tq >= MAX_DISTANCE
    rel = jnp.arange(-(2 * tq - 1), 2 * tq, dtype=jnp.int32)
    by_rel = (rel_bias[_rel_bucket(rel)].astype(F32) * LOG2E).T
    nh = by_rel.shape[0]
    gens = []
    for dd in range(3):
        lo = dd * tq
        gens.append(jnp.concatenate(
            [by_rel[:, lo:lo + tq][:, ::-1], jnp.zeros((nh, 1), F32),
             by_rel[:, lo + tq:lo + 2 * tq - 1][:, ::-1]], axis=1))
    gen = jnp.stack(gens, axis=1)[:, :, None, :]
    band = pl.pallas_call(
        _toeplitz_body,
        grid=(nh, 3),
        in_specs=[pl.BlockSpec((None, None, 1, 2 * tq), lambda h, d: (h, d, 0, 0))],
        out_specs=pl.BlockSpec((None, None, tq, tq), lambda h, d: (h, d, 0, 0)),
        out_shape=jax.ShapeDtypeStruct((nh, 3, tq, tq), F32),
        compiler_params=_cparams(("parallel", "parallel")),
        name="toeplitz",
    )(gen)
    far = jnp.stack([by_rel[:, 0], by_rel[:, -1]], axis=1)
    return band, far


def _pick(limit, n):
    tile = limit
    while n % tile:
        tile //= 2
    return tile


def _encode(xs, mems, p, tile_limits=None):
    lim = dict(ffn=512, proj=256, gdn=512, attn=512, mergex=512)
    if tile_limits:
        lim.update(tile_limits)
    depth = p["w_in"].shape[0]
    d = D_MODEL
    groups = []
    row = 0
    bat = 0
    for x in xs:
        b, l, _ = x.shape
        groups.append(dict(b=b, l=l, row=row, bat=bat))
        row += b * l
        bat += b
    t_all = row
    seqlens = [g["l"] for g in groups]
    common = functools.reduce(math.gcd, seqlens)
    tiles = {k: _pick(v, common) for k, v in lim.items()}

    x = jnp.concatenate([xx.reshape(-1, d) for xx in xs], axis=0)
    mem = jnp.concatenate(mems, axis=0)

    def tile_batch(i):
        r0 = i * tiles["mergex"]
        bidx = 0
        for g in groups:
            bidx = jnp.where(r0 >= g["row"], g["bat"] + (r0 - g["row"]) // g["l"], bidx)
        return bidx

    seq_tiles = tuple((g["row"] // tiles["proj"], g["l"] // tiles["proj"]) for g in groups)
    band, far = _bias_tables(p["rel_bias"], tiles["attn"])

    def ffn_weights(prefix, i):
        return (p[prefix + "_norm"][i].reshape(1, d), p[prefix + "_w_gate"][i].astype(BF16),
                p[prefix + "_w_up"][i].astype(BF16), p[prefix + "_w_down"][i].astype(BF16))

    for i in range(depth):
        x = _ffn(x, *ffn_weights("ffn1", i), None, tiles["ffn"])

        w_in = p["w_in"][i]
        o_beta = 3 * GDN_QK
        o_z = o_beta + 2 * N_GATES
        o_qb = o_z + GDN_V
        gate_w = jnp.pad(w_in[:, o_beta:o_z], ((0, 0), (0, V7X_LANES - 2 * N_GATES)))
        w_perm = jnp.concatenate(
            [w_in[:, :o_beta], gate_w, w_in[:, o_z:o_qb],
             w_in[:, o_qb:o_qb + DIFF_QK] * (DIFF_DH ** -0.5 * LOG2E),
             w_in[:, o_qb + DIFF_QK:]],
            axis=1).astype(BF16)
        assert w_perm.shape[1] == _P_COLS
        conv_w8 = jnp.pad(p["conv_w"][i], ((0, HALO - GDN_CONV), (0, 0)))
        lane_pad = (N_GATES, V7X_LANES - 2 * N_GATES)
        alog_pad = jnp.pad(p["gdn_a_log"][i].reshape(-1), lane_pad).reshape(1, V7X_LANES)
        dtb_pad = jnp.pad(p["gdn_dt_bias"][i].reshape(-1), lane_pad).reshape(1, V7X_LANES)
        qkvn, bg, z, qb, kb, vbt, sg = _proj(x, p["mix_norm"][i].reshape(1, d), w_perm, conv_w8,
                                             alog_pad, dtb_pad, tiles["proj"], seq_tiles)
        bg_rows = bg[:, :2 * N_GATES].T

        lam_init = 0.8 - 0.6 * math.exp(-0.3 * i)
        branches = []
        for g in groups:
            o_f, o_b = _gdn(qkvn, bg, bg_rows, g["row"], g["b"], g["l"], tiles["gdn"])
            o_attn = _attn(qb, kb, vbt, band, far, p["diff_lambda"][i],
                           p["diff_norm"][i].reshape(DIFF_DV, 1), g["row"], g["b"], g["l"],
                           tiles["attn"], lam_init)
            branches.append((o_f, o_b, o_attn))

        kv = _memkv(mem, p["mem_norm"][i].reshape(1, d), p["xattn_wkv"][i].astype(BF16))
        x = _mergex(x, z, sg, kv, p["gdn_norm"][i].reshape(1, GDN_DV),
                    p["w_up_a"][i].astype(BF16), p["w_up_b"][i].astype(BF16),
                    p["w_out"][i].astype(BF16), p["xattn_norm"][i].reshape(1, d),
                    p["xattn_wq"][i].astype(BF16), p["xattn_wo"][i].astype(BF16),
                    branches, [(g["row"], g["b"] * g["l"]) for g in groups],
                    tiles["mergex"], tile_batch)

        final_w = p["final_norm"].reshape(1, d) if i == depth - 1 else None
        x = _ffn(x, *ffn_weights("ffn2", i), final_w, tiles["ffn"])

    return tuple(x[g["row"]:g["row"] + g["b"] * g["l"]].reshape(g["b"], g["l"], d)
                 for g in groups)


def kernel(x_prompt, x_sample, mem_prompt, mem_sample, ffn1_norm, ffn1_w_gate, ffn1_w_up, ffn1_w_down, mix_norm, w_in, conv_w, gdn_a_log, gdn_dt_bias, gdn_norm, w_up_a, diff_lambda, diff_norm, w_up_b, w_out, rel_bias, xattn_norm, mem_norm, xattn_wq, xattn_wkv, xattn_wo, ffn2_norm, ffn2_w_gate, ffn2_w_up, ffn2_w_down, final_norm):
    params = dict(
        ffn1_norm=ffn1_norm, ffn1_w_gate=ffn1_w_gate, ffn1_w_up=ffn1_w_up,
        ffn1_w_down=ffn1_w_down, mix_norm=mix_norm, w_in=w_in, conv_w=conv_w,
        gdn_a_log=gdn_a_log, gdn_dt_bias=gdn_dt_bias, gdn_norm=gdn_norm, w_up_a=w_up_a,
        diff_lambda=diff_lambda, diff_norm=diff_norm, w_up_b=w_up_b, w_out=w_out,
        rel_bias=rel_bias, xattn_norm=xattn_norm, mem_norm=mem_norm, xattn_wq=xattn_wq,
        xattn_wkv=xattn_wkv, xattn_wo=xattn_wo, ffn2_norm=ffn2_norm,
        ffn2_w_gate=ffn2_w_gate, ffn2_w_up=ffn2_w_up, ffn2_w_down=ffn2_w_down,
        final_norm=final_norm)
    y_prompt, y_sample = _encode((x_prompt, x_sample), (mem_prompt, mem_sample), params)
    return (y_prompt, y_sample)
```
